```python
import jax, jax.numpy as jnp
from jax import lax
import numpy as np

D_MODEL = 1024
BATCH = 2
SEQ = 8192
DEPTH = 1
DEC_BATCH = 128
DEC_SEQ = 4
PAST_LEN = 2048
PAGE_SIZE = 128

D_MIX = D_MODEL
D_ATT = D_MIX // 2
H_ATT = 8
HD_ATT = D_ATT // H_ATT
D_SSM = D_MIX - D_ATT
P_SSM = 64
H_SSM = D_SSM // P_SSM
N_STATE = 128
N_BC_GROUPS = 2
HEADS_PER_BC_GROUP = H_SSM // N_BC_GROUPS
CONV_WIDTH = 4
CONV_CH = D_SSM + 2 * N_BC_GROUPS * N_STATE
SSD_CHUNK = 128
Q_BLOCK = 128
N_MEM = 256
H_X = 4
HD_X = D_MODEL // H_X
D_FF = -(-(8 * D_MODEL) // (3 * 256)) * 256
IN_COLS = 3 * D_ATT + H_ATT + D_SSM + CONV_CH + H_SSM
EPS = 1e-6
ATT_SCALE = HD_ATT ** -0.5
X_SCALE = HD_X ** -0.5

kernel_name = 'hymba_fox_ssd_decode_step'


def rms_norm(x, g):
    xf = x.astype(jnp.float32)
    xf = xf * lax.rsqrt(jnp.mean(xf * xf, axis=-1, keepdims=True) + EPS)
    return (xf * g.astype(jnp.float32)).astype(x.dtype)


def in_projection(h, w_in, b_forget):
    b, l, _ = h.shape
    proj = jnp.einsum('bld,dc->blc', h, w_in)
    cuts = [D_ATT, 2 * D_ATT, 3 * D_ATT, 3 * D_ATT + H_ATT,
            3 * D_ATT + H_ATT + D_SSM, 3 * D_ATT + H_ATT + D_SSM + CONV_CH]
    q, k, v, f_raw, z, xbc, dt_raw = jnp.split(proj, cuts, axis=-1)
    q = q.reshape(b, l, H_ATT, HD_ATT)
    k = k.reshape(b, l, H_ATT, HD_ATT)
    v = v.reshape(b, l, H_ATT, HD_ATT)
    logf = jax.nn.log_sigmoid((f_raw + b_forget).astype(jnp.float32))
    return q, k, v, logf, z, xbc, dt_raw


def fox_attend(q, c_q, k, v, c_k, mask):
    s = jnp.einsum('bqhd,bkhd->bhqk', q, k).astype(jnp.float32) * ATT_SCALE
    bias = jnp.swapaxes(c_q, 1, 2)[:, :, :, None] - jnp.swapaxes(c_k, 1, 2)[:, :, None, :]
    s = jnp.where(mask, s + bias, -jnp.inf)
    p = jax.nn.softmax(s, axis=-1)
    return jnp.einsum('bhqk,bkhd->bqhd', p.astype(v.dtype), v)


def fox_prompt(q, k, v, logf):
    b, l, h, d = q.shape
    c = jnp.cumsum(logf, axis=1)
    kpos = jnp.arange(l)

    def block(i):
        start = i * Q_BLOCK
        q_blk = lax.dynamic_slice_in_dim(q, start, Q_BLOCK, axis=1)
        c_blk = lax.dynamic_slice_in_dim(c, start, Q_BLOCK, axis=1)
        qpos = start + jnp.arange(Q_BLOCK)
        mask = qpos[:, None] >= kpos[None, :]
        return fox_attend(q_blk, c_blk, k, v, c, mask)

    out = lax.map(block, jnp.arange(l // Q_BLOCK))
    return jnp.moveaxis(out, 0, 1).reshape(b, l, h * d)


def fox_sample(q, k_new, v_new, logf_new, cache_k, cache_v, cache_logf, page_table):
    db, l = q.shape[:2]
    past = page_table.shape[1] * PAGE_SIZE
    k_past = cache_k[page_table].reshape(db, past, H_ATT, HD_ATT)
    v_past = cache_v[page_table].reshape(db, past, H_ATT, HD_ATT)
    logf_past = cache_logf[page_table].reshape(db, past, H_ATT).astype(jnp.float32)
    c_past = jnp.cumsum(logf_past, axis=1)
    c_new = c_past[:, -1:] + jnp.cumsum(logf_new, axis=1)
    k_all = jnp.concatenate([k_past.astype(k_new.dtype), k_new], axis=1)
    v_all = jnp.concatenate([v_past.astype(v_new.dtype), v_new], axis=1)
    c_all = jnp.concatenate([c_past, c_new], axis=1)
    qpos = past + jnp.arange(l)
    kpos = jnp.arange(past + l)
    mask = qpos[:, None] >= kpos[None, :]
    return fox_attend(q, c_new, k_all, v_all, c_all, mask).reshape(db, l, D_ATT)


def ssd_scan(xdt, a, bmat, cmat, h0):
    b, l, h, p = xdt.shape
    n = bmat.shape[-1]
    q = min(SSD_CHUNK, l)
    pad = (-l) % q
    if pad:
        xdt = jnp.pad(xdt, ((0, 0), (0, pad), (0, 0), (0, 0)))
        bmat = jnp.pad(bmat, ((0, 0), (0, pad), (0, 0), (0, 0)))
        cmat = jnp.pad(cmat, ((0, 0), (0, pad), (0, 0), (0, 0)))
        a = jnp.pad(a, ((0, 0), (0, pad), (0, 0)))
    nc = (l + pad) // q
    xdt = xdt.reshape(b, nc, q, h, p)
    bmat = bmat.reshape(b, nc, q, h, n)
    cmat = cmat.reshape(b, nc, q, h, n)
    a_cs = jnp.cumsum(a.reshape(b, nc, q, h), axis=2)
    causal = jnp.tril(jnp.ones((q, q), dtype=bool))
    seg = a_cs[:, :, :, None, :] - a_cs[:, :, None, :, :]
    decay_ts = jnp.exp(jnp.where(causal[None, None, :, :, None], seg, -jnp.inf))
    g = jnp.einsum('bcthn,bcshn->bctsh', cmat, bmat) * decay_ts
    y_diag = jnp.einsum('bctsh,bcshp->bcthp', g, xdt)
    decay_to_end = jnp.exp(a_cs[:, :, -1:, :] - a_cs)
    s_chunk = jnp.einsum('bcshn,bcsh,bcshp->bchpn', bmat, decay_to_end, xdt).astype(jnp.float32)
    chunk_decay = jnp.exp(a_cs[:, :, -1, :])

    def step(state, inp):
        s_c, d_c = inp
        return d_c[:, :, None, None] * state + s_c, state

    h_final, h_start = lax.scan(step, h0.astype(jnp.float32),
                                (jnp.moveaxis(s_chunk, 1, 0), jnp.moveaxis(chunk_decay, 1, 0)))
    h_start = jnp.moveaxis(h_start, 0, 1)
    y_off = jnp.einsum('bcthn,bchpn,bcth->bcthp', cmat, h_start, jnp.exp(a_cs))
    y = (y_diag + y_off).reshape(b, nc * q, h, p)[:, :l]
    return y, h_final


def ssd_branch(z, xbc, dt_raw, conv_buf, h0, conv_w, conv_b, dt_bias, a_log, d_skip, norm_g):
    b, l, _ = xbc.shape
    xpad = jnp.concatenate([conv_buf.astype(xbc.dtype), xbc], axis=1)
    acc = conv_b
    for tap in range(CONV_WIDTH):
        acc = acc + xpad[:, tap:tap + l] * conv_w[tap]
    new_buf = xpad[:, l:]
    act = jax.nn.silu(acc)
    xs = act[..., :D_SSM].reshape(b, l, H_SSM, P_SSM)
    bm = act[..., D_SSM:D_SSM + N_BC_GROUPS * N_STATE].reshape(b, l, N_BC_GROUPS, N_STATE)
    cm = act[..., D_SSM + N_BC_GROUPS * N_STATE:].reshape(b, l, N_BC_GROUPS, N_STATE)
    bm = jnp.repeat(bm, HEADS_PER_BC_GROUP, axis=2)
    cm = jnp.repeat(cm, HEADS_PER_BC_GROUP, axis=2)
    dt = jax.nn.softplus((dt_raw + dt_bias).astype(jnp.float32))
    a = -jnp.exp(a_log.astype(jnp.float32)) * dt
    y, h_new = ssd_scan(xs * dt[..., None], a, bm, cm, h0)
    y = y + d_skip[:, None] * xs
    y = y.reshape(b, l, D_SSM) * jax.nn.silu(z)
    return rms_norm(y, norm_g), new_buf, h_new


def mixer_sublayer(x, attend, conv_buf, h0, norm_g, w_in, b_forget, conv_w, conv_b,
                   dt_bias, a_log, d_skip, ssm_norm_g, w_out):
    h = rms_norm(x, norm_g)
    q, k, v, logf, z, xbc, dt_raw = in_projection(h, w_in, b_forget)
    att = attend(q, k, v, logf)
    ssm, conv_new, h_new = ssd_branch(z, xbc, dt_raw, conv_buf, h0, conv_w, conv_b,
                                      dt_bias, a_log, d_skip, ssm_norm_g)
    mixed = jnp.concatenate([att, ssm], axis=-1)
    x = x + jnp.einsum('blc,cd->bld', mixed, w_out).astype(x.dtype)
    return x, k, v, logf, conv_new, h_new


def memory_kv(mem, g, w_ck, w_cv):
    b, m, _ = mem.shape
    mn = rms_norm(mem, g)
    k = jnp.einsum('bmd,de->bme', mn, w_ck).reshape(b, m, H_X, HD_X)
    v = jnp.einsum('bmd,de->bme', mn, w_cv).reshape(b, m, H_X, HD_X)
    return k, v


def cross_sublayer(x, mem_k, mem_v, norm_g, w_cq, w_co):
    b, l, _ = x.shape
    h = rms_norm(x, norm_g)
    q = jnp.einsum('bld,de->ble', h, w_cq).reshape(b, l, H_X, HD_X)
    s = jnp.einsum('bqhd,bmhd->bhqm', q, mem_k.astype(q.dtype)).astype(jnp.float32) * X_SCALE
    p = jax.nn.softmax(s, axis=-1)
    o = jnp.einsum('bhqm,bmhd->bqhd', p.astype(q.dtype), mem_v.astype(q.dtype)).reshape(b, l, D_MODEL)
    return x + jnp.einsum('ble,ed->bld', o, w_co).astype(x.dtype)


def ffn_sublayer(x, norm_g, w_gate, w_up, w_down):
    h = rms_norm(x, norm_g)
    u = jax.nn.silu(jnp.einsum('bld,df->blf', h, w_gate)) * jnp.einsum('bld,df->blf', h, w_up)
    return x + jnp.einsum('blf,fd->bld', u, w_down).astype(x.dtype)


def setup_inputs(seed: int = 0) -> dict:
    key = jax.random.key(seed)
    keys = list(jax.random.split(key, 48))

    def nxt():
        return keys.pop()

    f32 = jnp.float32

    def normal(shape, scale=1.0):
        return jax.random.normal(nxt(), shape, f32) * scale

    def gain(shape):
        return 1.0 + 0.02 * normal(shape)

    n_pages = PAST_LEN // PAGE_SIZE
    n_used = DEC_BATCH * n_pages
    n_phys = n_used + max(n_used // 4, 1)

    x_prompt = normal((BATCH, SEQ, D_MODEL))
    x_sample = normal((DEC_BATCH, DEC_SEQ, D_MODEL))
    mem_prompt = normal((BATCH, N_MEM, D_MODEL))
    cache_k = normal((DEPTH, n_phys, PAGE_SIZE, H_ATT, HD_ATT))
    cache_v = normal((DEPTH, n_phys, PAGE_SIZE, H_ATT, HD_ATT))
    cache_logf = jax.nn.log_sigmoid(normal((DEPTH, n_phys, PAGE_SIZE, H_ATT)) + 2.0)
    page_table = jax.random.permutation(nxt(), n_phys)[:n_used].reshape(DEC_BATCH, n_pages).astype(jnp.int32)
    cache_mem_k = normal((DEPTH, DEC_BATCH, N_MEM, H_X, HD_X))
    cache_mem_v = normal((DEPTH, DEC_BATCH, N_MEM, H_X, HD_X))
    state_conv = normal((DEPTH, DEC_BATCH, CONV_WIDTH - 1, CONV_CH))
    state_ssm = normal((DEPTH, DEC_BATCH, H_SSM, P_SSM, N_STATE), 0.1)

    norm_mix_g = gain((DEPTH, D_MODEL))
    w_in = normal((DEPTH, D_MODEL, IN_COLS), D_MODEL ** -0.5)
    b_forget = jax.random.uniform(nxt(), (DEPTH, H_ATT), f32, 0.5, 3.0)
    conv_w = normal((DEPTH, CONV_WIDTH, CONV_CH), CONV_WIDTH ** -0.5)
    conv_b = normal((DEPTH, CONV_CH), 0.02)
    dt0 = jnp.exp(jax.random.uniform(nxt(), (DEPTH, H_SSM), f32, float(np.log(1e-3)), float(np.log(1e-1))))
    dt_bias = dt0 + jnp.log(-jnp.expm1(-dt0))
    a_log = jnp.log(jax.random.uniform(nxt(), (DEPTH, H_SSM), f32, 1.0, 16.0))
    d_skip = gain((DEPTH, H_SSM))
    ssm_norm_g = gain((DEPTH, D_SSM))
    w_out = normal((DEPTH, D_MIX, D_MODEL), D_MIX ** -0.5)
    norm_cross_g = gain((DEPTH, D_MODEL))
    norm_mem_g = gain((DEPTH, D_MODEL))
    w_cq = normal((DEPTH, D_MODEL, D_MODEL), D_MODEL ** -0.5)
    w_ck = normal((DEPTH, D_MODEL, D_MODEL), D_MODEL ** -0.5)
    w_cv = normal((DEPTH, D_MODEL, D_MODEL), D_MODEL ** -0.5)
    w_co = normal((DEPTH, D_MODEL, D_MODEL), D_MODEL ** -0.5)
    norm_ffn_g = gain((DEPTH, D_MODEL))
    w_gate = normal((DEPTH, D_MODEL, D_FF), D_MODEL ** -0.5)
    w_up = normal((DEPTH, D_MODEL, D_FF), D_MODEL ** -0.5)
    w_down = normal((DEPTH, D_FF, D_MODEL), D_FF ** -0.5)
    final_norm_g = gain((D_MODEL,))
    return {'x_prompt': x_prompt, 'x_sample': x_sample, 'mem_prompt': mem_prompt,
            'cache_k': cache_k, 'cache_v': cache_v, 'cache_logf': cache_logf,
            'page_table': page_table, 'cache_mem_k': cache_mem_k, 'cache_mem_v': cache_mem_v,
            'state_conv': state_conv, 'state_ssm': state_ssm,
            'norm_mix_g': norm_mix_g, 'w_in': w_in, 'b_forget': b_forget,
            'conv_w': conv_w, 'conv_b': conv_b, 'dt_bias': dt_bias, 'a_log': a_log,
            'd_skip': d_skip, 'ssm_norm_g': ssm_norm_g, 'w_out': w_out,
            'norm_cross_g': norm_cross_g, 'norm_mem_g': norm_mem_g,
            'w_cq': w_cq, 'w_ck': w_ck, 'w_cv': w_cv, 'w_co': w_co,
            'norm_ffn_g': norm_ffn_g, 'w_gate': w_gate, 'w_up': w_up, 'w_down': w_down,
            'final_norm_g': final_norm_g}


def reference(x_prompt, x_sample, mem_prompt, cache_k, cache_v, cache_logf, page_table,
              cache_mem_k, cache_mem_v, state_conv, state_ssm,
              norm_mix_g, w_in, b_forget, conv_w, conv_b, dt_bias, a_log, d_skip,
              ssm_norm_g, w_out, norm_cross_g, norm_mem_g, w_cq, w_ck, w_cv, w_co,
              norm_ffn_g, w_gate, w_up, w_down, final_norm_g):
    xp, xs = x_prompt, x_sample
    bp = xp.shape[0]
    kp_l, vp_l, fp_l, mkp_l, mvp_l, cp_l, sp_l = [], [], [], [], [], [], []
    ks_l, vs_l, fs_l, cs_l, ss_l = [], [], [], [], []
    for layer in range(DEPTH):
        mix_w = (norm_mix_g[layer], w_in[layer], b_forget[layer], conv_w[layer], conv_b[layer],
                 dt_bias[layer], a_log[layer], d_skip[layer], ssm_norm_g[layer], w_out[layer])
        conv0 = jnp.zeros((bp, CONV_WIDTH - 1, CONV_CH), xp.dtype)
        h0 = jnp.zeros((bp, H_SSM, P_SSM, N_STATE), jnp.float32)
        xp, kp, vp, fp, cp, sp = mixer_sublayer(xp, fox_prompt, conv0, h0, *mix_w)
        ck, cv, cf = cache_k[layer], cache_v[layer], cache_logf[layer]

        def attend_sample(q, k, v, logf, ck=ck, cv=cv, cf=cf):
            return fox_sample(q, k, v, logf, ck, cv, cf, page_table)

        xs, ks, vs, fs, cs, ss = mixer_sublayer(xs, attend_sample, state_conv[layer],
                                                state_ssm[layer], *mix_w)
        mkp, mvp = memory_kv(mem_prompt, norm_mem_g[layer], w_ck[layer], w_cv[layer])
        xp = cross_sublayer(xp, mkp, mvp, norm_cross_g[layer], w_cq[layer], w_co[layer])
        xs = cross_sublayer(xs, cache_mem_k[layer], cache_mem_v[layer], norm_cross_g[layer],
                            w_cq[layer], w_co[layer])
        xp = ffn_sublayer(xp, norm_ffn_g[layer], w_gate[layer], w_up[layer], w_down[layer])
        xs = ffn_sublayer(xs, norm_ffn_g[layer], w_gate[layer], w_up[layer], w_down[layer])
        kp_l.append(kp); vp_l.append(vp); fp_l.append(fp); mkp_l.append(mkp); mvp_l.append(mvp)
        cp_l.append(cp); sp_l.append(sp)
        ks_l.append(ks); vs_l.append(vs); fs_l.append(fs); cs_l.append(cs); ss_l.append(ss)
    y_prompt = rms_norm(xp, final_norm_g)
    y_sample = rms_norm(xs, final_norm_g)
    return (y_prompt, y_sample,
            jnp.stack(kp_l), jnp.stack(vp_l), jnp.stack(fp_l), jnp.stack(mkp_l), jnp.stack(mvp_l),
            jnp.stack(cp_l), jnp.stack(sp_l),
            jnp.stack(ks_l), jnp.stack(vs_l), jnp.stack(fs_l), jnp.stack(cs_l), jnp.stack(ss_l))
```

```python
import functools

import jax
import jax.numpy as jnp
from jax import lax
from jax.experimental import pallas as pl
from jax.experimental.pallas import tpu as pltpu

F32 = jnp.float32
BF16 = jnp.bfloat16

D_MODEL = 1024
D_ATT = 512
H_ATT = 8
HD_ATT = 64
D_SSM = 512
H_SSM = 8
P_SSM = 64
N_STATE = 128
N_BC_GROUPS = 2
GROUP_W = D_SSM // N_BC_GROUPS
CONV_WIDTH = 4
CONV_CH = D_SSM + 2 * N_BC_GROUPS * N_STATE
H_X = 4
HD_X = 256
EPS = 1e-6
ATT_SCALE = HD_ATT ** -0.5
X_SCALE = HD_X ** -0.5
SSD_CHUNK = 128
PAGE_SIZE = 128
NEG_BIG = -1e30

LANES = 128
VMEM_LIMIT = 56 * 1024 * 1024

_Q0, _K0, _V0, _Z0, _XBC0, _DT0, _MAIN_COLS = 0, 512, 1024, 1536, 2048, 3072, 3584


def _cparams(n_axes):
    return pltpu.CompilerParams(dimension_semantics=("arbitrary",) * n_axes,
                                vmem_limit_bytes=VMEM_LIMIT)


def _rms(x, g):
    ms = jnp.mean(x * x, axis=-1, keepdims=True)
    return x * lax.rsqrt(ms + EPS) * g


def _softplus(x):
    return jnp.maximum(x, 0.0) + jnp.log1p(jnp.exp(-jnp.abs(x)))


def _log_sigmoid(x):
    return jnp.minimum(x, 0.0) - jnp.log1p(jnp.exp(-jnp.abs(x)))


def _silu(x):
    return x * (1.0 / (1.0 + jnp.exp(-x)))


def _split3(x):
    hi = x.astype(BF16).astype(F32)
    r1 = x - hi
    mid = r1.astype(BF16).astype(F32)
    lo = (r1 - mid).astype(BF16).astype(F32)
    return hi, mid, lo


def _dot(a, b):
    return jnp.dot(a, b, preferred_element_type=F32)


def _dot_nt(a, b):
    return lax.dot_general(a, b, (((1,), (1,)), ((), ())), preferred_element_type=F32)


def _dot_tn(a, b):
    return lax.dot_general(a, b, (((0,), (0,)), ((), ())), preferred_element_type=F32)


def _cumsum_lanes(x, tri_upper):
    hi, mid, lo = _split3(x)
    parts = jnp.concatenate([hi, mid, lo], axis=0).astype(BF16)
    r = _dot(parts, tri_upper)
    return r[0:8] + r[8:16] + r[16:24]


def _inproj_kernel(x_ref, g_ref, wm_ref, wst_ref, bf_ref, dtb_ref, dtbx_ref, tri_ref,
                   qb_ref, kb_ref, vb_ref, k32_ref, v32_ref, z_ref, xbc_ref, dtx_ref,
                   logft_ref, ct_ref, dtt_ref, carry_ref):
    @pl.when(pl.program_id(1) == 0)
    def _():
        carry_ref[...] = jnp.zeros_like(carry_ref)

    h = _rms(x_ref[0], g_ref[...]).astype(BF16)
    qb_ref[0] = (_dot(h, wm_ref[:, _Q0:_K0]) * ATT_SCALE).astype(BF16)
    k = _dot(h, wm_ref[:, _K0:_V0])
    k32_ref[0] = k
    kb_ref[0] = k.astype(BF16)
    v = _dot(h, wm_ref[:, _V0:_Z0])
    v32_ref[0] = v
    vb_ref[0] = v.astype(BF16)
    z_ref[0] = _dot(h, wm_ref[:, _Z0:_XBC0])
    xbc_ref[0] = _dot(h, wm_ref[:, _XBC0:_DT0])
    dtx_ref[0] = _softplus(_dot(h, wm_ref[:, _DT0:_MAIN_COLS]) + dtbx_ref[...])

    small = _dot_nt(wst_ref[...], h)
    logft = _log_sigmoid(small[0:8] + bf_ref[...])
    logft_ref[0] = logft
    dtt_ref[0] = _softplus(small[8:16] + dtb_ref[...])
    c = _cumsum_lanes(logft, tri_ref[...]) + carry_ref[:, 0:1]
    ct_ref[0] = c
    bt = c.shape[1]
    carry_ref[...] = jnp.broadcast_to(c[:, bt - 1:bt], carry_ref.shape)


def _in_projection(x, g, w_main, w_small_t, b_forget, dt_bias, dt_bias_x, bt):
    nb, l, _ = x.shape
    tri = jnp.triu(jnp.ones((bt, bt), F32)).astype(BF16)
    tok = lambda w: pl.BlockSpec((1, bt, w), lambda b, j: (b, j, 0))
    tok_t = pl.BlockSpec((1, H_ATT, bt), lambda b, j: (b, 0, j))
    const = lambda shape: pl.BlockSpec(shape, lambda b, j: (0,) * len(shape))
    out_shape = (
        jax.ShapeDtypeStruct((nb, l, D_ATT), BF16),
        jax.ShapeDtypeStruct((nb, l, D_ATT), BF16),
        jax.ShapeDtypeStruct((nb, l, D_ATT), BF16),
        jax.ShapeDtypeStruct((nb, l, D_ATT), F32),
        jax.ShapeDtypeStruct((nb, l, D_ATT), F32),
        jax.ShapeDtypeStruct((nb, l, D_SSM), F32),
        jax.ShapeDtypeStruct((nb, l, CONV_CH), F32),
        jax.ShapeDtypeStruct((nb, l, D_SSM), F32),
        jax.ShapeDtypeStruct((nb, H_ATT, l), F32),
        jax.ShapeDtypeStruct((nb, H_ATT, l), F32),
        jax.ShapeDtypeStruct((nb, H_SSM, l), F32),
    )
    out_specs = (tok(D_ATT), tok(D_ATT), tok(D_ATT), tok(D_ATT), tok(D_ATT), tok(D_SSM),
                 tok(CONV_CH), tok(D_SSM), tok_t, tok_t, tok_t)
    return pl.pallas_call(
        _inproj_kernel,
        out_shape=out_shape,
        grid=(nb, l // bt),
        in_specs=[tok(D_MODEL), const((1, D_MODEL)), const((D_MODEL, _MAIN_COLS)),
                  const((2 * H_ATT, D_MODEL)), const((H_ATT, 1)), const((H_SSM, 1)),
                  const((1, D_SSM)), const((bt, bt))],
        out_specs=out_specs,
        scratch_shapes=[pltpu.VMEM((H_ATT, LANES), F32)],
        compiler_params=_cparams(2),
        name="in_projection",
    )(x, g, w_main, w_small_t, b_forget, dt_bias, dt_bias_x, tri)


def _fox_prompt_kernel(q_ref, k_ref, v_ref, c_ref, o_ref, *, blk):
    qi = pl.program_id(2)
    q = q_ref[0]
    lane = lax.broadcasted_iota(jnp.int32, (blk, LANES), 1)
    first = lane < HD_ATT
    zero = jnp.zeros_like(q)
    q_heads = (jnp.where(first, q, zero), jnp.where(first, zero, q))
    q0 = pl.multiple_of(qi * blk, blk)
    cbase = c_ref[0, 0, :, pl.ds(q0, LANES)][:, 0:1]

    def tile(j, carry, diagonal):
        k0 = pl.multiple_of(j * blk, blk)
        kj = k_ref[0, pl.ds(k0, blk), :]
        vj = v_ref[0, pl.ds(k0, blk), :]
        bias = cbase - c_ref[0, 0, :, pl.ds(k0, blk)]
        out = []
        for hh in range(2):
            m, l, acc = carry[hh]
            s = _dot_nt(q_heads[hh], kj) + bias[hh:hh + 1, :]
            if diagonal:
                row = lax.broadcasted_iota(jnp.int32, (blk, blk), 0)
                col = lax.broadcasted_iota(jnp.int32, (blk, blk), 1)
                s = jnp.where(row >= col, s, NEG_BIG)
            m_new = jnp.maximum(m, jnp.max(s, axis=-1, keepdims=True))
            p = jnp.exp(s - m_new)
            alpha = jnp.exp(m - m_new)
            l = alpha * l + jnp.sum(p, axis=-1, keepdims=True)
            acc = alpha * acc + _dot(p.astype(BF16), vj)
            out.append((m_new, l, acc))
        return tuple(out)

    init = tuple((jnp.full((blk, 1), NEG_BIG, F32), jnp.zeros((blk, 1), F32),
                  jnp.zeros((blk, LANES), F32)) for _ in range(2))
    carry = lax.fori_loop(0, qi, lambda j, c: tile(j, c, False), init)
    (_, l_a, acc_a), (_, l_b, acc_b) = tile(qi, carry, True)
    o_ref[0] = jnp.where(first, acc_a / l_a, acc_b / l_b).astype(o_ref.dtype)


def _fox_prompt(qb, kb, vb, ct, blk):
    nb, l, _ = qb.shape
    n_pairs = D_ATT // LANES
    c4 = ct.reshape(nb, n_pairs, 2, l)
    return pl.pallas_call(
        functools.partial(_fox_prompt_kernel, blk=blk),
        out_shape=jax.ShapeDtypeStruct((nb, l, D_ATT), BF16),
        grid=(nb, n_pairs, l // blk),
        in_specs=[pl.BlockSpec((1, blk, LANES), lambda b, p, i: (b, i, p)),
                  pl.BlockSpec((1, l, LANES), lambda b, p, i: (b, 0, p)),
                  pl.BlockSpec((1, l, LANES), lambda b, p, i: (b, 0, p)),
                  pl.BlockSpec((1, 1, 2, l), lambda b, p, i: (b, p, 0, 0))],
        out_specs=pl.BlockSpec((1, blk, LANES), lambda b, p, i: (b, i, p)),
        compiler_params=_cparams(3),
        name="fox_prompt",
    )(qb, kb, vb, c4)


def _ssd_gate_norm(y, z, g):
    return _rms(y * _silu(z), g)


def _ssd_prompt_kernel(xbc_ref, z_ref, dtx_ref, dtt_ref, cw_ref, cb_ref, alx_ref, alc_ref,
                       dsx_ref, g_ref, tril_ref, triu_ref,
                       y_ref, st_ref, xp_ref, state_ref):
    q = SSD_CHUNK
    c = pl.program_id(1)

    @pl.when(c == 0)
    def _():
        xp_ref[0:8, :] = jnp.zeros((8, CONV_CH), F32)
        state_ref[...] = jnp.zeros_like(state_ref)

    xp_ref[8:8 + q, :] = xbc_ref[0]
    acc = cb_ref[...]
    for tap in range(CONV_WIDTH):
        off = 8 - (CONV_WIDTH - 1) + tap
        acc = acc + xp_ref[off:off + q, :] * cw_ref[tap:tap + 1, :]
    xp_ref[0:8, :] = xp_ref[q:q + 8, :]
    act = _silu(acc)
    xs = act[:, :D_SSM]
    bmat = act[:, D_SSM:D_SSM + N_BC_GROUPS * N_STATE]
    cmat = act[:, D_SSM + N_BC_GROUPS * N_STATE:]

    dt = dtx_ref[0]
    a_x = -jnp.exp(alx_ref[...]) * dt
    hi, mid, lo = _split3(a_x)
    parts = jnp.concatenate([hi, mid, lo], axis=1).astype(BF16)
    r = _dot(tril_ref[...], parts)
    acs_x = r[:, 0:D_SSM] + r[:, D_SSM:2 * D_SSM] + r[:, 2 * D_SSM:3 * D_SSM]
    a_t = -jnp.exp(alc_ref[...]) * dtt_ref[0]
    acs_t = _cumsum_lanes(a_t, triu_ref[...])

    xdt = xs * dt
    a_last = acs_x[q - 1:q, :]
    xdt_end = (xdt * jnp.exp(a_last - acs_x)).astype(BF16)
    e_acs = jnp.exp(acs_x)
    xdt_b = xdt.astype(BF16)
    state_old = state_ref[...]
    state_b = state_old.astype(BF16)

    row = lax.broadcasted_iota(jnp.int32, (q, q), 0)
    col = lax.broadcasted_iota(jnp.int32, (q, q), 1)
    causal = row >= col
    first = lax.broadcasted_iota(jnp.int32, (q, LANES), 1) < P_SSM

    y_parts = []
    for g in range(N_BC_GROUPS):
        cg = cmat[:, g * N_STATE:(g + 1) * N_STATE].astype(BF16)
        bg = bmat[:, g * N_STATE:(g + 1) * N_STATE].astype(BF16)
        cb = _dot_nt(cg, bg)
        gl = slice(g * GROUP_W, (g + 1) * GROUP_W)
        y_off = _dot(cg, state_b[:, gl]) * e_acs[:, gl]
        for pr in range(GROUP_W // LANES):
            lo_lane = g * GROUP_W + pr * LANES
            pair = []
            for hh in range(2):
                h = lo_lane // P_SSM + hh
                seg = acs_x[:, h * P_SSM:h * P_SSM + 1] - acs_t[h:h + 1, :]
                decay = jnp.exp(jnp.where(causal, seg, -jnp.inf))
                gm = (cb * decay).astype(BF16)
                pair.append(_dot(gm, xdt_b[:, lo_lane:lo_lane + LANES]))
            y_parts.append(jnp.where(first, pair[0], pair[1]) + y_off[:, pr * LANES:(pr + 1) * LANES])
        state_ref[:, gl] = jnp.exp(a_last[:, gl]) * state_old[:, gl] + _dot_tn(bg, xdt_end[:, gl])
    y = jnp.concatenate(y_parts, axis=1) + dsx_ref[...] * xs
    y_ref[0] = _ssd_gate_norm(y, z_ref[0], g_ref[...]).astype(y_ref.dtype)

    @pl.when(c == pl.num_programs(1) - 1)
    def _():
        st_ref[0] = state_ref[...].T


def _ssd_prompt(xbc, z, dtx, dtt, conv_w, conv_b, a_log_x, a_log_c, d_skip_x, norm_g):
    nb, l, _ = xbc.shape
    q = SSD_CHUNK
    tril = jnp.tril(jnp.ones((q, q), F32)).astype(BF16)
    triu = jnp.triu(jnp.ones((q, q), F32)).astype(BF16)
    tok = lambda w: pl.BlockSpec((1, q, w), lambda b, c: (b, c, 0))
    const = lambda shape: pl.BlockSpec(shape, lambda b, c: (0,) * len(shape))
    return pl.pallas_call(
        _ssd_prompt_kernel,
        out_shape=(jax.ShapeDtypeStruct((nb, l, D_SSM), BF16),
                   jax.ShapeDtypeStruct((nb, D_SSM, N_STATE), F32)),
        grid=(nb, l // q),
        in_specs=[tok(CONV_CH), tok(D_SSM), tok(D_SSM),
                  pl.BlockSpec((1, H_SSM, q), lambda b, c: (b, 0, c)),
                  const((CONV_WIDTH, CONV_CH)), const((1, CONV_CH)), const((1, D_SSM)),
                  const((H_SSM, 1)), const((1, D_SSM)), const((1, D_SSM)),
                  const((q, q)), const((q, q))],
        out_specs=(tok(D_SSM), pl.BlockSpec((1, D_SSM, N_STATE), lambda b, c: (b, 0, 0))),
        scratch_shapes=[pltpu.VMEM((q + 8, CONV_CH), F32), pltpu.VMEM((N_STATE, D_SSM), F32)],
        compiler_params=_cparams(2),
        name="ssd_prompt",
    )(xbc, z, dtx, dtt, conv_w, conv_b, a_log_x, a_log_c, d_skip_x, norm_g, tril, triu)


def _mix_out_kernel(x_ref, att_ref, ssm_ref, wo_ref, g_ref, wq_ref, x1_ref, qc_ref):
    x1 = x_ref[...] + _dot(att_ref[...], wo_ref[0:D_ATT, :]) + _dot(ssm_ref[...], wo_ref[D_ATT:, :])
    x1_ref[...] = x1
    h = _rms(x1, g_ref[...]).astype(BF16)
    qc_ref[...] = (_dot(h, wq_ref[...]) * X_SCALE).astype(BF16)


def _mix_out(x, att, ssm, w_out, g_cross, w_cq, bt):
    t = x.shape[0]
    tok = lambda w: pl.BlockSpec((bt, w), lambda i: (i, 0))
    const = lambda shape: pl.BlockSpec(shape, lambda i: (0,) * len(shape))
    return pl.pallas_call(
        _mix_out_kernel,
        out_shape=(jax.ShapeDtypeStruct((t, D_MODEL), F32), jax.ShapeDtypeStruct((t, D_MODEL), BF16)),
        grid=(t // bt,),
        in_specs=[tok(D_MODEL), tok(D_ATT), tok(D_SSM), const((D_MODEL, D_MODEL)),
                  const((1, D_MODEL)), const((D_MODEL, D_MODEL))],
        out_specs=(tok(D_MODEL), tok(D_MODEL)),
        compiler_params=_cparams(1),
        name="mix_out",
    )(x, att, ssm, w_out, g_cross, w_cq)


def _ffn_kernel(x1_ref, o_ref, wco_ref, g_ref, wg_ref, wu_ref, wd_ref, gf_ref, y_ref):
    x2 = x1_ref[...] + _dot(o_ref[...], wco_ref[...])
    h = _rms(x2, g_ref[...]).astype(BF16)
    u = (_silu(_dot(h, wg_ref[...])) * _dot(h, wu_ref[...])).astype(BF16)
    x3 = x2 + _dot(u, wd_ref[...])
    y_ref[...] = _rms(x3, gf_ref[...])


def _ffn(x1, o, w_co, g_ffn, w_gate, w_up, w_down, g_final, bt):
    t = x1.shape[0]
    d_ff = w_gate.shape[1]
    tok = lambda w: pl.BlockSpec((bt, w), lambda i: (i, 0))
    const = lambda shape: pl.BlockSpec(shape, lambda i: (0,) * len(shape),
                                       pipeline_mode=pl.Buffered(1))
    return pl.pallas_call(
        _ffn_kernel,
        out_shape=jax.ShapeDtypeStruct((t, D_MODEL), F32),
        grid=(t // bt,),
        in_specs=[tok(D_MODEL), tok(D_MODEL), const((D_MODEL, D_MODEL)), const((1, D_MODEL)),
                  const((D_MODEL, d_ff)), const((D_MODEL, d_ff)), const((d_ff, D_MODEL)),
                  const((1, D_MODEL))],
        out_specs=tok(D_MODEL),
        compiler_params=_cparams(1),
        name="cross_out_ffn",
    )(x1, o, w_co, g_ffn, w_gate, w_up, w_down, g_final)


def _memory_kv_kernel(mem_ref, g_ref, wk_ref, wv_ref, k32_ref, v32_ref, kb_ref, vb_ref):
    mn = _rms(mem_ref[0], g_ref[...]).astype(BF16)
    k = _dot(mn, wk_ref[...])
    v = _dot(mn, wv_ref[...])
    k32_ref[0] = k
    v32_ref[0] = v
    kb_ref[0] = k.astype(BF16)
    vb_ref[0] = v.astype(BF16)


def _memory_kv(mem, g, w_ck, w_cv):
    nb, m, _ = mem.shape
    blk = pl.BlockSpec((1, m, D_MODEL), lambda b: (b, 0, 0))
    const = lambda shape: pl.BlockSpec(shape, lambda b: (0,) * len(shape))
    f = jax.ShapeDtypeStruct((nb, m, D_MODEL), F32)
    h = jax.ShapeDtypeStruct((nb, m, D_MODEL), BF16)
    return pl.pallas_call(
        _memory_kv_kernel,
        out_shape=(f, f, h, h),
        grid=(nb,),
        in_specs=[blk, const((1, D_MODEL)), const((D_MODEL, D_MODEL)), const((D_MODEL, D_MODEL))],
        out_specs=(blk, blk, blk, blk),
        compiler_params=_cparams(1),
        name="memory_kv",
    )(mem, g, w_ck, w_cv)


def _softmax_rows(s):
    m = jnp.max(s, axis=-1, keepdims=True)
    p = jnp.exp(s - m)
    return p / jnp.sum(p, axis=-1, keepdims=True)


def _cross_prompt_kernel(q_ref, k_ref, v_ref, o_ref):
    for h in range(H_X):
        hl = slice(h * HD_X, (h + 1) * HD_X)
        p = _softmax_rows(_dot_nt(q_ref[0, :, hl], k_ref[0, :, hl]))
        o_ref[0, :, hl] = _dot(p.astype(BF16), v_ref[0, :, hl]).astype(o_ref.dtype)


def _cross_prompt(qc, mk, mv, bt):
    nb, l, _ = qc.shape
    m = mk.shape[1]
    tok = pl.BlockSpec((1, bt, D_MODEL), lambda b, i: (b, i, 0))
    mem = pl.BlockSpec((1, m, D_MODEL), lambda b, i: (b, 0, 0))
    return pl.pallas_call(
        _cross_prompt_kernel,
        out_shape=jax.ShapeDtypeStruct((nb, l, D_MODEL), BF16),
        grid=(nb, l // bt),
        in_specs=[tok, mem, mem],
        out_specs=tok,
        compiler_params=_cparams(2),
        name="cross_prompt",
    )(qc, mk, mv)


def _cross_sample_kernel(q_ref, k_ref, v_ref, o_ref):
    for h in range(H_X):
        hl = slice(h * HD_X, (h + 1) * HD_X)
        k = k_ref[:, :, hl].astype(BF16)
        v = v_ref[:, :, hl].astype(BF16)
        s = jnp.einsum("sqd,smd->sqm", q_ref[:, :, hl], k, preferred_element_type=F32)
        p = _softmax_rows(s)
        o = jnp.einsum("sqm,smd->sqd", p.astype(BF16), v, preferred_element_type=F32)
        o_ref[:, :, hl] = o.astype(o_ref.dtype)


def _cross_sample(qc, mem_k, mem_v, n_seq_blk):
    ns, lq, _ = qc.shape
    m = mem_k.shape[1]
    tok = pl.BlockSpec((n_seq_blk, lq, D_MODEL), lambda i: (i, 0, 0))
    mem = pl.BlockSpec((n_seq_blk, m, D_MODEL), lambda i: (i, 0, 0))
    return pl.pallas_call(
        _cross_sample_kernel,
        out_shape=jax.ShapeDtypeStruct((ns, lq, D_MODEL), BF16),
        grid=(ns // n_seq_blk,),
        in_specs=[tok, mem, mem],
        out_specs=tok,
        compiler_params=_cparams(1),
        name="cross_sample",
    )(qc, mem_k, mem_v)


def _fox_sample_kernel(pt_ref, q_ref, kn_ref, vn_ref, lfn_ref, tril_ref, *rest, n_pages, lq):
    del pt_ref
    k_refs = rest[0:n_pages]
    v_refs = rest[n_pages:2 * n_pages]
    lf_refs = rest[2 * n_pages:3 * n_pages]
    o_ref = rest[3 * n_pages]
    s_ref = rest[3 * n_pages + 1]
    rows = lq * H_ATT

    q = q_ref[0].astype(F32)
    head_of_lane = lax.broadcasted_iota(jnp.int32, (H_ATT, D_ATT), 1) // HD_ATT
    hmask = head_of_lane == lax.broadcasted_iota(jnp.int32, (H_ATT, D_ATT), 0)
    qbd = jnp.concatenate(
        [jnp.where(hmask, jnp.broadcast_to(q[t:t + 1, :], (H_ATT, D_ATT)), 0.0) for t in range(lq)],
        axis=0).astype(BF16)

    lf_all = jnp.concatenate([r[0] for r in lf_refs], axis=0)
    hi, mid, lo = _split3(lf_all)
    tril = tril_ref[...]
    incl = _dot(hi.astype(BF16), tril) + _dot(mid.astype(BF16), tril) + _dot(lo.astype(BF16), tril)
    after = jnp.zeros((H_ATT, 1), F32)
    m_run = jnp.full((rows, 1), NEG_BIG, F32)
    for j in reversed(range(n_pages)):
        sl = slice(j * H_ATT, (j + 1) * H_ATT)
        bias = incl[sl] - lf_all[sl] + after
        after = after + incl[sl][:, 0:1]
        s = _dot_nt(qbd, k_refs[j][0].astype(BF16)) + jnp.concatenate([bias] * lq, axis=0)
        s_ref[:, j * PAGE_SIZE:(j + 1) * PAGE_SIZE] = s
        m_run = jnp.maximum(m_run, jnp.max(s, axis=-1, keepdims=True))

    lfn = lfn_ref[0]
    lane = lax.broadcasted_iota(jnp.int32, (H_ATT, lq), 1)
    cn = jnp.zeros((H_ATT, lq), F32)
    for t in range(lq):
        cn = cn + jnp.where(lane >= t, lfn[:, t:t + 1], 0.0)
    s_new = _dot_nt(qbd, kn_ref[0]) - jnp.concatenate([cn] * lq, axis=0)
    t_row = lax.broadcasted_iota(jnp.int32, (rows, lq), 0) // H_ATT
    t_col = lax.broadcasted_iota(jnp.int32, (rows, lq), 1)
    s_new = jnp.where(t_col <= t_row, s_new, NEG_BIG)
    m_run = jnp.maximum(m_run, jnp.max(s_new, axis=-1, keepdims=True))

    p_new = jnp.exp(s_new - m_run)
    l_run = jnp.sum(p_new, axis=-1, keepdims=True)
    acc = _dot(p_new.astype(BF16), vn_ref[0])
    for j in range(n_pages):
        p = jnp.exp(s_ref[:, j * PAGE_SIZE:(j + 1) * PAGE_SIZE] - m_run)
        l_run = l_run + jnp.sum(p, axis=-1, keepdims=True)
        acc = acc + _dot(p.astype(BF16), v_refs[j][0].astype(BF16))
    acc = acc / l_run
    out = [jnp.sum(jnp.where(hmask, acc[t * H_ATT:(t + 1) * H_ATT, :], 0.0), axis=0, keepdims=True)
           for t in range(lq)]
    o_ref[0] = jnp.concatenate(out, axis=0).astype(o_ref.dtype)


def _fox_sample(qb, kb, vb, lfn, cache_k, cache_v, cache_lft, page_table):
    ns, lq, _ = qb.shape
    n_pages = page_table.shape[1]
    tril = jnp.tril(jnp.ones((PAGE_SIZE, PAGE_SIZE), F32)).astype(BF16)
    tok = pl.BlockSpec((1, lq, D_ATT), lambda i, pt: (i, 0, 0))

    def page(j, shape):
        return pl.BlockSpec((1,) + shape, lambda i, pt: (pt[i * n_pages + j], 0, 0))

    in_specs = [tok, tok, tok, pl.BlockSpec((1, H_ATT, lq), lambda i, pt: (i, 0, 0)),
                pl.BlockSpec((PAGE_SIZE, PAGE_SIZE), lambda i, pt: (0, 0))]
    in_specs += [page(j, (PAGE_SIZE, D_ATT)) for j in range(n_pages)]
    in_specs += [page(j, (PAGE_SIZE, D_ATT)) for j in range(n_pages)]
    in_specs += [page(j, (H_ATT, PAGE_SIZE)) for j in range(n_pages)]
    grid_spec = pltpu.PrefetchScalarGridSpec(
        num_scalar_prefetch=1,
        grid=(ns,),
        in_specs=in_specs,
        out_specs=tok,
        scratch_shapes=[pltpu.VMEM((lq * H_ATT, n_pages * PAGE_SIZE), F32)],
    )
    return pl.pallas_call(
        functools.partial(_fox_sample_kernel, n_pages=n_pages, lq=lq),
        out_shape=jax.ShapeDtypeStruct((ns, lq, D_ATT), BF16),
        grid_spec=grid_spec,
        compiler_params=_cparams(1),
        name="fox_sample",
    )(page_table.reshape(-1), qb, kb, vb, lfn, tril,
      *([cache_k] * n_pages), *([cache_v] * n_pages), *([cache_lft] * n_pages))


def _ssd_sample_kernel(xbc_ref, z_ref, dtx_ref, cs_ref, st_ref, cw_ref, cb_ref, alx_ref, dsx_ref,
                       g_ref, y_ref, sto_ref, xp_ref, rows_ref, brow_ref, *, lq):
    nsb = xbc_ref.shape[0]
    assert nsb * (lq + 1) <= LANES
    kw = CONV_WIDTH - 1
    xp_ref[:, 0:kw, :] = cs_ref[...]
    xp_ref[:, kw:kw + lq, :] = xbc_ref[...]
    acc = cb_ref[...]
    for tap in range(CONV_WIDTH):
        acc = acc + xp_ref[:, tap:tap + lq, :] * cw_ref[tap:tap + 1, :]
    act = _silu(acc)
    xs = act[:, :, :D_SSM]
    bmat = act[:, :, D_SSM:D_SSM + N_BC_GROUPS * N_STATE]
    cmat = act[:, :, D_SSM + N_BC_GROUPS * N_STATE:]
    dt = dtx_ref[...]
    a = -jnp.exp(alx_ref[...]) * dt
    xdt = xs * dt
    acs = [a[:, 0:1, :]]
    for t in range(1, lq):
        acs.append(acs[-1] + a[:, t:t + 1, :])
    a_last = acs[lq - 1]

    ccat = jnp.concatenate([cmat[:, :, g * N_STATE:(g + 1) * N_STATE] for g in range(N_BC_GROUPS)],
                           axis=1).astype(BF16)
    state = st_ref[...]
    r = jnp.einsum("sgn,shn->sgh", ccat, state.astype(BF16), preferred_element_type=F32)
    group0 = lax.broadcasted_iota(jnp.int32, (1, 1, D_SSM), 2) < GROUP_W
    y_off = jnp.where(group0, r[:, 0:lq, :], r[:, lq:2 * lq, :])

    ys = []
    for t in range(lq):
        y_t = jnp.exp(acs[t]) * y_off[:, t:t + 1, :] + dsx_ref[...] * xs[:, t:t + 1, :]
        for s in range(t + 1):
            cb = jnp.concatenate(
                [jnp.broadcast_to(
                    jnp.sum(cmat[:, t:t + 1, g * N_STATE:(g + 1) * N_STATE]
                            * bmat[:, s:s + 1, g * N_STATE:(g + 1) * N_STATE], axis=-1, keepdims=True),
                    (nsb, 1, GROUP_W)) for g in range(N_BC_GROUPS)], axis=2)
            y_t = y_t + cb * jnp.exp(acs[t] - acs[s]) * xdt[:, s:s + 1, :]
        ys.append(y_t)
    y = jnp.concatenate(ys, axis=1)
    y_ref[...] = _ssd_gate_norm(y, z_ref[...], g_ref[...]).astype(y_ref.dtype)

    upd = jnp.concatenate([xdt[:, s:s + 1, :] * jnp.exp(a_last - acs[s]) for s in range(lq)], axis=1)
    e_last = jnp.exp(a_last)
    rows_ref[...] = jnp.zeros_like(rows_ref)
    brow_ref[...] = jnp.zeros_like(brow_ref)
    for i in range(nsb):
        rows_ref[i * lq:(i + 1) * lq, :] = upd[i]
        rows_ref[nsb * lq + i:nsb * lq + i + 1, :] = e_last[i]
        brow_ref[i * lq:(i + 1) * lq, :] = bmat[i]
    cols = rows_ref[...].T
    upd_cols = cols.astype(BF16)
    b_all = brow_ref[...]
    row_seq = lax.broadcasted_iota(jnp.int32, (LANES, N_BC_GROUPS * N_STATE), 0) // lq
    top = lax.broadcasted_iota(jnp.int32, (D_SSM, N_STATE), 0) < GROUP_W
    for i in range(nsb):
        b_rows = jnp.where(row_seq == i, b_all, 0.0).astype(BF16)
        m = _dot(upd_cols, b_rows)
        add = jnp.where(top, m[:, 0:N_STATE], m[:, N_STATE:2 * N_STATE])
        decay = jnp.broadcast_to(cols[:, nsb * lq + i:nsb * lq + i + 1], (D_SSM, N_STATE))
        sto_ref[i] = decay * state[i] + add


def _ssd_sample(xbc, z, dtx, conv_state, state, conv_w, conv_b, a_log_x, d_skip_x, norm_g, nsb):
    ns, lq, _ = xbc.shape
    seq = lambda r, w: pl.BlockSpec((nsb, r, w), lambda i: (i, 0, 0))
    const = lambda shape: pl.BlockSpec(shape, lambda i: (0,) * len(shape))
    return pl.pallas_call(
        functools.partial(_ssd_sample_kernel, lq=lq),
        out_shape=(jax.ShapeDtypeStruct((ns, lq, D_SSM), BF16),
                   jax.ShapeDtypeStruct((ns, D_SSM, N_STATE), F32)),
        grid=(ns // nsb,),
        in_specs=[seq(lq, CONV_CH), seq(lq, D_SSM), seq(lq, D_SSM), seq(CONV_WIDTH - 1, CONV_CH),
                  seq(D_SSM, N_STATE), const((CONV_WIDTH, CONV_CH)), const((1, CONV_CH)),
                  const((1, D_SSM)), const((1, D_SSM)), const((1, D_SSM))],
        out_specs=(seq(lq, D_SSM), seq(D_SSM, N_STATE)),
        scratch_shapes=[pltpu.VMEM((nsb, 8, CONV_CH), F32), pltpu.VMEM((LANES, D_SSM), F32),
                        pltpu.VMEM((LANES, N_BC_GROUPS * N_STATE), F32)],
        compiler_params=_cparams(1),
        name="ssd_sample",
    )(xbc, z, dtx, conv_state, state, conv_w, conv_b, a_log_x, d_skip_x, norm_g)


def _pick(pref, n):
    return pref if n % pref == 0 else n


def kernel(x_prompt, x_sample, mem_prompt, cache_k, cache_v, cache_logf, page_table, cache_mem_k, cache_mem_v, state_conv, state_ssm, norm_mix_g, w_in, b_forget, conv_w, conv_b, dt_bias, a_log, d_skip, ssm_norm_g, w_out, norm_cross_g, norm_mem_g, w_cq, w_ck, w_cv, w_co, norm_ffn_g, w_gate, w_up, w_down, final_norm_g):
    assert w_in.shape[0] == 1, "one layer"
    nb, l, _ = x_prompt.shape
    ns, lq, _ = x_sample.shape
    n_phys = cache_k.shape[1]
    row = lambda v: v.reshape(1, -1).astype(F32)
    colv = lambda v: v.reshape(-1, 1).astype(F32)
    per_ch = lambda v: jnp.repeat(v.astype(F32), P_SSM).reshape(1, D_SSM)

    w = w_in[0]
    cuts = [D_ATT, 2 * D_ATT, 3 * D_ATT, 3 * D_ATT + H_ATT, 3 * D_ATT + H_ATT + D_SSM,
            3 * D_ATT + H_ATT + D_SSM + CONV_CH]
    w_q, w_k, w_v, w_f, w_z, w_xbc, w_dt = jnp.split(w, cuts, axis=1)
    w_main = jnp.concatenate([w_q, w_k, w_v, w_z, w_xbc, jnp.repeat(w_dt, P_SSM, axis=1)],
                             axis=1).astype(BF16)
    w_small_t = jnp.concatenate([w_f, w_dt], axis=1).T.astype(BF16)
    in_params = (row(norm_mix_g[0]), w_main, w_small_t, colv(b_forget[0]), colv(dt_bias[0]),
                 per_ch(dt_bias[0]))
    ssd_params = (conv_w[0], row(conv_b[0]), per_ch(a_log[0]))
    ssd_tail = (per_ch(d_skip[0]), row(ssm_norm_g[0]))
    w_out_b, w_cq_b, w_co_b = w_out[0].astype(BF16), w_cq[0].astype(BF16), w_co[0].astype(BF16)
    w_gate_b, w_up_b, w_down_b = w_gate[0].astype(BF16), w_up[0].astype(BF16), w_down[0].astype(BF16)
    g_cross, g_ffn, g_final = row(norm_cross_g[0]), row(norm_ffn_g[0]), row(final_norm_g)

    def tail(x, att, ssm, cross, bt):
        x1, qc = _mix_out(x, att, ssm, w_out_b, g_cross, w_cq_b, bt)
        o = cross(qc)
        return _ffn(x1, o, w_co_b, g_ffn, w_gate_b, w_up_b, w_down_b, g_final, bt)

    bt = _pick(512, l)
    qb, kb, vb, k32, v32, z, xbc, dtx, logft, ct, dtt = _in_projection(x_prompt, *in_params, bt)
    att = _fox_prompt(qb, kb, vb, ct, _pick(512, l))
    ssm, st_p = _ssd_prompt(xbc, z, dtx, dtt, *ssd_params, colv(a_log[0]), *ssd_tail)
    mk32, mv32, mkb, mvb = _memory_kv(mem_prompt, row(norm_mem_g[0]), w_ck[0].astype(BF16),
                                      w_cv[0].astype(BF16))
    cross_p = lambda qc: _cross_prompt(qc.reshape(nb, l, D_MODEL), mkb, mvb, bt).reshape(nb * l, D_MODEL)
    y_prompt = tail(x_prompt.reshape(nb * l, D_MODEL), att.reshape(nb * l, D_ATT),
                    ssm.reshape(nb * l, D_SSM), cross_p, bt).reshape(nb, l, D_MODEL)

    ts = ns * lq
    bts = _pick(512, ts)
    qb_s, kb_s, vb_s, k32_s, v32_s, z_s, xbc_s, dtx_s, logft_s, _, _ = _in_projection(
        x_sample.reshape(1, ts, D_MODEL), *in_params, bts)
    seq3 = lambda a: a.reshape(ns, lq, a.shape[-1])
    lfn = logft_s.reshape(H_ATT, ns, lq).transpose(1, 0, 2)
    att_s = _fox_sample(seq3(qb_s), seq3(kb_s), seq3(vb_s), lfn,
                        cache_k[0].reshape(n_phys, PAGE_SIZE, D_ATT),
                        cache_v[0].reshape(n_phys, PAGE_SIZE, D_ATT),
                        cache_logf[0].transpose(0, 2, 1), page_table)
    ssm_s, st_s = _ssd_sample(seq3(xbc_s), seq3(z_s), seq3(dtx_s), state_conv[0],
                              state_ssm[0].reshape(ns, D_SSM, N_STATE), *ssd_params, *ssd_tail,
                              _pick(8, ns))
    cross_s = lambda qc: _cross_sample(qc.reshape(ns, lq, D_MODEL), cache_mem_k[0].reshape(ns, -1, D_MODEL),
                                       cache_mem_v[0].reshape(ns, -1, D_MODEL),
                                       _pick(4, ns)).reshape(ts, D_MODEL)
    y_sample = tail(x_sample.reshape(ts, D_MODEL), att_s.reshape(ts, D_ATT), ssm_s.reshape(ts, D_SSM),
                    cross_s, bts).reshape(ns, lq, D_MODEL)

    heads = lambda a, n: a.reshape(1, n, -1, H_ATT, HD_ATT)
    n_mem = mem_prompt.shape[1]
    kw = CONV_WIDTH - 1
    return (y_prompt, y_sample,
            heads(k32, nb), heads(v32, nb), logft.transpose(0, 2, 1)[None],
            mk32.reshape(1, nb, n_mem, H_X, HD_X), mv32.reshape(1, nb, n_mem, H_X, HD_X),
            xbc[:, l - kw:, :][None], st_p.reshape(1, nb, H_SSM, P_SSM, N_STATE),
            heads(k32_s, ns), heads(v32_s, ns),
            logft_s.reshape(H_ATT, ns, lq).transpose(1, 2, 0)[None],
            seq3(xbc_s)[:, lq - kw:, :][None], st_s.reshape(1, ns, H_SSM, P_SSM, N_STATE))
```

```python
import functools

import jax
import jax.numpy as jnp
from jax import lax
from jax.experimental import pallas as pl
from jax.experimental.pallas import tpu as pltpu

F32 = jnp.float32
BF16 = jnp.bfloat16

D_MODEL = 1024
D_ATT = 512
H_ATT = 8
HD_ATT = 64
D_SSM = 512
H_SSM = 8
P_SSM = 64
N_STATE = 128
N_BC_GROUPS = 2
GROUP_W = D_SSM // N_BC_GROUPS
CONV_WIDTH = 4
CONV_CH = D_SSM + 2 * N_BC_GROUPS * N_STATE
H_X = 4
HD_X = 256
EPS = 1e-6
ATT_SCALE = HD_ATT ** -0.5
X_SCALE = HD_X ** -0.5
SSD_CHUNK = 128
PAGE_SIZE = 128
NEG_BIG = -1e30

LANES = 128
VMEM_LIMIT = 56 * 1024 * 1024

_Q0, _K0, _V0, _Z0, _XBC0, _DT0, _MAIN_COLS = 0, 512, 1024, 1536, 2048, 3072, 3584


def _cparams(n_axes):
    return pltpu.CompilerParams(dimension_semantics=("arbitrary",) * n_axes,
                                vmem_limit_bytes=VMEM_LIMIT)


def _rms(x, g):
    ms = jnp.mean(x * x, axis=-1, keepdims=True)
    return x * lax.rsqrt(ms + EPS) * g


def _softplus(x):
    return jnp.maximum(x, 0.0) + jnp.log1p(jnp.exp(-jnp.abs(x)))


def _log_sigmoid(x):
    return jnp.minimum(x, 0.0) - jnp.log1p(jnp.exp(-jnp.abs(x)))


def _silu(x):
    return x * (1.0 / (1.0 + jnp.exp(-x)))


def _split3(x):
    hi = x.astype(BF16).astype(F32)
    r1 = x - hi
    mid = r1.astype(BF16).astype(F32)
    lo = (r1 - mid).astype(BF16).astype(F32)
    return hi, mid, lo


def _dot(a, b):
    return jnp.dot(a, b, preferred_element_type=F32)


def _dot_nt(a, b):
    return lax.dot_general(a, b, (((1,), (1,)), ((), ())), preferred_element_type=F32)


def _dot_tn(a, b):
    return lax.dot_general(a, b, (((0,), (0,)), ((), ())), preferred_element_type=F32)


def _cumsum_lanes(x, tri_upper):
    hi, mid, lo = _split3(x)
    parts = jnp.concatenate([hi, mid, lo], axis=0).astype(BF16)
    r = _dot(parts, tri_upper)
    return r[0:8] + r[8:16] + r[16:24]


def _inproj_kernel(x_ref, g_ref, wm_ref, wst_ref, bf_ref, dtb_ref, dtbx_ref, tri_ref,
                   qb_ref, kb_ref, vb_ref, k32_ref, v32_ref, z_ref, xbc_ref, dtx_ref,
                   logft_ref, ct_ref, dtt_ref, carry_ref):
    @pl.when(pl.program_id(1) == 0)
    def _():
        carry_ref[...] = jnp.zeros_like(carry_ref)

    h = _rms(x_ref[0], g_ref[...]).astype(BF16)
    qb_ref[0] = (_dot(h, wm_ref[:, _Q0:_K0]) * ATT_SCALE).astype(BF16)
    k = _dot(h, wm_ref[:, _K0:_V0])
    k32_ref[0] = k
    kb_ref[0] = k.astype(BF16)
    v = _dot(h, wm_ref[:, _V0:_Z0])
    v32_ref[0] = v
    vb_ref[0] = v.astype(BF16)
    z_ref[0] = _dot(h, wm_ref[:, _Z0:_XBC0])
    xbc_ref[0] = _dot(h, wm_ref[:, _XBC0:_DT0])
    dtx_ref[0] = _softplus(_dot(h, wm_ref[:, _DT0:_MAIN_COLS]) + dtbx_ref[...])

    small = _dot_nt(wst_ref[...], h)
    logft = _log_sigmoid(small[0:8] + bf_ref[...])
    logft_ref[0] = logft
    dtt_ref[0] = _softplus(small[8:16] + dtb_ref[...])
    c = _cumsum_lanes(logft, tri_ref[...]) + carry_ref[:, 0:1]
    ct_ref[0] = c
    bt = c.shape[1]
    carry_ref[...] = jnp.broadcast_to(c[:, bt - 1:bt], carry_ref.shape)


def _in_projection(x, g, w_main, w_small_t, b_forget, dt_bias, dt_bias_x, bt):
    nb, l, _ = x.shape
    tri = jnp.triu(jnp.ones((bt, bt), F32)).astype(BF16)
    tok = lambda w: pl.BlockSpec((1, bt, w), lambda b, j: (b, j, 0))
    tok_t = pl.BlockSpec((1, H_ATT, bt), lambda b, j: (b, 0, j))
    const = lambda shape: pl.BlockSpec(shape, lambda b, j: (0,) * len(shape))
    out_shape = (
        jax.ShapeDtypeStruct((nb, l, D_ATT), BF16),
        jax.ShapeDtypeStruct((nb, l, D_ATT), BF16),
        jax.ShapeDtypeStruct((nb, l, D_ATT), BF16),
        jax.ShapeDtypeStruct((nb, l, D_ATT), F32),
        jax.ShapeDtypeStruct((nb, l, D_ATT), F32),
        jax.ShapeDtypeStruct((nb, l, D_SSM), F32),
        jax.ShapeDtypeStruct((nb, l, CONV_CH), F32),
        jax.ShapeDtypeStruct((nb, l, D_SSM), F32),
        jax.ShapeDtypeStruct((nb, H_ATT, l), F32),
        jax.ShapeDtypeStruct((nb, H_ATT, l), F32),
        jax.ShapeDtypeStruct((nb, H_SSM, l), F32),
    )
    out_specs = (tok(D_ATT), tok(D_ATT), tok(D_ATT), tok(D_ATT), tok(D_ATT), tok(D_SSM),
                 tok(CONV_CH), tok(D_SSM), tok_t, tok_t, tok_t)
    return pl.pallas_call(
        _inproj_kernel,
        out_shape=out_shape,
        grid=(nb, l // bt),
        in_specs=[tok(D_MODEL), const((1, D_MODEL)), const((D_MODEL, _MAIN_COLS)),
                  const((2 * H_ATT, D_MODEL)), const((H_ATT, 1)), const((H_SSM, 1)),
                  const((1, D_SSM)), const((bt, bt))],
        out_specs=out_specs,
        scratch_shapes=[pltpu.VMEM((H_ATT, LANES), F32)],
        compiler_params=_cparams(2),
        name="in_projection",
    )(x, g, w_main, w_small_t, b_forget, dt_bias, dt_bias_x, tri)


def _fox_prompt_kernel(q_ref, k_ref, v_ref, c_ref, o_ref, *, blk):
    qi = pl.program_id(2)
    q = q_ref[0]
    lane = lax.broadcasted_iota(jnp.int32, (blk, LANES), 1)
    first = lane < HD_ATT
    zero = jnp.zeros_like(q)
    q_heads = (jnp.where(first, q, zero), jnp.where(first, zero, q))
    q0 = pl.multiple_of(qi * blk, blk)
    cbase = c_ref[0, 0, :, pl.ds(q0, LANES)][:, 0:1]

    def tile(j, carry, diagonal):
        k0 = pl.multiple_of(j * blk, blk)
        kj = k_ref[0, pl.ds(k0, blk), :]
        vj = v_ref[0, pl.ds(k0, blk), :]
        bias = cbase - c_ref[0, 0, :, pl.ds(k0, blk)]
        out = []
        for hh in range(2):
            m, l, acc = carry[hh]
            s = _dot_nt(q_heads[hh], kj) + bias[hh:hh + 1, :]
            if diagonal:
                row = lax.broadcasted_iota(jnp.int32, (blk, blk), 0)
                col = lax.broadcasted_iota(jnp.int32, (blk, blk), 1)
                s = jnp.where(row >= col, s, NEG_BIG)
            m_new = jnp.maximum(m, jnp.max(s, axis=-1, keepdims=True))
            p = jnp.exp(s - m_new)
            alpha = jnp.exp(m - m_new)
            l = alpha * l + jnp.sum(p, axis=-1, keepdims=True)
            acc = alpha * acc + _dot(p.astype(BF16), vj)
            out.append((m_new, l, acc))
        return tuple(out)

    init = tuple((jnp.full((blk, 1), NEG_BIG, F32), jnp.zeros((blk, 1), F32),
                  jnp.zeros((blk, LANES), F32)) for _ in range(2))
    carry = lax.fori_loop(0, qi, lambda j, c: tile(j, c, False), init)
    (_, l_a, acc_a), (_, l_b, acc_b) = tile(qi, carry, True)
    o_ref[0] = jnp.where(first, acc_a / l_a, acc_b / l_b).astype(o_ref.dtype)


def _fox_prompt(qb, kb, vb, ct, blk):
    nb, l, _ = qb.shape
    n_pairs = D_ATT // LANES
    c4 = ct.reshape(nb, n_pairs, 2, l)
    return pl.pallas_call(
        functools.partial(_fox_prompt_kernel, blk=blk),
        out_shape=jax.ShapeDtypeStruct((nb, l, D_ATT), BF16),
        grid=(nb, n_pairs, l // blk),
        in_specs=[pl.BlockSpec((1, blk, LANES), lambda b, p, i: (b, i, p)),
                  pl.BlockSpec((1, l, LANES), lambda b, p, i: (b, 0, p)),
                  pl.BlockSpec((1, l, LANES), lambda b, p, i: (b, 0, p)),
                  pl.BlockSpec((1, 1, 2, l), lambda b, p, i: (b, p, 0, 0))],
        out_specs=pl.BlockSpec((1, blk, LANES), lambda b, p, i: (b, i, p)),
        compiler_params=_cparams(3),
        name="fox_prompt",
    )(qb, kb, vb, c4)


def _ssd_gate_norm(y, z, g):
    return _rms(y * _silu(z), g)


def _ssd_prompt_kernel(xbc_ref, z_ref, dtx_ref, dtt_ref, cw_ref, cb_ref, alx_ref, alc_ref,
                       dsx_ref, g_ref, tril_ref, triu_ref,
                       y_ref, st_ref, xp_ref, state_ref):
    q = SSD_CHUNK
    c = pl.program_id(1)

    @pl.when(c == 0)
    def _():
        xp_ref[0:8, :] = jnp.zeros((8, CONV_CH), F32)
        state_ref[...] = jnp.zeros_like(state_ref)

    xp_ref[8:8 + q, :] = xbc_ref[0]
    acc = cb_ref[...]
    for tap in range(CONV_WIDTH):
        off = 8 - (CONV_WIDTH - 1) + tap
        acc = acc + xp_ref[off:off + q, :] * cw_ref[tap:tap + 1, :]
    xp_ref[0:8, :] = xp_ref[q:q + 8, :]
    act = _silu(acc)
    xs = act[:, :D_SSM]
    bmat = act[:, D_SSM:D_SSM + N_BC_GROUPS * N_STATE]
    cmat = act[:, D_SSM + N_BC_GROUPS * N_STATE:]

    dt = dtx_ref[0]
    a_x = -jnp.exp(alx_ref[...]) * dt
    hi, mid, lo = _split3(a_x)
    parts = jnp.concatenate([hi, mid, lo], axis=1).astype(BF16)
    r = _dot(tril_ref[...], parts)
    acs_x = r[:, 0:D_SSM] + r[:, D_SSM:2 * D_SSM] + r[:, 2 * D_SSM:3 * D_SSM]
    a_t = -jnp.exp(alc_ref[...]) * dtt_ref[0]
    acs_t = _cumsum_lanes(a_t, triu_ref[...])

    xdt = xs * dt
    a_last = acs_x[q - 1:q, :]
    xdt_end = (xdt * jnp.exp(a_last - acs_x)).astype(BF16)
    e_acs = jnp.exp(acs_x)
    xdt_b = xdt.astype(BF16)
    state_old = state_ref[...]
    state_b = state_old.astype(BF16)

    row = lax.broadcasted_iota(jnp.int32, (q, q), 0)
    col = lax.broadcasted_iota(jnp.int32, (q, q), 1)
    causal = row >= col
    first = lax.broadcasted_iota(jnp.int32, (q, LANES), 1) < P_SSM

    y_parts = []
    for g in range(N_BC_GROUPS):
        cg = cmat[:, g * N_STATE:(g + 1) * N_STATE].astype(BF16)
        bg = bmat[:, g * N_STATE:(g + 1) * N_STATE].astype(BF16)
        cb = _dot_nt(cg, bg)
        gl = slice(g * GROUP_W, (g + 1) * GROUP_W)
        y_off = _dot(cg, state_b[:, gl]) * e_acs[:, gl]
        for pr in range(GROUP_W // LANES):
            lo_lane = g * GROUP_W + pr * LANES
            pair = []
            for hh in range(2):
                h = lo_lane // P_SSM + hh
                seg = acs_x[:, h * P_SSM:h * P_SSM + 1] - acs_t[h:h + 1, :]
                decay = jnp.exp(jnp.where(causal, seg, -jnp.inf))
                gm = (cb * decay).astype(BF16)
                pair.append(_dot(gm, xdt_b[:, lo_lane:lo_lane + LANES]))
            y_parts.append(jnp.where(first, pair[0], pair[1]) + y_off[:, pr * LANES:(pr + 1) * LANES])
        state_ref[:, gl] = jnp.exp(a_last[:, gl]) * state_old[:, gl] + _dot_tn(bg, xdt_end[:, gl])
    y = jnp.concatenate(y_parts, axis=1) + dsx_ref[...] * xs
    y_ref[0] = _ssd_gate_norm(y, z_ref[0], g_ref[...]).astype(y_ref.dtype)

    @pl.when(c == pl.num_programs(1) - 1)
    def _():
        st_ref[0] = state_ref[...].T


def _ssd_prompt(xbc, z, dtx, dtt, conv_w, conv_b, a_log_x, a_log_c, d_skip_x, norm_g):
    nb, l, _ = xbc.shape
    q = SSD_CHUNK
    tril = jnp.tril(jnp.ones((q, q), F32)).astype(BF16)
    triu = jnp.triu(jnp.ones((q, q), F32)).astype(BF16)
    tok = lambda w: pl.BlockSpec((1, q, w), lambda b, c: (b, c, 0))
    const = lambda shape: pl.BlockSpec(shape, lambda b, c: (0,) * len(shape))
    return pl.pallas_call(
        _ssd_prompt_kernel,
        out_shape=(jax.ShapeDtypeStruct((nb, l, D_SSM), BF16),
                   jax.ShapeDtypeStruct((nb, D_SSM, N_STATE), F32)),
        grid=(nb, l // q),
        in_specs=[tok(CONV_CH), tok(D_SSM), tok(D_SSM),
                  pl.BlockSpec((1, H_SSM, q), lambda b, c: (b, 0, c)),
                  const((CONV_WIDTH, CONV_CH)), const((1, CONV_CH)), const((1, D_SSM)),
                  const((H_SSM, 1)), const((1, D_SSM)), const((1, D_SSM)),
                  const((q, q)), const((q, q))],
        out_specs=(tok(D_SSM), pl.BlockSpec((1, D_SSM, N_STATE), lambda b, c: (b, 0, 0))),
        scratch_shapes=[pltpu.VMEM((q + 8, CONV_CH), F32), pltpu.VMEM((N_STATE, D_SSM), F32)],
        compiler_params=_cparams(2),
        name="ssd_prompt",
    )(xbc, z, dtx, dtt, conv_w, conv_b, a_log_x, a_log_c, d_skip_x, norm_g, tril, triu)


def _mix_out_kernel(x_ref, att_ref, ssm_ref, wo_ref, g_ref, wq_ref, x1_ref, qc_ref):
    x1 = x_ref[...] + _dot(att_ref[...], wo_ref[0:D_ATT, :]) + _dot(ssm_ref[...], wo_ref[D_ATT:, :])
    x1_ref[...] = x1
    h = _rms(x1, g_ref[...]).astype(BF16)
    qc_ref[...] = (_dot(h, wq_ref[...]) * X_SCALE).astype(BF16)


def _mix_out(x, att, ssm, w_out, g_cross, w_cq, bt):
    t = x.shape[0]
    tok = lambda w: pl.BlockSpec((bt, w), lambda i: (i, 0))
    const = lambda shape: pl.BlockSpec(shape, lambda i: (0,) * len(shape))
    return pl.pallas_call(
        _mix_out_kernel,
        out_shape=(jax.ShapeDtypeStruct((t, D_MODEL), F32), jax.ShapeDtypeStruct((t, D_MODEL), BF16)),
        grid=(t // bt,),
        in_specs=[tok(D_MODEL), tok(D_ATT), tok(D_SSM), const((D_MODEL, D_MODEL)),
                  const((1, D_MODEL)), const((D_MODEL, D_MODEL))],
        out_specs=(tok(D_MODEL), tok(D_MODEL)),
        compiler_params=_cparams(1),
        name="mix_out",
    )(x, att, ssm, w_out, g_cross, w_cq)


def _ffn_kernel(x1_ref, o_ref, wco_ref, g_ref, wg_ref, wu_ref, wd_ref, gf_ref, y_ref):
    x2 = x1_ref[...] + _dot(o_ref[...], wco_ref[...])
    h = _rms(x2, g_ref[...]).astype(BF16)
    u = (_silu(_dot(h, wg_ref[...])) * _dot(h, wu_ref[...])).astype(BF16)
    x3 = x2 + _dot(u, wd_ref[...])
    y_ref[...] = _rms(x3, gf_ref[...])


def _ffn(x1, o, w_co, g_ffn, w_gate, w_up, w_down, g_final, bt):
    t = x1.shape[0]
    d_ff = w_gate.shape[1]
    tok = lambda w: pl.BlockSpec((bt, w), lambda i: (i, 0))
    const = lambda shape: pl.BlockSpec(shape, lambda i: (0,) * len(shape),
                                       pipeline_mode=pl.Buffered(1))
    return pl.pallas_call(
        _ffn_kernel,
        out_shape=jax.ShapeDtypeStruct((t, D_MODEL), F32),
        grid=(t // bt,),
        in_specs=[tok(D_MODEL), tok(D_MODEL), const((D_MODEL, D_MODEL)), const((1, D_MODEL)),
                  const((D_MODEL, d_ff)), const((D_MODEL, d_ff)), const((d_ff, D_MODEL)),
                  const((1, D_MODEL))],
        out_specs=tok(D_MODEL),
        compiler_params=_cparams(1),
        name="cross_out_ffn",
    )(x1, o, w_co, g_ffn, w_gate, w_up, w_down, g_final)


def _memory_kv_kernel(mem_ref, g_ref, wk_ref, wv_ref, k32_ref, v32_ref, kb_ref, vb_ref):
    mn = _rms(mem_ref[0], g_ref[...]).astype(BF16)
    k = _dot(mn, wk_ref[...])
    v = _dot(mn, wv_ref[...])
    k32_ref[0] = k
    v32_ref[0] = v
    kb_ref[0] = k.astype(BF16)
    vb_ref[0] = v.astype(BF16)


def _memory_kv(mem, g, w_ck, w_cv):
    nb, m, _ = mem.shape
    blk = pl.BlockSpec((1, m, D_MODEL), lambda b: (b, 0, 0))
    const = lambda shape: pl.BlockSpec(shape, lambda b: (0,) * len(shape))
    f = jax.ShapeDtypeStruct((nb, m, D_MODEL), F32)
    h = jax.ShapeDtypeStruct((nb, m, D_MODEL), BF16)
    return pl.pallas_call(
        _memory_kv_kernel,
        out_shape=(f, f, h, h),
        grid=(nb,),
        in_specs=[blk, const((1, D_MODEL)), const((D_MODEL, D_MODEL)), const((D_MODEL, D_MODEL))],
        out_specs=(blk, blk, blk, blk),
        compiler_params=_cparams(1),
        name="memory_kv",
    )(mem, g, w_ck, w_cv)


def _softmax_rows(s):
    m = jnp.max(s, axis=-1, keepdims=True)
    p = jnp.exp(s - m)
    return p / jnp.sum(p, axis=-1, keepdims=True)


def _cross_prompt_kernel(q_ref, k_ref, v_ref, o_ref):
    for h in range(H_X):
        hl = slice(h * HD_X, (h + 1) * HD_X)
        p = _softmax_rows(_dot_nt(q_ref[0, :, hl], k_ref[0, :, hl]))
        o_ref[0, :, hl] = _dot(p.astype(BF16), v_ref[0, :, hl]).astype(o_ref.dtype)


def _cross_prompt(qc, mk, mv, bt):
    nb, l, _ = qc.shape
    m = mk.shape[1]
    tok = pl.BlockSpec((1, bt, D_MODEL), lambda b, i: (b, i, 0))
    mem = pl.BlockSpec((1, m, D_MODEL), lambda b, i: (b, 0, 0))
    return pl.pallas_call(
        _cross_prompt_kernel,
        out_shape=jax.ShapeDtypeStruct((nb, l, D_MODEL), BF16),
        grid=(nb, l // bt),
        in_specs=[tok, mem, mem],
        out_specs=tok,
        compiler_params=_cparams(2),
        name="cross_prompt",
    )(qc, mk, mv)


def _cross_sample_kernel(q_ref, k_ref, v_ref, o_ref):
    for h in range(H_X):
        hl = slice(h * HD_X, (h + 1) * HD_X)
        k = k_ref[:, :, h, :].astype(BF16)
        v = v_ref[:, :, h, :].astype(BF16)
        s = jnp.einsum("sqd,smd->sqm", q_ref[:, :, hl], k, preferred_element_type=F32)
        p = _softmax_rows(s)
        o = jnp.einsum("sqm,smd->sqd", p.astype(BF16), v, preferred_element_type=F32)
        o_ref[:, :, hl] = o.astype(o_ref.dtype)


def _cross_sample(qc, mem_k, mem_v, n_seq_blk):
    ns, lq, _ = qc.shape
    m = mem_k.shape[1]
    tok = pl.BlockSpec((n_seq_blk, lq, D_MODEL), lambda i: (i, 0, 0))
    mem = pl.BlockSpec((n_seq_blk, m, H_X, HD_X), lambda i: (i, 0, 0, 0))
    return pl.pallas_call(
        _cross_sample_kernel,
        out_shape=jax.ShapeDtypeStruct((ns, lq, D_MODEL), BF16),
        grid=(ns // n_seq_blk,),
        in_specs=[tok, mem, mem],
        out_specs=tok,
        compiler_params=_cparams(1),
        name="cross_sample",
    )(qc, mem_k, mem_v)


def _fox_sample_kernel(pt_ref, q_ref, kn_ref, vn_ref, lfn_ref, tril_ref, *rest, n_pages, lq):
    del pt_ref
    k_refs = rest[0:n_pages]
    v_refs = rest[n_pages:2 * n_pages]
    lf_refs = rest[2 * n_pages:3 * n_pages]
    o_ref = rest[3 * n_pages]
    kt_ref = rest[3 * n_pages + 1]
    vt_ref = rest[3 * n_pages + 2]

    for j in range(n_pages):
        kt_ref[:, :, j * PAGE_SIZE:(j + 1) * PAGE_SIZE] = k_refs[j][0].astype(BF16)
        vt_ref[:, :, j * PAGE_SIZE:(j + 1) * PAGE_SIZE] = v_refs[j][0].astype(BF16)

    lf_all = jnp.concatenate([r[0] for r in lf_refs], axis=0)
    hi, mid, lo = _split3(lf_all)
    tril = tril_ref[...]
    incl = _dot(hi.astype(BF16), tril) + _dot(mid.astype(BF16), tril) + _dot(lo.astype(BF16), tril)
    after = jnp.zeros((H_ATT, 1), F32)
    pieces = [None] * n_pages
    for j in reversed(range(n_pages)):
        sl = slice(j * H_ATT, (j + 1) * H_ATT)
        pieces[j] = incl[sl] - lf_all[sl] + after
        after = after + incl[sl][:, 0:1]
    bias = jnp.concatenate(pieces, axis=1)

    lfn = lfn_ref[0]
    lane = lax.broadcasted_iota(jnp.int32, (H_ATT, lq), 1)
    cn = jnp.zeros((H_ATT, lq), F32)
    for t in range(lq):
        cn = cn + jnp.where(lane >= t, lfn[:, t:t + 1], 0.0)
    causal = (lax.broadcasted_iota(jnp.int32, (lq, lq), 1)
              <= lax.broadcasted_iota(jnp.int32, (lq, lq), 0))

    for h in range(H_ATT):
        qh = q_ref[0, h]
        s = _dot(qh, kt_ref[h]) + bias[h:h + 1, :]
        s_new = jnp.where(causal, _dot_nt(qh, kn_ref[0, h]) - cn[h:h + 1, :], NEG_BIG)
        m = jnp.maximum(jnp.max(s, axis=-1, keepdims=True), jnp.max(s_new, axis=-1, keepdims=True))
        p = jnp.exp(s - m)
        p_new = jnp.exp(s_new - m)
        l = jnp.sum(p, axis=-1, keepdims=True) + jnp.sum(p_new, axis=-1, keepdims=True)
        o = _dot_nt(p.astype(BF16), vt_ref[h]) + _dot(p_new.astype(BF16), vn_ref[0, h])
        o_ref[0, h] = (o / l).astype(o_ref.dtype)


def _fox_sample(qh, kh, vh, lfn, cache_kt, cache_vt, cache_lft, page_table):
    ns, _, lq, _ = qh.shape
    n_pages = page_table.shape[1]
    past = n_pages * PAGE_SIZE
    tril = jnp.tril(jnp.ones((PAGE_SIZE, PAGE_SIZE), F32)).astype(BF16)
    tok = pl.BlockSpec((1, H_ATT, lq, HD_ATT), lambda i, pt: (i, 0, 0, 0))

    def page(j, shape):
        return pl.BlockSpec((1,) + shape, lambda i, pt: (pt[i * n_pages + j],) + (0,) * len(shape))

    in_specs = [tok, tok, tok, pl.BlockSpec((1, H_ATT, lq), lambda i, pt: (i, 0, 0)),
                pl.BlockSpec((PAGE_SIZE, PAGE_SIZE), lambda i, pt: (0, 0))]
    in_specs += [page(j, (H_ATT, HD_ATT, PAGE_SIZE)) for j in range(n_pages)]
    in_specs += [page(j, (H_ATT, HD_ATT, PAGE_SIZE)) for j in range(n_pages)]
    in_specs += [page(j, (H_ATT, PAGE_SIZE)) for j in range(n_pages)]
    grid_spec = pltpu.PrefetchScalarGridSpec(
        num_scalar_prefetch=1,
        grid=(ns,),
        in_specs=in_specs,
        out_specs=tok,
        scratch_shapes=[pltpu.VMEM((H_ATT, HD_ATT, past), BF16), pltpu.VMEM((H_ATT, HD_ATT, past), BF16)],
    )
    return pl.pallas_call(
        functools.partial(_fox_sample_kernel, n_pages=n_pages, lq=lq),
        out_shape=jax.ShapeDtypeStruct((ns, H_ATT, lq, HD_ATT), BF16),
        grid_spec=grid_spec,
        compiler_params=_cparams(1),
        name="fox_sample",
    )(page_table.reshape(-1), qh, kh, vh, lfn, tril,
      *([cache_kt] * n_pages), *([cache_vt] * n_pages), *([cache_lft] * n_pages))


def _ssd_sample_kernel(xbc_ref, z_ref, dtx_ref, cs_ref, st_ref, cw_ref, cb_ref, alx_ref, dsx_ref,
                       g_ref, y_ref, sto_ref, xp_ref, rows_ref, brow_ref, *, lq):
    nsb = xbc_ref.shape[0]
    assert nsb * (lq + 1) <= LANES
    kw = CONV_WIDTH - 1
    xp_ref[:, 0:kw, :] = cs_ref[...]
    xp_ref[:, kw:kw + lq, :] = xbc_ref[...]
    acc = cb_ref[...]
    for tap in range(CONV_WIDTH):
        acc = acc + xp_ref[:, tap:tap + lq, :] * cw_ref[tap:tap + 1, :]
    act = _silu(acc)
    xs = act[:, :, :D_SSM]
    bmat = act[:, :, D_SSM:D_SSM + N_BC_GROUPS * N_STATE]
    cmat = act[:, :, D_SSM + N_BC_GROUPS * N_STATE:]
    dt = dtx_ref[...]
    a = -jnp.exp(alx_ref[...]) * dt
    xdt = xs * dt
    acs = [a[:, 0:1, :]]
    for t in range(1, lq):
        acs.append(acs[-1] + a[:, t:t + 1, :])
    a_last = acs[lq - 1]

    ccat = jnp.concatenate([cmat[:, :, g * N_STATE:(g + 1) * N_STATE] for g in range(N_BC_GROUPS)],
                           axis=1).astype(BF16)
    state = st_ref[...]
    r = jnp.einsum("sgn,shn->sgh", ccat, state.astype(BF16), preferred_element_type=F32)
    group0 = lax.broadcasted_iota(jnp.int32, (1, 1, D_SSM), 2) < GROUP_W
    y_off = jnp.where(group0, r[:, 0:lq, :], r[:, lq:2 * lq, :])

    ys = []
    for t in range(lq):
        y_t = jnp.exp(acs[t]) * y_off[:, t:t + 1, :] + dsx_ref[...] * xs[:, t:t + 1, :]
        for s in range(t + 1):
            cb = jnp.concatenate(
                [jnp.broadcast_to(
                    jnp.sum(cmat[:, t:t + 1, g * N_STATE:(g + 1) * N_STATE]
                            * bmat[:, s:s + 1, g * N_STATE:(g + 1) * N_STATE], axis=-1, keepdims=True),
                    (nsb, 1, GROUP_W)) for g in range(N_BC_GROUPS)], axis=2)
            y_t = y_t + cb * jnp.exp(acs[t] - acs[s]) * xdt[:, s:s + 1, :]
        ys.append(y_t)
    y = jnp.concatenate(ys, axis=1)
    y_ref[...] = _ssd_gate_norm(y, z_ref[...], g_ref[...]).astype(y_ref.dtype)

    upd = jnp.concatenate([xdt[:, s:s + 1, :] * jnp.exp(a_last - acs[s]) for s in range(lq)], axis=1)
    e_last = jnp.exp(a_last)
    rows_ref[...] = jnp.zeros_like(rows_ref)
    brow_ref[...] = jnp.zeros_like(brow_ref)
    for i in range(nsb):
        rows_ref[i * lq:(i + 1) * lq, :] = upd[i]
        rows_ref[nsb * lq + i:nsb * lq + i + 1, :] = e_last[i]
        brow_ref[i * lq:(i + 1) * lq, :] = bmat[i]
    cols = rows_ref[...].T
    upd_cols = cols.astype(BF16)
    b_all = brow_ref[...]
    row_seq = lax.broadcasted_iota(jnp.int32, (LANES, N_BC_GROUPS * N_STATE), 0) // lq
    top = lax.broadcasted_iota(jnp.int32, (D_SSM, N_STATE), 0) < GROUP_W
    for i in range(nsb):
        b_rows = jnp.where(row_seq == i, b_all, 0.0).astype(BF16)
        m = _dot(upd_cols, b_rows)
        add = jnp.where(top, m[:, 0:N_STATE], m[:, N_STATE:2 * N_STATE])
        decay = jnp.broadcast_to(cols[:, nsb * lq + i:nsb * lq + i + 1], (D_SSM, N_STATE))
        sto_ref[i] = decay * state[i] + add


def _ssd_sample(xbc, z, dtx, conv_state, state, conv_w, conv_b, a_log_x, d_skip_x, norm_g, nsb):
    ns, lq, _ = xbc.shape
    seq = lambda r, w: pl.BlockSpec((nsb, r, w), lambda i: (i, 0, 0))
    const = lambda shape: pl.BlockSpec(shape, lambda i: (0,) * len(shape))
    return pl.pallas_call(
        functools.partial(_ssd_sample_kernel, lq=lq),
        out_shape=(jax.ShapeDtypeStruct((ns, lq, D_SSM), BF16),
                   jax.ShapeDtypeStruct((ns, D_SSM, N_STATE), F32)),
        grid=(ns // nsb,),
        in_specs=[seq(lq, CONV_CH), seq(lq, D_SSM), seq(lq, D_SSM), seq(CONV_WIDTH - 1, CONV_CH),
                  seq(D_SSM, N_STATE), const((CONV_WIDTH, CONV_CH)), const((1, CONV_CH)),
                  const((1, D_SSM)), const((1, D_SSM)), const((1, D_SSM))],
        out_specs=(seq(lq, D_SSM), seq(D_SSM, N_STATE)),
        scratch_shapes=[pltpu.VMEM((nsb, 8, CONV_CH), F32), pltpu.VMEM((LANES, D_SSM), F32),
                        pltpu.VMEM((LANES, N_BC_GROUPS * N_STATE), F32)],
        compiler_params=_cparams(1),
        name="ssd_sample",
    )(xbc, z, dtx, conv_state, state, conv_w, conv_b, a_log_x, d_skip_x, norm_g)


def _pick(pref, n):
    return pref if n % pref == 0 else n


def kernel(x_prompt, x_sample, mem_prompt, cache_k, cache_v, cache_logf, page_table, cache_mem_k, cache_mem_v, state_conv, state_ssm, norm_mix_g, w_in, b_forget, conv_w, conv_b, dt_bias, a_log, d_skip, ssm_norm_g, w_out, norm_cross_g, norm_mem_g, w_cq, w_ck, w_cv, w_co, norm_ffn_g, w_gate, w_up, w_down, final_norm_g):
    assert w_in.shape[0] == 1, "one layer"
    nb, l, _ = x_prompt.shape
    ns, lq, _ = x_sample.shape
    n_phys = cache_k.shape[1]
    row = lambda v: v.reshape(1, -1).astype(F32)
    colv = lambda v: v.reshape(-1, 1).astype(F32)
    per_ch = lambda v: jnp.repeat(v.astype(F32), P_SSM).reshape(1, D_SSM)

    w = w_in[0]
    cuts = [D_ATT, 2 * D_ATT, 3 * D_ATT, 3 * D_ATT + H_ATT, 3 * D_ATT + H_ATT + D_SSM,
            3 * D_ATT + H_ATT + D_SSM + CONV_CH]
    w_q, w_k, w_v, w_f, w_z, w_xbc, w_dt = jnp.split(w, cuts, axis=1)
    w_main = jnp.concatenate([w_q, w_k, w_v, w_z, w_xbc, jnp.repeat(w_dt, P_SSM, axis=1)],
                             axis=1).astype(BF16)
    w_small_t = jnp.concatenate([w_f, w_dt], axis=1).T.astype(BF16)
    in_params = (row(norm_mix_g[0]), w_main, w_small_t, colv(b_forget[0]), colv(dt_bias[0]),
                 per_ch(dt_bias[0]))
    ssd_params = (conv_w[0], row(conv_b[0]), per_ch(a_log[0]))
    ssd_tail = (per_ch(d_skip[0]), row(ssm_norm_g[0]))
    w_out_b, w_cq_b, w_co_b = w_out[0].astype(BF16), w_cq[0].astype(BF16), w_co[0].astype(BF16)
    w_gate_b, w_up_b, w_down_b = w_gate[0].astype(BF16), w_up[0].astype(BF16), w_down[0].astype(BF16)
    g_cross, g_ffn, g_final = row(norm_cross_g[0]), row(norm_ffn_g[0]), row(final_norm_g)

    def tail(x, att, ssm, cross, bt):
        x1, qc = _mix_out(x, att, ssm, w_out_b, g_cross, w_cq_b, bt)
        o = cross(qc)
        return _ffn(x1, o, w_co_b, g_ffn, w_gate_b, w_up_b, w_down_b, g_final, bt)

    bt = _pick(512, l)
    qb, kb, vb, k32, v32, z, xbc, dtx, logft, ct, dtt = _in_projection(x_prompt, *in_params, bt)
    att = _fox_prompt(qb, kb, vb, ct, _pick(512, l))
    ssm, st_p = _ssd_prompt(xbc, z, dtx, dtt, *ssd_params, colv(a_log[0]), *ssd_tail)
    mk32, mv32, mkb, mvb = _memory_kv(mem_prompt, row(norm_mem_g[0]), w_ck[0].astype(BF16),
                                      w_cv[0].astype(BF16))
    cross_p = lambda qc: _cross_prompt(qc.reshape(nb, l, D_MODEL), mkb, mvb, bt).reshape(nb * l, D_MODEL)
    y_prompt = tail(x_prompt.reshape(nb * l, D_MODEL), att.reshape(nb * l, D_ATT),
                    ssm.reshape(nb * l, D_SSM), cross_p, bt).reshape(nb, l, D_MODEL)

    ts = ns * lq
    bts = _pick(512, ts)
    qb_s, kb_s, vb_s, k32_s, v32_s, z_s, xbc_s, dtx_s, logft_s, _, _ = _in_projection(
        x_sample.reshape(1, ts, D_MODEL), *in_params, bts)
    seq3 = lambda a: a.reshape(ns, lq, a.shape[-1])
    lfn = logft_s.reshape(H_ATT, ns, lq).transpose(1, 0, 2)
    head_major = lambda a: a.reshape(ns, lq, H_ATT, HD_ATT).transpose(0, 2, 1, 3)
    att_s = _fox_sample(head_major(qb_s), head_major(kb_s), head_major(vb_s), lfn,
                        cache_k[0].transpose(0, 2, 3, 1), cache_v[0].transpose(0, 2, 3, 1),
                        cache_logf[0].transpose(0, 2, 1), page_table)
    att_s = att_s.transpose(0, 2, 1, 3)
    ssm_s, st_s = _ssd_sample(seq3(xbc_s), seq3(z_s), seq3(dtx_s), state_conv[0],
                              state_ssm[0].reshape(ns, D_SSM, N_STATE), *ssd_params, *ssd_tail,
                              _pick(8, ns))
    cross_s = lambda qc: _cross_sample(qc.reshape(ns, lq, D_MODEL), cache_mem_k[0], cache_mem_v[0],
                                       _pick(4, ns)).reshape(ts, D_MODEL)
    y_sample = tail(x_sample.reshape(ts, D_MODEL), att_s.reshape(ts, D_ATT), ssm_s.reshape(ts, D_SSM),
                    cross_s, bts).reshape(ns, lq, D_MODEL)

    heads = lambda a, n: a.reshape(1, n, -1, H_ATT, HD_ATT)
    n_mem = mem_prompt.shape[1]
    kw = CONV_WIDTH - 1
    return (y_prompt, y_sample,
            heads(k32, nb), heads(v32, nb), logft.transpose(0, 2, 1)[None],
            mk32.reshape(1, nb, n_mem, H_X, HD_X), mv32.reshape(1, nb, n_mem, H_X, HD_X),
            xbc[:, l - kw:, :][None], st_p.reshape(1, nb, H_SSM, P_SSM, N_STATE),
            heads(k32_s, ns), heads(v32_s, ns),
            logft_s.reshape(H_ATT, ns, lq).transpose(1, 2, 0)[None],
            seq3(xbc_s)[:, lq - kw:, :][None], st_s.reshape(1, ns, H_SSM, P_SSM, N_STATE))
```

```python
import functools
import math

import numpy as np
import jax
import jax.numpy as jnp
from jax import lax
from jax.experimental import pallas as pl
from jax.experimental.pallas import tpu as pltpu

F32 = jnp.float32
BF16 = jnp.bfloat16

D_MODEL = 1024
D_ATT = 512
H_ATT = 8
HD_ATT = 64
D_SSM = 512
H_SSM = 8
P_SSM = 64
N_STATE = 128
N_BC_GROUPS = 2
GROUP_W = D_SSM // N_BC_GROUPS
CONV_WIDTH = 4
CONV_CH = D_SSM + 2 * N_BC_GROUPS * N_STATE
H_X = 4
HD_X = 256
EPS = 1e-6
ATT_SCALE = HD_ATT ** -0.5
X_SCALE = HD_X ** -0.5
SSD_CHUNK = 128
PAGE_SIZE = 128
NEG_BIG = -1e30
LOG2E = math.log2(math.e)

LANES = 128
ONES_ROWS = 16
VMEM_LIMIT = 56 * 1024 * 1024

_Q0, _K0, _Z0, _XBC0, _DT0, _F0, _MAIN_COLS = 0, 512, 1024, 1536, 2560, 3072, 3200


def _cparams(n_axes):
    return pltpu.CompilerParams(dimension_semantics=("arbitrary",) * n_axes,
                                vmem_limit_bytes=VMEM_LIMIT)


def _rms(x, g):
    ms = jnp.mean(x * x, axis=-1, keepdims=True)
    return x * lax.rsqrt(ms + EPS) * g


def _softplus(x):
    return jnp.maximum(x, 0.0) + jnp.log1p(jnp.exp(-jnp.abs(x)))


def _log_sigmoid(x):
    return jnp.minimum(x, 0.0) - jnp.log1p(jnp.exp(-jnp.abs(x)))


def _silu(x):
    return x * (1.0 / (1.0 + jnp.exp(-x)))


def _split3(x):
    hi = x.astype(BF16).astype(F32)
    r1 = x - hi
    mid = r1.astype(BF16).astype(F32)
    lo = (r1 - mid).astype(BF16).astype(F32)
    return hi, mid, lo


def _dot(a, b):
    return jnp.dot(a, b, preferred_element_type=F32)


def _dot_nt(a, b):
    return lax.dot_general(a, b, (((1,), (1,)), ((), ())), preferred_element_type=F32)


def _dot_tn(a, b):
    return lax.dot_general(a, b, (((0,), (0,)), ((), ())), preferred_element_type=F32)


def _cumsum_lanes(x, tri_upper):
    hi, mid, lo = _split3(x)
    parts = jnp.concatenate([hi, mid, lo], axis=0).astype(BF16)
    r = _dot(parts, tri_upper)
    return r[0:8] + r[8:16] + r[16:24]


def _inproj_kernel(x_ref, g_ref, wm_ref, wvt_ref, wst_ref, bf_ref, dtb_ref, dtbx_ref, bfx_ref,
                   tril_ref, sel_ref, ones_ref,
                   qx_ref, kx_ref, k32_ref, vtb_ref, v32t_ref, z_ref, xbc_ref, dtx_ref,
                   logft_ref, dtt_ref, carry_ref):
    @pl.when(pl.program_id(1) == 0)
    def _():
        carry_ref[...] = jnp.zeros_like(carry_ref)

    h = _rms(x_ref[0], g_ref[...]).astype(BF16)
    q = (_dot(h, wm_ref[:, _Q0:_K0]) * (ATT_SCALE * LOG2E)).astype(BF16)
    k = _dot(h, wm_ref[:, _K0:_Z0])
    k32_ref[0] = k
    kb = k.astype(BF16)
    vt = _dot_nt(wvt_ref[...], h)
    v32t_ref[0] = vt
    vtb_ref[0] = vt.astype(BF16)
    z_ref[0] = _dot(h, wm_ref[:, _Z0:_XBC0])
    xbc_ref[0] = _dot(h, wm_ref[:, _XBC0:_DT0])
    dtx_ref[0] = _softplus(_dot(h, wm_ref[:, _DT0:_F0]) + dtbx_ref[...])

    small = _dot_nt(wst_ref[...], h)
    logft_ref[0] = _log_sigmoid(small[0:8] + bf_ref[...])
    dtt_ref[0] = _softplus(small[8:16] + dtb_ref[...])

    lf2 = _log_sigmoid(_dot(h, wm_ref[:, _F0:_MAIN_COLS]) + bfx_ref[...]) * LOG2E
    parts = jnp.concatenate(_split3(lf2), axis=1).astype(BF16)
    r = _dot(tril_ref[...], parts)
    c2 = r[:, 0:LANES] + r[:, LANES:2 * LANES] + r[:, 2 * LANES:3 * LANES] + carry_ref[0:1, :]
    bt = c2.shape[0]
    carry_ref[...] = jnp.broadcast_to(c2[bt - 1:bt, :], carry_ref.shape)
    csplit = jnp.concatenate(_split3(c2), axis=1).astype(BF16)
    ext = (_dot(csplit, sel_ref[...]) + ones_ref[...]).astype(BF16)
    for p in range(D_ATT // LANES):
        pl_ = slice(p * LANES, (p + 1) * LANES)
        kx_ref[0, :, 2 * p * LANES:(2 * p + 1) * LANES] = kb[:, pl_]
        kx_ref[0, :, (2 * p + 1) * LANES:(2 * p + 2) * LANES] = ext[:, pl_]
        qx_ref[0, :, 2 * p * LANES:(2 * p + 1) * LANES] = q[:, pl_]
        qx_ref[0, :, (2 * p + 1) * LANES:(2 * p + 2) * LANES] = ext[:, D_ATT + p * LANES:D_ATT + (p + 1) * LANES]


def _attention_extras():
    sel = np.zeros((3 * LANES, 2 * D_ATT), np.float32)
    ones = np.zeros((1, 2 * D_ATT), np.float32)
    for p in range(D_ATT // LANES):
        ones[0, p * LANES + 6:p * LANES + 12] = 1.0
        for hh in range(2):
            for s in range(3):
                sel[s * LANES + 2 * p + hh, p * LANES + 3 * hh + s] = 1.0
                sel[s * LANES + 2 * p + hh, D_ATT + p * LANES + 6 + 3 * hh + s] = 1.0
    return jnp.asarray(sel, BF16), jnp.asarray(ones, F32)


def _in_projection(x, g, w_main, w_vt, w_small_t, b_forget, dt_bias, dt_bias_x, b_forget_x, bt):
    nb, l, _ = x.shape
    tril = jnp.tril(jnp.ones((bt, bt), F32)).astype(BF16)
    sel, ones = _attention_extras()
    tok = lambda w: pl.BlockSpec((1, bt, w), lambda b, j: (b, j, 0))
    tok_t = lambda r: pl.BlockSpec((1, r, bt), lambda b, j: (b, 0, j))
    const = lambda shape: pl.BlockSpec(shape, lambda b, j: (0,) * len(shape))
    out_shape = (
        jax.ShapeDtypeStruct((nb, l, 2 * D_ATT), BF16),
        jax.ShapeDtypeStruct((nb, l, 2 * D_ATT), BF16),
        jax.ShapeDtypeStruct((nb, l, D_ATT), F32),
        jax.ShapeDtypeStruct((nb, D_ATT, l), BF16),
        jax.ShapeDtypeStruct((nb, D_ATT, l), F32),
        jax.ShapeDtypeStruct((nb, l, D_SSM), F32),
        jax.ShapeDtypeStruct((nb, l, CONV_CH), F32),
        jax.ShapeDtypeStruct((nb, l, D_SSM), F32),
        jax.ShapeDtypeStruct((nb, H_ATT, l), F32),
        jax.ShapeDtypeStruct((nb, H_SSM, l), F32),
    )
    out_specs = (tok(2 * D_ATT), tok(2 * D_ATT), tok(D_ATT), tok_t(D_ATT), tok_t(D_ATT), tok(D_SSM),
                 tok(CONV_CH), tok(D_SSM), tok_t(H_ATT), tok_t(H_SSM))
    return pl.pallas_call(
        _inproj_kernel,
        out_shape=out_shape,
        grid=(nb, l // bt),
        in_specs=[tok(D_MODEL), const((1, D_MODEL)), const((D_MODEL, _MAIN_COLS)),
                  const((D_ATT, D_MODEL)), const((2 * H_ATT, D_MODEL)), const((H_ATT, 1)),
                  const((H_SSM, 1)), const((1, D_SSM)), const((1, LANES)), const((bt, bt)),
                  const((3 * LANES, 2 * D_ATT)), const((1, 2 * D_ATT))],
        out_specs=out_specs,
        scratch_shapes=[pltpu.VMEM((8, LANES), F32)],
        compiler_params=_cparams(2),
        name="in_projection",
    )(x, g, w_main, w_vt, w_small_t, b_forget, dt_bias, dt_bias_x, b_forget_x, tril, sel, ones)


def _fox_prompt_kernel(qx_ref, kx_ref, vt_ref, o_ref, *, blk, n_split):
    qi = pl.program_id(2)
    q = qx_ref[0, :, 0:LANES]
    cq = jnp.broadcast_to(qx_ref[0, 0:1, LANES:2 * LANES], (blk, LANES))
    lane = lax.broadcasted_iota(jnp.int32, (blk, LANES), 1)
    first = lane < HD_ATT
    zero = jnp.zeros_like(q)
    minus = jnp.full_like(q, -1.0)
    q_heads = []
    for hh in range(2):
        ext = jnp.where((lane >= 3 * hh) & (lane < 3 * hh + 3), minus,
                        jnp.where((lane >= 6 + 3 * hh) & (lane < 9 + 3 * hh), cq, zero))
        own = jnp.where(first, q, zero) if hh == 0 else jnp.where(first, zero, q)
        q_heads.append(jnp.concatenate([own, ext], axis=1))

    qw = blk // n_split
    chains = [(hh, c) for hh in range(2) for c in range(n_split)]
    q_parts = [q_heads[hh][c * qw:(c + 1) * qw, :] for hh, c in chains]

    def attend(j, carry, diagonal):
        k0 = pl.multiple_of(j * blk, blk)
        kj = kx_ref[0, pl.ds(k0, blk), :]
        out = []
        st_next = _dot_nt(kj, q_parts[0])
        for i, ((hh, c), (m, acc)) in enumerate(zip(chains, carry)):
            st = st_next
            if i + 1 < len(chains):
                st_next = _dot_nt(kj, q_parts[i + 1])
            if diagonal:
                key = lax.broadcasted_iota(jnp.int32, (blk, qw), 0)
                qry = lax.broadcasted_iota(jnp.int32, (blk, qw), 1) + c * qw
                st = jnp.where(key <= qry, st, NEG_BIG)
            m_new = jnp.maximum(m, jnp.max(st, axis=0, keepdims=True))
            p = jnp.exp2((st - m_new).astype(BF16))
            alpha = jnp.exp2(m - m_new)
            vt = vt_ref[0, hh * HD_ATT:(hh + 1) * HD_ATT, pl.ds(k0, blk)]
            vt1 = jnp.concatenate([vt, jnp.ones((ONES_ROWS, blk), BF16)], axis=0)
            acc = alpha * acc + _dot(vt1, p)
            out.append((m_new, acc))
        return tuple(out)

    init = tuple((jnp.full((1, qw), NEG_BIG, F32), jnp.zeros((HD_ATT + ONES_ROWS, qw), F32))
                 for _ in chains)
    carry = lax.fori_loop(0, qi, lambda j, c: attend(j, c, False), init)
    done = attend(qi, carry, True)
    o_heads = [jnp.concatenate([done[hh * n_split + c][1][0:HD_ATT] / done[hh * n_split + c][1][HD_ATT:HD_ATT + 1]
                                for c in range(n_split)], axis=1) for hh in range(2)]
    o_ref[0] = jnp.concatenate(o_heads, axis=0).T.astype(o_ref.dtype)


def _fox_prompt(qx, kx, vtb, blk):
    nb, l, _ = qx.shape
    n_pairs = D_ATT // LANES
    return pl.pallas_call(
        functools.partial(_fox_prompt_kernel, blk=blk, n_split=2 if blk % 512 == 0 else 1),
        out_shape=jax.ShapeDtypeStruct((nb, l, D_ATT), BF16),
        grid=(nb, n_pairs, l // blk),
        in_specs=[pl.BlockSpec((1, blk, 2 * LANES), lambda b, p, i: (b, i, p)),
                  pl.BlockSpec((1, l, 2 * LANES), lambda b, p, i: (b, 0, p)),
                  pl.BlockSpec((1, LANES, l), lambda b, p, i: (b, p, 0))],
        out_specs=pl.BlockSpec((1, blk, LANES), lambda b, p, i: (b, i, p)),
        compiler_params=_cparams(3),
        name="fox_prompt",
    )(qx, kx, vtb)


def _ssd_gate_norm(y, z, g):
    return _rms(y * _silu(z), g)


def _ssd_prompt_kernel(xbc_ref, z_ref, dtx_ref, dtt_ref, cw_ref, cb_ref, alx_ref, alc_ref,
                       dsx_ref, g_ref, tril_ref, triu_ref,
                       y_ref, st_ref, xp_ref, state_ref):
    q = SSD_CHUNK
    c = pl.program_id(1)

    @pl.when(c == 0)
    def _():
        xp_ref[0:8, :] = jnp.zeros((8, CONV_CH), F32)
        state_ref[...] = jnp.zeros_like(state_ref)

    xp_ref[8:8 + q, :] = xbc_ref[0]
    acc = cb_ref[...]
    for tap in range(CONV_WIDTH):
        off = 8 - (CONV_WIDTH - 1) + tap
        acc = acc + xp_ref[off:off + q, :] * cw_ref[tap:tap + 1, :]
    xp_ref[0:8, :] = xp_ref[q:q + 8, :]
    act = _silu(acc)
    xs = act[:, :D_SSM]
    bmat = act[:, D_SSM:D_SSM + N_BC_GROUPS * N_STATE]
    cmat = act[:, D_SSM + N_BC_GROUPS * N_STATE:]

    dt = dtx_ref[0]
    a_x = -jnp.exp(alx_ref[...]) * dt
    hi, mid, lo = _split3(a_x)
    parts = jnp.concatenate([hi, mid, lo], axis=1).astype(BF16)
    r = _dot(tril_ref[...], parts)
    acs_x = r[:, 0:D_SSM] + r[:, D_SSM:2 * D_SSM] + r[:, 2 * D_SSM:3 * D_SSM]
    a_t = -jnp.exp(alc_ref[...]) * dtt_ref[0]
    acs_t = _cumsum_lanes(a_t, triu_ref[...])

    xdt = xs * dt
    a_last = acs_x[q - 1:q, :]
    xdt_end = (xdt * jnp.exp(a_last - acs_x)).astype(BF16)
    e_acs = jnp.exp(acs_x)
    xdt_b = xdt.astype(BF16)
    state_old = state_ref[...]
    state_b = state_old.astype(BF16)

    row = lax.broadcasted_iota(jnp.int32, (q, q), 0)
    col = lax.broadcasted_iota(jnp.int32, (q, q), 1)
    causal = row >= col
    first = lax.broadcasted_iota(jnp.int32, (q, LANES), 1) < P_SSM

    y_parts = []
    for g in range(N_BC_GROUPS):
        cg = cmat[:, g * N_STATE:(g + 1) * N_STATE].astype(BF16)
        bg = bmat[:, g * N_STATE:(g + 1) * N_STATE].astype(BF16)
        cb = _dot_nt(cg, bg)
        gl = slice(g * GROUP_W, (g + 1) * GROUP_W)
        y_off = _dot(cg, state_b[:, gl]) * e_acs[:, gl]
        for pr in range(GROUP_W // LANES):
            lo_lane = g * GROUP_W + pr * LANES
            pair = []
            for hh in range(2):
                h = lo_lane // P_SSM + hh
                seg = acs_x[:, h * P_SSM:h * P_SSM + 1] - acs_t[h:h + 1, :]
                decay = jnp.exp(jnp.where(causal, seg, -jnp.inf))
                gm = (cb * decay).astype(BF16)
                pair.append(_dot(gm, xdt_b[:, lo_lane:lo_lane + LANES]))
            y_parts.append(jnp.where(first, pair[0], pair[1]) + y_off[:, pr * LANES:(pr + 1) * LANES])
        state_ref[:, gl] = jnp.exp(a_last[:, gl]) * state_old[:, gl] + _dot_tn(bg, xdt_end[:, gl])
    y = jnp.concatenate(y_parts, axis=1) + dsx_ref[...] * xs
    y_ref[0] = _ssd_gate_norm(y, z_ref[0], g_ref[...]).astype(y_ref.dtype)

    @pl.when(c == pl.num_programs(1) - 1)
    def _():
        st_ref[0] = state_ref[...].T


def _ssd_prompt(xbc, z, dtx, dtt, conv_w, conv_b, a_log_x, a_log_c, d_skip_x, norm_g):
    nb, l, _ = xbc.shape
    q = SSD_CHUNK
    tril = jnp.tril(jnp.ones((q, q), F32)).astype(BF16)
    triu = jnp.triu(jnp.ones((q, q), F32)).astype(BF16)
    tok = lambda w: pl.BlockSpec((1, q, w), lambda b, c: (b, c, 0))
    const = lambda shape: pl.BlockSpec(shape, lambda b, c: (0,) * len(shape))
    return pl.pallas_call(
        _ssd_prompt_kernel,
        out_shape=(jax.ShapeDtypeStruct((nb, l, D_SSM), BF16),
                   jax.ShapeDtypeStruct((nb, D_SSM, N_STATE), F32)),
        grid=(nb, l // q),
        in_specs=[tok(CONV_CH), tok(D_SSM), tok(D_SSM),
                  pl.BlockSpec((1, H_SSM, q), lambda b, c: (b, 0, c)),
                  const((CONV_WIDTH, CONV_CH)), const((1, CONV_CH)), const((1, D_SSM)),
                  const((H_SSM, 1)), const((1, D_SSM)), const((1, D_SSM)),
                  const((q, q)), const((q, q))],
        out_specs=(tok(D_SSM), pl.BlockSpec((1, D_SSM, N_STATE), lambda b, c: (b, 0, 0))),
        scratch_shapes=[pltpu.VMEM((q + 8, CONV_CH), F32), pltpu.VMEM((N_STATE, D_SSM), F32)],
        compiler_params=_cparams(2),
        name="ssd_prompt",
    )(xbc, z, dtx, dtt, conv_w, conv_b, a_log_x, a_log_c, d_skip_x, norm_g, tril, triu)


def _mix_out_kernel(x_ref, att_ref, ssm_ref, wo_ref, g_ref, wq_ref, x1_ref, qc_ref):
    x1 = x_ref[...] + _dot(att_ref[...], wo_ref[0:D_ATT, :]) + _dot(ssm_ref[...], wo_ref[D_ATT:, :])
    x1_ref[...] = x1
    h = _rms(x1, g_ref[...]).astype(BF16)
    qc_ref[...] = (_dot(h, wq_ref[...]) * X_SCALE).astype(BF16)


def _mix_out(x, att, ssm, w_out, g_cross, w_cq, bt):
    t = x.shape[0]
    tok = lambda w: pl.BlockSpec((bt, w), lambda i: (i, 0))
    const = lambda shape: pl.BlockSpec(shape, lambda i: (0,) * len(shape))
    return pl.pallas_call(
        _mix_out_kernel,
        out_shape=(jax.ShapeDtypeStruct((t, D_MODEL), F32), jax.ShapeDtypeStruct((t, D_MODEL), BF16)),
        grid=(t // bt,),
        in_specs=[tok(D_MODEL), tok(D_ATT), tok(D_SSM), const((D_MODEL, D_MODEL)),
                  const((1, D_MODEL)), const((D_MODEL, D_MODEL))],
        out_specs=(tok(D_MODEL), tok(D_MODEL)),
        compiler_params=_cparams(1),
        name="mix_out",
    )(x, att, ssm, w_out, g_cross, w_cq)


def _ffn_kernel(x1_ref, o_ref, wco_ref, g_ref, wg_ref, wu_ref, wd_ref, gf_ref, y_ref):
    x2 = x1_ref[...] + _dot(o_ref[...], wco_ref[...])
    h = _rms(x2, g_ref[...]).astype(BF16)
    u = (_silu(_dot(h, wg_ref[...])) * _dot(h, wu_ref[...])).astype(BF16)
    x3 = x2 + _dot(u, wd_ref[...])
    y_ref[...] = _rms(x3, gf_ref[...])


def _ffn(x1, o, w_co, g_ffn, w_gate, w_up, w_down, g_final, bt):
    t = x1.shape[0]
    d_ff = w_gate.shape[1]
    tok = lambda w: pl.BlockSpec((bt, w), lambda i: (i, 0))
    const = lambda shape: pl.BlockSpec(shape, lambda i: (0,) * len(shape),
                                       pipeline_mode=pl.Buffered(1))
    return pl.pallas_call(
        _ffn_kernel,
        out_shape=jax.ShapeDtypeStruct((t, D_MODEL), F32),
        grid=(t // bt,),
        in_specs=[tok(D_MODEL), tok(D_MODEL), const((D_MODEL, D_MODEL)), const((1, D_MODEL)),
                  const((D_MODEL, d_ff)), const((D_MODEL, d_ff)), const((d_ff, D_MODEL)),
                  const((1, D_MODEL))],
        out_specs=tok(D_MODEL),
        compiler_params=_cparams(1),
        name="cross_out_ffn",
    )(x1, o, w_co, g_ffn, w_gate, w_up, w_down, g_final)


def _memory_kv_kernel(mem_ref, g_ref, wk_ref, wv_ref, k32_ref, v32_ref, kb_ref, vb_ref):
    mn = _rms(mem_ref[0], g_ref[...]).astype(BF16)
    k = _dot(mn, wk_ref[...])
    v = _dot(mn, wv_ref[...])
    k32_ref[0] = k
    v32_ref[0] = v
    kb_ref[0] = k.astype(BF16)
    vb_ref[0] = v.astype(BF16)


def _memory_kv(mem, g, w_ck, w_cv):
    nb, m, _ = mem.shape
    blk = pl.BlockSpec((1, m, D_MODEL), lambda b: (b, 0, 0))
    const = lambda shape: pl.BlockSpec(shape, lambda b: (0,) * len(shape))
    f = jax.ShapeDtypeStruct((nb, m, D_MODEL), F32)
    h = jax.ShapeDtypeStruct((nb, m, D_MODEL), BF16)
    return pl.pallas_call(
        _memory_kv_kernel,
        out_shape=(f, f, h, h),
        grid=(nb,),
        in_specs=[blk, const((1, D_MODEL)), const((D_MODEL, D_MODEL)), const((D_MODEL, D_MODEL))],
        out_specs=(blk, blk, blk, blk),
        compiler_params=_cparams(1),
        name="memory_kv",
    )(mem, g, w_ck, w_cv)


def _softmax_rows(s):
    m = jnp.max(s, axis=-1, keepdims=True)
    p = jnp.exp(s - m)
    return p / jnp.sum(p, axis=-1, keepdims=True)


def _cross_prompt_kernel(q_ref, k_ref, v_ref, o_ref):
    for h in range(H_X):
        hl = slice(h * HD_X, (h + 1) * HD_X)
        p = _softmax_rows(_dot_nt(q_ref[0, :, hl], k_ref[0, :, hl]))
        o_ref[0, :, hl] = _dot(p.astype(BF16), v_ref[0, :, hl]).astype(o_ref.dtype)


def _cross_prompt(qc, mk, mv, bt):
    nb, l, _ = qc.shape
    m = mk.shape[1]
    tok = pl.BlockSpec((1, bt, D_MODEL), lambda b, i: (b, i, 0))
    mem = pl.BlockSpec((1, m, D_MODEL), lambda b, i: (b, 0, 0))
    return pl.pallas_call(
        _cross_prompt_kernel,
        out_shape=jax.ShapeDtypeStruct((nb, l, D_MODEL), BF16),
        grid=(nb, l // bt),
        in_specs=[tok, mem, mem],
        out_specs=tok,
        compiler_params=_cparams(2),
        name="cross_prompt",
    )(qc, mk, mv)


_X_SUB = 2 * H_X


def _cross_sample_kernel(q_ref, k_ref, v_ref, o_ref, *, lq):
    nsb, n_mem = k_ref.shape[0], k_ref.shape[1]
    n = n_mem * _X_SUB
    rows = lq * _X_SUB
    lane = lax.broadcasted_iota(jnp.int32, (rows, n), 1)
    sub = lax.broadcasted_iota(jnp.int32, (rows, n), 0)
    own = (lane % _X_SUB) == (sub % _X_SUB)
    cls = lane % _X_SUB
    for i in range(nsb):
        k5 = k_ref[i].reshape(n, LANES).astype(BF16)
        s5 = jnp.where(own, _dot_nt(q_ref[i], k5), 0.0)
        u = jnp.concatenate(
            [jnp.broadcast_to(jnp.sum(s5[t * _X_SUB:(t + 1) * _X_SUB], axis=0, keepdims=True), (_X_SUB, n))
             for t in range(lq)], axis=0)
        s = u + pltpu.roll(u, n - H_X, 1)
        m_full = jnp.zeros((rows, n), F32)
        for h in range(H_X):
            m_h = jnp.max(jnp.where(cls == h, s, NEG_BIG), axis=1, keepdims=True)
            m_full = jnp.where(cls == h, m_h, m_full)
        p = jnp.where(cls < H_X, jnp.exp(s - m_full), 0.0)
        l_full = jnp.ones((rows, n), F32)
        for h in range(H_X):
            l_h = jnp.sum(jnp.where(cls == h, p, 0.0), axis=1, keepdims=True)
            l_full = jnp.where(cls == h, l_h, l_full)
        p = p / l_full
        p5 = jnp.where(own, p + pltpu.roll(p, H_X, 1), 0.0).astype(BF16)
        v5 = v_ref[i].reshape(n, LANES).astype(BF16)
        o_ref[i] = _dot(p5, v5).astype(o_ref.dtype)


def _cross_sample(q5, mem_k, mem_v, lq, n_seq_blk):
    ns, rows, _ = q5.shape
    m = mem_k.shape[1]
    tok = pl.BlockSpec((n_seq_blk, rows, LANES), lambda i: (i, 0, 0))
    mem = pl.BlockSpec((n_seq_blk, m, _X_SUB, LANES), lambda i: (i, 0, 0, 0))
    return pl.pallas_call(
        functools.partial(_cross_sample_kernel, lq=lq),
        out_shape=jax.ShapeDtypeStruct((ns, rows, LANES), BF16),
        grid=(ns // n_seq_blk,),
        in_specs=[tok, mem, mem],
        out_specs=tok,
        compiler_params=_cparams(1),
        name="cross_sample",
    )(q5, mem_k, mem_v)


def _fox_sample_kernel(pt_ref, q_ref, kn_ref, vn_ref, lfn_ref, tril_ref, *rest, n_pages, lq):
    del pt_ref
    k_refs = rest[0:n_pages]
    v_refs = rest[n_pages:2 * n_pages]
    lf_refs = rest[2 * n_pages:3 * n_pages]
    o_ref = rest[3 * n_pages]
    kt_ref = rest[3 * n_pages + 1]
    vt_ref = rest[3 * n_pages + 2]

    for j in range(n_pages):
        kt_ref[:, :, j * PAGE_SIZE:(j + 1) * PAGE_SIZE] = k_refs[j][0].astype(BF16)
        vt_ref[:, :, j * PAGE_SIZE:(j + 1) * PAGE_SIZE] = v_refs[j][0].astype(BF16)

    lf_all = jnp.concatenate([r[0] for r in lf_refs], axis=0) * LOG2E
    hi, mid, lo = _split3(lf_all)
    tril = tril_ref[...]
    incl = _dot(hi.astype(BF16), tril) + _dot(mid.astype(BF16), tril) + _dot(lo.astype(BF16), tril)
    after = jnp.zeros((H_ATT, 1), F32)
    pieces = [None] * n_pages
    for j in reversed(range(n_pages)):
        sl = slice(j * H_ATT, (j + 1) * H_ATT)
        pieces[j] = incl[sl] - lf_all[sl] + after
        after = after + incl[sl][:, 0:1]
    bias = jnp.concatenate(pieces, axis=1)

    lfn = lfn_ref[0] * LOG2E
    lane = lax.broadcasted_iota(jnp.int32, (H_ATT, lq), 1)
    cn = jnp.zeros((H_ATT, lq), F32)
    for t in range(lq):
        cn = cn + jnp.where(lane >= t, lfn[:, t:t + 1], 0.0)
    causal = (lax.broadcasted_iota(jnp.int32, (lq, lq), 1)
              <= lax.broadcasted_iota(jnp.int32, (lq, lq), 0))

    for h in range(H_ATT):
        qh = q_ref[0, h]
        s = _dot(qh, kt_ref[h]) + bias[h:h + 1, :]
        s_new = jnp.where(causal, _dot_nt(qh, kn_ref[0, h]) - cn[h:h + 1, :], NEG_BIG)
        m = jnp.maximum(jnp.max(s, axis=-1, keepdims=True), jnp.max(s_new, axis=-1, keepdims=True))
        p = jnp.exp2(s - m)
        p_new = jnp.exp2(s_new - m)
        l = jnp.sum(p, axis=-1, keepdims=True) + jnp.sum(p_new, axis=-1, keepdims=True)
        o = _dot_nt(p.astype(BF16), vt_ref[h]) + _dot(p_new.astype(BF16), vn_ref[0, h])
        o_ref[0, h] = (o / l).astype(o_ref.dtype)


def _fox_sample(qh, kh, vh, lfn, cache_kt, cache_vt, cache_lft, page_table):
    ns, _, lq, _ = qh.shape
    n_pages = page_table.shape[1]
    past = n_pages * PAGE_SIZE
    tril = jnp.tril(jnp.ones((PAGE_SIZE, PAGE_SIZE), F32)).astype(BF16)
    tok = pl.BlockSpec((1, H_ATT, lq, HD_ATT), lambda i, pt: (i, 0, 0, 0))

    def page(j, shape):
        return pl.BlockSpec((1,) + shape, lambda i, pt: (pt[i * n_pages + j],) + (0,) * len(shape))

    in_specs = [tok, tok, tok, pl.BlockSpec((1, H_ATT, lq), lambda i, pt: (i, 0, 0)),
                pl.BlockSpec((PAGE_SIZE, PAGE_SIZE), lambda i, pt: (0, 0))]
    in_specs += [page(j, (H_ATT, HD_ATT, PAGE_SIZE)) for j in range(n_pages)]
    in_specs += [page(j, (H_ATT, HD_ATT, PAGE_SIZE)) for j in range(n_pages)]
    in_specs += [page(j, (H_ATT, PAGE_SIZE)) for j in range(n_pages)]
    grid_spec = pltpu.PrefetchScalarGridSpec(
        num_scalar_prefetch=1,
        grid=(ns,),
        in_specs=in_specs,
        out_specs=tok,
        scratch_shapes=[pltpu.VMEM((H_ATT, HD_ATT, past), BF16), pltpu.VMEM((H_ATT, HD_ATT, past), BF16)],
    )
    return pl.pallas_call(
        functools.partial(_fox_sample_kernel, n_pages=n_pages, lq=lq),
        out_shape=jax.ShapeDtypeStruct((ns, H_ATT, lq, HD_ATT), BF16),
        grid_spec=grid_spec,
        compiler_params=_cparams(1),
        name="fox_sample",
    )(page_table.reshape(-1), qh, kh, vh, lfn, tril,
      *([cache_kt] * n_pages), *([cache_vt] * n_pages), *([cache_lft] * n_pages))


def _ssd_sample_kernel(xbc_ref, z_ref, dtx_ref, cs_ref, st_ref, cw_ref, cb_ref, alx_ref, dsx_ref,
                       g_ref, y_ref, sto_ref, xp_ref, rows_ref, brow_ref, *, lq):
    nsb = xbc_ref.shape[0]
    assert nsb * (lq + 1) <= LANES
    kw = CONV_WIDTH - 1
    xp_ref[:, 0:kw, :] = cs_ref[...]
    xp_ref[:, kw:kw + lq, :] = xbc_ref[...]
    acc = cb_ref[...]
    for tap in range(CONV_WIDTH):
        acc = acc + xp_ref[:, tap:tap + lq, :] * cw_ref[tap:tap + 1, :]
    act = _silu(acc)
    xs = act[:, :, :D_SSM]
    bmat = act[:, :, D_SSM:D_SSM + N_BC_GROUPS * N_STATE]
    cmat = act[:, :, D_SSM + N_BC_GROUPS * N_STATE:]
    dt = dtx_ref[...]
    a = -jnp.exp(alx_ref[...]) * dt
    xdt = xs * dt
    acs = [a[:, 0:1, :]]
    for t in range(1, lq):
        acs.append(acs[-1] + a[:, t:t + 1, :])
    a_last = acs[lq - 1]

    ccat = jnp.concatenate([cmat[:, :, g * N_STATE:(g + 1) * N_STATE] for g in range(N_BC_GROUPS)],
                           axis=1).astype(BF16)
    state = st_ref[...]
    r = jnp.einsum("sgn,shn->sgh", ccat, state.astype(BF16), preferred_element_type=F32)
    group0 = lax.broadcasted_iota(jnp.int32, (1, 1, D_SSM), 2) < GROUP_W
    y_off = jnp.where(group0, r[:, 0:lq, :], r[:, lq:2 * lq, :])

    ys = []
    for t in range(lq):
        y_t = jnp.exp(acs[t]) * y_off[:, t:t + 1, :] + dsx_ref[...] * xs[:, t:t + 1, :]
        for s in range(t + 1):
            cb = jnp.concatenate(
                [jnp.broadcast_to(
                    jnp.sum(cmat[:, t:t + 1, g * N_STATE:(g + 1) * N_STATE]
                            * bmat[:, s:s + 1, g * N_STATE:(g + 1) * N_STATE], axis=-1, keepdims=True),
                    (nsb, 1, GROUP_W)) for g in range(N_BC_GROUPS)], axis=2)
            y_t = y_t + cb * jnp.exp(acs[t] - acs[s]) * xdt[:, s:s + 1, :]
        ys.append(y_t)
    y = jnp.concatenate(ys, axis=1)
    y_ref[...] = _ssd_gate_norm(y, z_ref[...], g_ref[...]).astype(y_ref.dtype)

    upd = jnp.concatenate([xdt[:, s:s + 1, :] * jnp.exp(a_last - acs[s]) for s in range(lq)], axis=1)
    e_last = jnp.exp(a_last)
    rows_ref[...] = jnp.zeros_like(rows_ref)
    brow_ref[...] = jnp.zeros_like(brow_ref)
    for i in range(nsb):
        rows_ref[i * lq:(i + 1) * lq, :] = upd[i]
        rows_ref[nsb * lq + i:nsb * lq + i + 1, :] = e_last[i]
        brow_ref[i * lq:(i + 1) * lq, :] = bmat[i]
    cols = rows_ref[...].T
    upd_cols = cols.astype(BF16)
    b_all = brow_ref[...]
    row_seq = lax.broadcasted_iota(jnp.int32, (LANES, N_BC_GROUPS * N_STATE), 0) // lq
    top = lax.broadcasted_iota(jnp.int32, (D_SSM, N_STATE), 0) < GROUP_W
    for i in range(nsb):
        b_rows = jnp.where(row_seq == i, b_all, 0.0).astype(BF16)
        m = _dot(upd_cols, b_rows)
        add = jnp.where(top, m[:, 0:N_STATE], m[:, N_STATE:2 * N_STATE])
        decay = jnp.broadcast_to(cols[:, nsb * lq + i:nsb * lq + i + 1], (D_SSM, N_STATE))
        sto_ref[i] = decay * state[i] + add


def _ssd_sample(xbc, z, dtx, conv_state, state, conv_w, conv_b, a_log_x, d_skip_x, norm_g, nsb):
    ns, lq, _ = xbc.shape
    seq = lambda r, w: pl.BlockSpec((nsb, r, w), lambda i: (i, 0, 0))
    const = lambda shape: pl.BlockSpec(shape, lambda i: (0,) * len(shape))
    return pl.pallas_call(
        functools.partial(_ssd_sample_kernel, lq=lq),
        out_shape=(jax.ShapeDtypeStruct((ns, lq, D_SSM), BF16),
                   jax.ShapeDtypeStruct((ns, D_SSM, N_STATE), F32)),
        grid=(ns // nsb,),
        in_specs=[seq(lq, CONV_CH), seq(lq, D_SSM), seq(lq, D_SSM), seq(CONV_WIDTH - 1, CONV_CH),
                  seq(D_SSM, N_STATE), const((CONV_WIDTH, CONV_CH)), const((1, CONV_CH)),
                  const((1, D_SSM)), const((1, D_SSM)), const((1, D_SSM))],
        out_specs=(seq(lq, D_SSM), seq(D_SSM, N_STATE)),
        scratch_shapes=[pltpu.VMEM((nsb, 8, CONV_CH), F32), pltpu.VMEM((LANES, D_SSM), F32),
                        pltpu.VMEM((LANES, N_BC_GROUPS * N_STATE), F32)],
        compiler_params=_cparams(1),
        name="ssd_sample",
    )(xbc, z, dtx, conv_state, state, conv_w, conv_b, a_log_x, d_skip_x, norm_g)


def _pick(pref, n):
    return pref if n % pref == 0 else n


def kernel(x_prompt, x_sample, mem_prompt, cache_k, cache_v, cache_logf, page_table, cache_mem_k, cache_mem_v, state_conv, state_ssm, norm_mix_g, w_in, b_forget, conv_w, conv_b, dt_bias, a_log, d_skip, ssm_norm_g, w_out, norm_cross_g, norm_mem_g, w_cq, w_ck, w_cv, w_co, norm_ffn_g, w_gate, w_up, w_down, final_norm_g):
    assert w_in.shape[0] == 1, "one layer"
    nb, l, _ = x_prompt.shape
    ns, lq, _ = x_sample.shape
    n_phys = cache_k.shape[1]
    row = lambda v: v.reshape(1, -1).astype(F32)
    colv = lambda v: v.reshape(-1, 1).astype(F32)
    per_ch = lambda v: jnp.repeat(v.astype(F32), P_SSM).reshape(1, D_SSM)

    w = w_in[0]
    cuts = [D_ATT, 2 * D_ATT, 3 * D_ATT, 3 * D_ATT + H_ATT, 3 * D_ATT + H_ATT + D_SSM,
            3 * D_ATT + H_ATT + D_SSM + CONV_CH]
    w_q, w_k, w_v, w_f, w_z, w_xbc, w_dt = jnp.split(w, cuts, axis=1)
    lane_pad = lambda a: jnp.pad(a, ((0, 0), (0, LANES - a.shape[1])))
    w_main = jnp.concatenate([w_q, w_k, w_z, w_xbc, jnp.repeat(w_dt, P_SSM, axis=1), lane_pad(w_f)],
                             axis=1).astype(BF16)
    w_small_t = jnp.concatenate([w_f, w_dt], axis=1).T.astype(BF16)
    in_params = (row(norm_mix_g[0]), w_main, w_v.T.astype(BF16), w_small_t, colv(b_forget[0]),
                 colv(dt_bias[0]), per_ch(dt_bias[0]), lane_pad(row(b_forget[0])))
    ssd_params = (conv_w[0], row(conv_b[0]), per_ch(a_log[0]))
    ssd_tail = (per_ch(d_skip[0]), row(ssm_norm_g[0]))
    w_out_b, w_cq_b, w_co_b = w_out[0].astype(BF16), w_cq[0].astype(BF16), w_co[0].astype(BF16)
    w_gate_b, w_up_b, w_down_b = w_gate[0].astype(BF16), w_up[0].astype(BF16), w_down[0].astype(BF16)
    g_cross, g_ffn, g_final = row(norm_cross_g[0]), row(norm_ffn_g[0]), row(final_norm_g)

    def tail(x, att, ssm, cross, bt):
        x1, qc = _mix_out(x, att, ssm, w_out_b, g_cross, w_cq_b, bt)
        o = cross(qc)
        return _ffn(x1, o, w_co_b, g_ffn, w_gate_b, w_up_b, w_down_b, g_final, bt)

    bt = _pick(512, l)
    qx, kx, k32, vtb, v32t, z, xbc, dtx, logft, dtt = _in_projection(x_prompt, *in_params, bt)
    att = _fox_prompt(qx, kx, vtb, _pick(512, l))
    ssm, st_p = _ssd_prompt(xbc, z, dtx, dtt, *ssd_params, colv(a_log[0]), *ssd_tail)
    mk32, mv32, mkb, mvb = _memory_kv(mem_prompt, row(norm_mem_g[0]), w_ck[0].astype(BF16),
                                      w_cv[0].astype(BF16))
    cross_p = lambda qc: _cross_prompt(qc.reshape(nb, l, D_MODEL), mkb, mvb, bt).reshape(nb * l, D_MODEL)
    y_prompt = tail(x_prompt.reshape(nb * l, D_MODEL), att.reshape(nb * l, D_ATT),
                    ssm.reshape(nb * l, D_SSM), cross_p, bt).reshape(nb, l, D_MODEL)

    ts = ns * lq
    bts = _pick(512, ts)
    qx_s, _, k32_s, vtb_s, v32t_s, z_s, xbc_s, dtx_s, logft_s, _ = _in_projection(
        x_sample.reshape(1, ts, D_MODEL), *in_params, bts)
    seq3 = lambda a: a.reshape(ns, lq, a.shape[-1])
    lfn = logft_s.reshape(H_ATT, ns, lq).transpose(1, 0, 2)
    head_major = lambda a: a.reshape(ns, lq, H_ATT, HD_ATT).transpose(0, 2, 1, 3)
    from_t = lambda a: a.reshape(H_ATT, HD_ATT, ns, lq)
    qb_s = qx_s.reshape(ts, D_ATT // LANES, 2, LANES)[:, :, 0, :]
    att_s = _fox_sample(head_major(qb_s), head_major(k32_s.astype(BF16)),
                        from_t(vtb_s).transpose(2, 0, 3, 1), lfn,
                        cache_k[0].transpose(0, 2, 3, 1), cache_v[0].transpose(0, 2, 3, 1),
                        cache_logf[0].transpose(0, 2, 1), page_table)
    att_s = att_s.transpose(0, 2, 1, 3)
    ssm_s, st_s = _ssd_sample(seq3(xbc_s), seq3(z_s), seq3(dtx_s), state_conv[0],
                              state_ssm[0].reshape(ns, D_SSM, N_STATE), *ssd_params, *ssd_tail,
                              _pick(8, ns))
    n_mem = mem_prompt.shape[1]
    stored = lambda a: a.reshape(ns, -1, H_X, 2, LANES).transpose(0, 1, 3, 2, 4).reshape(ns, -1, _X_SUB, LANES)
    cross_s = lambda qc: _cross_sample(
        stored(qc).reshape(ns, lq * _X_SUB, LANES), stored(cache_mem_k[0]), stored(cache_mem_v[0]), lq,
        _pick(4, ns)).reshape(ns, lq, 2, H_X, LANES).transpose(0, 1, 3, 2, 4).reshape(ts, D_MODEL)
    y_sample = tail(x_sample.reshape(ts, D_MODEL), att_s.reshape(ts, D_ATT), ssm_s.reshape(ts, D_SSM),
                    cross_s, bts).reshape(ns, lq, D_MODEL)

    heads = lambda a, n: a.reshape(1, n, -1, H_ATT, HD_ATT)
    kw = CONV_WIDTH - 1
    return (y_prompt, y_sample,
            heads(k32, nb), v32t.reshape(nb, H_ATT, HD_ATT, l).transpose(0, 3, 1, 2)[None],
            logft.transpose(0, 2, 1)[None],
            mk32.reshape(1, nb, n_mem, H_X, HD_X), mv32.reshape(1, nb, n_mem, H_X, HD_X),
            xbc[:, l - kw:, :][None], st_p.reshape(1, nb, H_SSM, P_SSM, N_STATE),
            heads(k32_s, ns), from_t(v32t_s).transpose(2, 3, 0, 1)[None],
            logft_s.reshape(H_ATT, ns, lq).transpose(1, 2, 0)[None],
            seq3(xbc_s)[:, lq - kw:, :][None], st_s.reshape(1, ns, H_SSM, P_SSM, N_STATE))
```

```python
import functools
import math

import numpy as np
import jax
import jax.numpy as jnp
from jax import lax
from jax.experimental import pallas as pl
from jax.experimental.pallas import tpu as pltpu

F32 = jnp.float32
BF16 = jnp.bfloat16

D_MODEL = 1024
D_ATT = 512
H_ATT = 8
HD_ATT = 64
D_SSM = 512
H_SSM = 8
P_SSM = 64
N_STATE = 128
N_BC_GROUPS = 2
GROUP_W = D_SSM // N_BC_GROUPS
CONV_WIDTH = 4
CONV_CH = D_SSM + 2 * N_BC_GROUPS * N_STATE
H_X = 4
HD_X = 256
EPS = 1e-6
ATT_SCALE = HD_ATT ** -0.5
X_SCALE = HD_X ** -0.5
SSD_CHUNK = 128
PAGE_SIZE = 128
NEG_BIG = -1e30
LOG2E = math.log2(math.e)

BOUND_SCALE = 1.03
BOUND_MARGIN = 2.0
SLACK_LIMIT = 80.0

LANES = 128
ONES_ROWS = 16
VMEM_LIMIT = 56 * 1024 * 1024

_Q0, _K0, _Z0, _XBC0, _DT0, _F0, _MAIN_COLS = 0, 512, 1024, 1536, 2560, 3072, 3200


def _cparams(n_axes):
    return pltpu.CompilerParams(dimension_semantics=("arbitrary",) * n_axes,
                                vmem_limit_bytes=VMEM_LIMIT)


def _rms(x, g):
    ms = jnp.mean(x * x, axis=-1, keepdims=True)
    return x * lax.rsqrt(ms + EPS) * g


def _softplus(x):
    return jnp.maximum(x, 0.0) + jnp.log1p(jnp.exp(-jnp.abs(x)))


def _log_sigmoid(x):
    return jnp.minimum(x, 0.0) - jnp.log1p(jnp.exp(-jnp.abs(x)))


def _silu(x):
    return x * (1.0 / (1.0 + jnp.exp(-x)))


def _split3(x):
    hi = x.astype(BF16).astype(F32)
    r1 = x - hi
    mid = r1.astype(BF16).astype(F32)
    lo = (r1 - mid).astype(BF16).astype(F32)
    return hi, mid, lo


def _dot(a, b):
    return jnp.dot(a, b, preferred_element_type=F32)


def _dot_nt(a, b):
    return lax.dot_general(a, b, (((1,), (1,)), ((), ())), preferred_element_type=F32)


def _dot_tn(a, b):
    return lax.dot_general(a, b, (((0,), (0,)), ((), ())), preferred_element_type=F32)


def _cumsum_lanes(x, tri_upper):
    hi, mid, lo = _split3(x)
    parts = jnp.concatenate([hi, mid, lo], axis=0).astype(BF16)
    r = _dot(parts, tri_upper)
    return r[0:8] + r[8:16] + r[16:24]


def _inproj_kernel(x_ref, g_ref, wm_ref, wvt_ref, wst_ref, bf_ref, dtb_ref, dtbx_ref, bfx_ref,
                   tril_ref, sel_ref, ones_ref, hsel_ref,
                   qx_ref, kx_ref, k32_ref, vtb_ref, v32t_ref, z_ref, xbc_ref, dtx_ref,
                   logft_ref, dtt_ref, c2t_ref, kn_ref, carry_ref, carry_t_ref, carry_kn_ref):
    @pl.when(pl.program_id(1) == 0)
    def _():
        carry_ref[...] = jnp.zeros_like(carry_ref)
        carry_t_ref[...] = jnp.zeros_like(carry_t_ref)
        carry_kn_ref[...] = jnp.zeros_like(carry_kn_ref)

    h = _rms(x_ref[0], g_ref[...]).astype(BF16)
    q = (_dot(h, wm_ref[:, _Q0:_K0]) * (ATT_SCALE * LOG2E)).astype(BF16)
    k = _dot(h, wm_ref[:, _K0:_Z0])
    k32_ref[0] = k
    kb = k.astype(BF16)
    vt = _dot_nt(wvt_ref[...], h)
    v32t_ref[0] = vt
    vtb_ref[0] = vt.astype(BF16)
    z_ref[0] = _dot(h, wm_ref[:, _Z0:_XBC0])
    xbc_ref[0] = _dot(h, wm_ref[:, _XBC0:_DT0])
    dtx_ref[0] = _softplus(_dot(h, wm_ref[:, _DT0:_F0]) + dtbx_ref[...])

    small = _dot_nt(wst_ref[...], h)
    logft = _log_sigmoid(small[0:8] + bf_ref[...])
    logft_ref[0] = logft
    dtt_ref[0] = _softplus(small[8:16] + dtb_ref[...])

    hi, mid, lo = _split3(logft * LOG2E)
    r_t = _dot_nt(jnp.concatenate([hi, mid, lo], axis=0).astype(BF16), tril_ref[...])
    c2t = r_t[0:8] + r_t[8:16] + r_t[16:24] + carry_t_ref[:, 0:1]
    c2t_ref[0] = c2t
    carry_t_ref[...] = jnp.broadcast_to(c2t[:, c2t.shape[1] - 1:], carry_t_ref.shape)
    kf = kb.astype(F32)
    ksq_t = _dot_nt(hsel_ref[...], (kf * kf).astype(BF16))
    kn = jnp.maximum(carry_kn_ref[...], jnp.max(ksq_t, axis=1, keepdims=True))
    carry_kn_ref[...] = kn
    kn_ref[0] = kn

    lf2 = _log_sigmoid(_dot(h, wm_ref[:, _F0:_MAIN_COLS]) + bfx_ref[...]) * LOG2E
    parts = jnp.concatenate(_split3(lf2), axis=1).astype(BF16)
    r = _dot(tril_ref[...], parts)
    c2 = r[:, 0:LANES] + r[:, LANES:2 * LANES] + r[:, 2 * LANES:3 * LANES] + carry_ref[0:1, :]
    bt = c2.shape[0]
    carry_ref[...] = jnp.broadcast_to(c2[bt - 1:bt, :], carry_ref.shape)
    csplit = jnp.concatenate(_split3(c2), axis=1).astype(BF16)
    ext = (_dot(csplit, sel_ref[...]) + ones_ref[...]).astype(BF16)
    for p in range(D_ATT // LANES):
        pl_ = slice(p * LANES, (p + 1) * LANES)
        kx_ref[0, :, 2 * p * LANES:(2 * p + 1) * LANES] = kb[:, pl_]
        kx_ref[0, :, (2 * p + 1) * LANES:(2 * p + 2) * LANES] = ext[:, pl_]
        qx_ref[0, :, 2 * p * LANES:(2 * p + 1) * LANES] = q[:, pl_]
        qx_ref[0, :, (2 * p + 1) * LANES:(2 * p + 2) * LANES] = ext[:, D_ATT + p * LANES:D_ATT + (p + 1) * LANES]


def _attention_extras():
    sel = np.zeros((3 * LANES, 2 * D_ATT), np.float32)
    ones = np.zeros((1, 2 * D_ATT), np.float32)
    for p in range(D_ATT // LANES):
        ones[0, p * LANES + 6:p * LANES + 12] = 1.0
        for hh in range(2):
            for s in range(3):
                sel[s * LANES + 2 * p + hh, p * LANES + 3 * hh + s] = 1.0
                sel[s * LANES + 2 * p + hh, D_ATT + p * LANES + 6 + 3 * hh + s] = 1.0
    return jnp.asarray(sel, BF16), jnp.asarray(ones, F32)


def _in_projection(x, g, w_main, w_vt, w_small_t, b_forget, dt_bias, dt_bias_x, b_forget_x, bt):
    nb, l, _ = x.shape
    tril = jnp.tril(jnp.ones((bt, bt), F32)).astype(BF16)
    sel, ones = _attention_extras()
    tok = lambda w: pl.BlockSpec((1, bt, w), lambda b, j: (b, j, 0))
    tok_t = lambda r: pl.BlockSpec((1, r, bt), lambda b, j: (b, 0, j))
    const = lambda shape: pl.BlockSpec(shape, lambda b, j: (0,) * len(shape))
    out_shape = (
        jax.ShapeDtypeStruct((nb, l, 2 * D_ATT), BF16),
        jax.ShapeDtypeStruct((nb, l, 2 * D_ATT), BF16),
        jax.ShapeDtypeStruct((nb, l, D_ATT), F32),
        jax.ShapeDtypeStruct((nb, D_ATT, l), BF16),
        jax.ShapeDtypeStruct((nb, D_ATT, l), F32),
        jax.ShapeDtypeStruct((nb, l, D_SSM), F32),
        jax.ShapeDtypeStruct((nb, l, CONV_CH), F32),
        jax.ShapeDtypeStruct((nb, l, D_SSM), F32),
        jax.ShapeDtypeStruct((nb, H_ATT, l), F32),
        jax.ShapeDtypeStruct((nb, H_SSM, l), F32),
        jax.ShapeDtypeStruct((nb, H_ATT, l), F32),
        jax.ShapeDtypeStruct((nb, H_ATT, (l // bt) * LANES), F32),
    )
    out_specs = (tok(2 * D_ATT), tok(2 * D_ATT), tok(D_ATT), tok_t(D_ATT), tok_t(D_ATT), tok(D_SSM),
                 tok(CONV_CH), tok(D_SSM), tok_t(H_ATT), tok_t(H_SSM), tok_t(H_ATT),
                 pl.BlockSpec((1, H_ATT, LANES), lambda b, j: (b, 0, j)))
    head_sel = jnp.asarray(np.repeat(np.eye(H_ATT, dtype=np.float32), HD_ATT, axis=1), BF16)
    return pl.pallas_call(
        _inproj_kernel,
        out_shape=out_shape,
        grid=(nb, l // bt),
        in_specs=[tok(D_MODEL), const((1, D_MODEL)), const((D_MODEL, _MAIN_COLS)),
                  const((D_ATT, D_MODEL)), const((2 * H_ATT, D_MODEL)), const((H_ATT, 1)),
                  const((H_SSM, 1)), const((1, D_SSM)), const((1, LANES)), const((bt, bt)),
                  const((3 * LANES, 2 * D_ATT)), const((1, 2 * D_ATT)), const((H_ATT, D_ATT))],
        out_specs=out_specs,
        scratch_shapes=[pltpu.VMEM((8, LANES), F32), pltpu.VMEM((H_ATT, LANES), F32),
                        pltpu.VMEM((H_ATT, LANES), F32)],
        compiler_params=_cparams(2),
        name="in_projection",
    )(x, g, w_main, w_vt, w_small_t, b_forget, dt_bias, dt_bias_x, b_forget_x, tril, sel, ones, head_sel)


def _fox_prompt_kernel(qx_ref, kx_ref, vt_ref, c2t_ref, kn_ref, o_ref, *, blk, kt):
    qi = pl.program_id(2)
    q = qx_ref[0, :, 0:LANES]
    cq = jnp.broadcast_to(qx_ref[0, 0:1, LANES:2 * LANES], (blk, LANES))
    lane = lax.broadcasted_iota(jnp.int32, (blk, LANES), 1)
    first = lane < HD_ATT
    zero = jnp.zeros_like(q)
    minus = jnp.full_like(q, -1.0)
    q_heads, bounds = [], []
    for hh in range(2):
        ext = jnp.where((lane >= 3 * hh) & (lane < 3 * hh + 3), minus,
                        jnp.where((lane >= 6 + 3 * hh) & (lane < 9 + 3 * hh), cq, zero))
        own = jnp.where(first, q, zero) if hh == 0 else jnp.where(first, zero, q)
        q_heads.append(jnp.concatenate([own, ext], axis=1))
        qsq = _dot_nt(jnp.ones((8, LANES), BF16), own * own)[0:1]
        c2q = c2t_ref[0, 0, hh:hh + 1, :]
        bounds.append(BOUND_SCALE * jnp.sqrt(qsq * kn_ref[0, 0, hh:hh + 1, 0:1])
                      + (c2q[:, 0:1] - c2q) + BOUND_MARGIN)

    def v_ones(hh, k0, n):
        vt = vt_ref[0, hh * HD_ATT:(hh + 1) * HD_ATT, pl.ds(k0, n)]
        return jnp.concatenate([vt, jnp.ones((ONES_ROWS, n), BF16)], axis=0)

    def causal(st, k0, n):
        key = lax.broadcasted_iota(jnp.int32, (n, blk), 0) + (k0 - qi * blk)
        qry = lax.broadcasted_iota(jnp.int32, (n, blk), 1)
        return jnp.where(key <= qry, st, NEG_BIG)

    def fast_tile(s, carry, masked):
        k0 = pl.multiple_of(s * kt, kt)
        kj = kx_ref[0, pl.ds(k0, kt), :]
        st_next = _dot_nt(kj, q_heads[0])
        out = []
        for hh in range(2):
            g, acc = carry[hh]
            st = st_next
            if hh == 0:
                st_next = _dot_nt(kj, q_heads[1])
            if masked:
                st = causal(st, k0, kt)
            g = jnp.maximum(g, jnp.max(st.reshape(kt // 8, 8, blk), axis=0))
            p = jnp.exp2(st - bounds[hh]).astype(BF16)
            out.append((g, acc + _dot(v_ones(hh, k0, kt), p)))
        return tuple(out)

    n_tiles = (qi * blk + blk + kt - 1) // kt
    init = tuple((jnp.full((8, blk), NEG_BIG, F32), jnp.zeros((HD_ATT + ONES_ROWS, blk), F32))
                 for _ in range(2))
    carry = lax.fori_loop(0, n_tiles - 1, lambda s, c: fast_tile(s, c, False), init)
    done = fast_tile(n_tiles - 1, carry, True)
    o_t = jnp.concatenate([acc[0:HD_ATT] / acc[HD_ATT:HD_ATT + 1] for _, acc in done], axis=0)
    o_ref[0] = o_t.T.astype(o_ref.dtype)
    slack = jnp.maximum(*[jnp.max(bounds[hh] - jnp.max(done[hh][0], axis=0, keepdims=True))
                          for hh in range(2)])

    @pl.when(slack > SLACK_LIMIT)
    def _():
        def exact_tile(j, carry, masked):
            k0 = pl.multiple_of(j * blk, blk)
            kj = kx_ref[0, pl.ds(k0, blk), :]
            out = []
            for hh in range(2):
                m, acc = carry[hh]
                st = _dot_nt(kj, q_heads[hh])
                if masked:
                    st = causal(st, k0, blk)
                m_new = jnp.maximum(m, jnp.max(st, axis=0, keepdims=True))
                p = jnp.exp2(st - m_new).astype(BF16)
                out.append((m_new, jnp.exp2(m - m_new) * acc + _dot(v_ones(hh, k0, blk), p)))
            return tuple(out)

        init_x = tuple((jnp.full((1, blk), NEG_BIG, F32), jnp.zeros((HD_ATT + ONES_ROWS, blk), F32))
                       for _ in range(2))
        carry_x = lax.fori_loop(0, qi, lambda j, c: exact_tile(j, c, False), init_x)
        done_x = exact_tile(qi, carry_x, True)
        o_x = jnp.concatenate([acc[0:HD_ATT] / acc[HD_ATT:HD_ATT + 1] for _, acc in done_x], axis=0)
        o_ref[0] = o_x.T.astype(o_ref.dtype)


def _fox_prompt(qx, kx, vtb, c2t, kn, blk):
    nb, l, _ = qx.shape
    n_pairs = D_ATT // LANES
    kt = 2 * blk if l % (2 * blk) == 0 else blk
    pair_rows = lambda a: a.reshape(nb, n_pairs, 2, a.shape[-1])
    return pl.pallas_call(
        functools.partial(_fox_prompt_kernel, blk=blk, kt=kt),
        out_shape=jax.ShapeDtypeStruct((nb, l, D_ATT), BF16),
        grid=(nb, n_pairs, l // blk),
        in_specs=[pl.BlockSpec((1, blk, 2 * LANES), lambda b, p, i: (b, i, p)),
                  pl.BlockSpec((1, l, 2 * LANES), lambda b, p, i: (b, 0, p)),
                  pl.BlockSpec((1, LANES, l), lambda b, p, i: (b, p, 0)),
                  pl.BlockSpec((1, 1, 2, blk), lambda b, p, i: (b, p, 0, i)),
                  pl.BlockSpec((1, 1, 2, LANES), lambda b, p, i: (b, p, 0, i))],
        out_specs=pl.BlockSpec((1, blk, LANES), lambda b, p, i: (b, i, p)),
        compiler_params=_cparams(3),
        name="fox_prompt",
    )(qx, kx, vtb, pair_rows(c2t), pair_rows(kn))


def _ssd_gate_norm(y, z, g):
    return _rms(y * _silu(z), g)


def _ssd_prompt_kernel(xbc_ref, z_ref, dtx_ref, dtt_ref, cw_ref, cb_ref, alx_ref, alc_ref,
                       dsx_ref, g_ref, tril_ref, triu_ref,
                       y_ref, st_ref, xp_ref, state_ref):
    q = SSD_CHUNK
    c = pl.program_id(1)

    @pl.when(c == 0)
    def _():
        xp_ref[0:8, :] = jnp.zeros((8, CONV_CH), F32)
        state_ref[...] = jnp.zeros_like(state_ref)

    xp_ref[8:8 + q, :] = xbc_ref[0]
    acc = cb_ref[...]
    for tap in range(CONV_WIDTH):
        off = 8 - (CONV_WIDTH - 1) + tap
        acc = acc + xp_ref[off:off + q, :] * cw_ref[tap:tap + 1, :]
    xp_ref[0:8, :] = xp_ref[q:q + 8, :]
    act = _silu(acc)
    xs = act[:, :D_SSM]
    bmat = act[:, D_SSM:D_SSM + N_BC_GROUPS * N_STATE]
    cmat = act[:, D_SSM + N_BC_GROUPS * N_STATE:]

    dt = dtx_ref[0]
    a_x = -jnp.exp(alx_ref[...]) * dt
    hi, mid, lo = _split3(a_x)
    parts = jnp.concatenate([hi, mid, lo], axis=1).astype(BF16)
    r = _dot(tril_ref[...], parts)
    acs_x = r[:, 0:D_SSM] + r[:, D_SSM:2 * D_SSM] + r[:, 2 * D_SSM:3 * D_SSM]
    a_t = -jnp.exp(alc_ref[...]) * dtt_ref[0]
    acs_t = _cumsum_lanes(a_t, triu_ref[...])

    xdt = xs * dt
    a_last = acs_x[q - 1:q, :]
    xdt_end = (xdt * jnp.exp(a_last - acs_x)).astype(BF16)
    e_acs = jnp.exp(acs_x)
    xdt_b = xdt.astype(BF16)
    state_old = state_ref[...]
    state_b = state_old.astype(BF16)

    row = lax.broadcasted_iota(jnp.int32, (q, q), 0)
    col = lax.broadcasted_iota(jnp.int32, (q, q), 1)
    causal = row >= col
    first = lax.broadcasted_iota(jnp.int32, (q, LANES), 1) < P_SSM

    y_parts = []
    for g in range(N_BC_GROUPS):
        cg = cmat[:, g * N_STATE:(g + 1) * N_STATE].astype(BF16)
        bg = bmat[:, g * N_STATE:(g + 1) * N_STATE].astype(BF16)
        cb = _dot_nt(cg, bg)
        gl = slice(g * GROUP_W, (g + 1) * GROUP_W)
        y_off = _dot(cg, state_b[:, gl]) * e_acs[:, gl]
        for pr in range(GROUP_W // LANES):
            lo_lane = g * GROUP_W + pr * LANES
            pair = []
            for hh in range(2):
                h = lo_lane // P_SSM + hh
                seg = acs_x[:, h * P_SSM:h * P_SSM + 1] - acs_t[h:h + 1, :]
                decay = jnp.exp(jnp.where(causal, seg, -jnp.inf))
                gm = (cb * decay).astype(BF16)
                pair.append(_dot(gm, xdt_b[:, lo_lane:lo_lane + LANES]))
            y_parts.append(jnp.where(first, pair[0], pair[1]) + y_off[:, pr * LANES:(pr + 1) * LANES])
        state_ref[:, gl] = jnp.exp(a_last[:, gl]) * state_old[:, gl] + _dot_tn(bg, xdt_end[:, gl])
    y = jnp.concatenate(y_parts, axis=1) + dsx_ref[...] * xs
    y_ref[0] = _ssd_gate_norm(y, z_ref[0], g_ref[...]).astype(y_ref.dtype)

    @pl.when(c == pl.num_programs(1) - 1)
    def _():
        st_ref[0] = state_ref[...].T


def _ssd_prompt(xbc, z, dtx, dtt, conv_w, conv_b, a_log_x, a_log_c, d_skip_x, norm_g):
    nb, l, _ = xbc.shape
    q = SSD_CHUNK
    tril = jnp.tril(jnp.ones((q, q), F32)).astype(BF16)
    triu = jnp.triu(jnp.ones((q, q), F32)).astype(BF16)
    tok = lambda w: pl.BlockSpec((1, q, w), lambda b, c: (b, c, 0))
    const = lambda shape: pl.BlockSpec(shape, lambda b, c: (0,) * len(shape))
    return pl.pallas_call(
        _ssd_prompt_kernel,
        out_shape=(jax.ShapeDtypeStruct((nb, l, D_SSM), BF16),
                   jax.ShapeDtypeStruct((nb, D_SSM, N_STATE), F32)),
        grid=(nb, l // q),
        in_specs=[tok(CONV_CH), tok(D_SSM), tok(D_SSM),
                  pl.BlockSpec((1, H_SSM, q), lambda b, c: (b, 0, c)),
                  const((CONV_WIDTH, CONV_CH)), const((1, CONV_CH)), const((1, D_SSM)),
                  const((H_SSM, 1)), const((1, D_SSM)), const((1, D_SSM)),
                  const((q, q)), const((q, q))],
        out_specs=(tok(D_SSM), pl.BlockSpec((1, D_SSM, N_STATE), lambda b, c: (b, 0, 0))),
        scratch_shapes=[pltpu.VMEM((q + 8, CONV_CH), F32), pltpu.VMEM((N_STATE, D_SSM), F32)],
        compiler_params=_cparams(2),
        name="ssd_prompt",
    )(xbc, z, dtx, dtt, conv_w, conv_b, a_log_x, a_log_c, d_skip_x, norm_g, tril, triu)


def _mix_out_kernel(x_ref, att_ref, ssm_ref, wo_ref, g_ref, wq_ref, x1_ref, qc_ref):
    x1 = x_ref[...] + _dot(att_ref[...], wo_ref[0:D_ATT, :]) + _dot(ssm_ref[...], wo_ref[D_ATT:, :])
    x1_ref[...] = x1
    h = _rms(x1, g_ref[...]).astype(BF16)
    qc_ref[...] = (_dot(h, wq_ref[...]) * X_SCALE).astype(BF16)


def _mix_out(x, att, ssm, w_out, g_cross, w_cq, bt):
    t = x.shape[0]
    tok = lambda w: pl.BlockSpec((bt, w), lambda i: (i, 0))
    const = lambda shape: pl.BlockSpec(shape, lambda i: (0,) * len(shape))
    return pl.pallas_call(
        _mix_out_kernel,
        out_shape=(jax.ShapeDtypeStruct((t, D_MODEL), F32), jax.ShapeDtypeStruct((t, D_MODEL), BF16)),
        grid=(t // bt,),
        in_specs=[tok(D_MODEL), tok(D_ATT), tok(D_SSM), const((D_MODEL, D_MODEL)),
                  const((1, D_MODEL)), const((D_MODEL, D_MODEL))],
        out_specs=(tok(D_MODEL), tok(D_MODEL)),
        compiler_params=_cparams(1),
        name="mix_out",
    )(x, att, ssm, w_out, g_cross, w_cq)


def _ffn_kernel(x1_ref, o_ref, wco_ref, g_ref, wg_ref, wu_ref, wd_ref, gf_ref, y_ref):
    x2 = x1_ref[...] + _dot(o_ref[...], wco_ref[...])
    h = _rms(x2, g_ref[...]).astype(BF16)
    u = (_silu(_dot(h, wg_ref[...])) * _dot(h, wu_ref[...])).astype(BF16)
    x3 = x2 + _dot(u, wd_ref[...])
    y_ref[...] = _rms(x3, gf_ref[...])


def _ffn(x1, o, w_co, g_ffn, w_gate, w_up, w_down, g_final, bt):
    t = x1.shape[0]
    d_ff = w_gate.shape[1]
    tok = lambda w: pl.BlockSpec((bt, w), lambda i: (i, 0))
    const = lambda shape: pl.BlockSpec(shape, lambda i: (0,) * len(shape),
                                       pipeline_mode=pl.Buffered(1))
    return pl.pallas_call(
        _ffn_kernel,
        out_shape=jax.ShapeDtypeStruct((t, D_MODEL), F32),
        grid=(t // bt,),
        in_specs=[tok(D_MODEL), tok(D_MODEL), const((D_MODEL, D_MODEL)), const((1, D_MODEL)),
                  const((D_MODEL, d_ff)), const((D_MODEL, d_ff)), const((d_ff, D_MODEL)),
                  const((1, D_MODEL))],
        out_specs=tok(D_MODEL),
        compiler_params=_cparams(1),
        name="cross_out_ffn",
    )(x1, o, w_co, g_ffn, w_gate, w_up, w_down, g_final)


def _memory_kv_kernel(mem_ref, g_ref, wk_ref, wv_ref, k32_ref, v32_ref, kb_ref, vb_ref):
    mn = _rms(mem_ref[0], g_ref[...]).astype(BF16)
    k = _dot(mn, wk_ref[...])
    v = _dot(mn, wv_ref[...])
    k32_ref[0] = k
    v32_ref[0] = v
    kb_ref[0] = k.astype(BF16)
    vb_ref[0] = v.astype(BF16)


def _memory_kv(mem, g, w_ck, w_cv):
    nb, m, _ = mem.shape
    blk = pl.BlockSpec((1, m, D_MODEL), lambda b: (b, 0, 0))
    const = lambda shape: pl.BlockSpec(shape, lambda b: (0,) * len(shape))
    f = jax.ShapeDtypeStruct((nb, m, D_MODEL), F32)
    h = jax.ShapeDtypeStruct((nb, m, D_MODEL), BF16)
    return pl.pallas_call(
        _memory_kv_kernel,
        out_shape=(f, f, h, h),
        grid=(nb,),
        in_specs=[blk, const((1, D_MODEL)), const((D_MODEL, D_MODEL)), const((D_MODEL, D_MODEL))],
        out_specs=(blk, blk, blk, blk),
        compiler_params=_cparams(1),
        name="memory_kv",
    )(mem, g, w_ck, w_cv)


def _softmax_rows(s):
    m = jnp.max(s, axis=-1, keepdims=True)
    p = jnp.exp(s - m)
    return p / jnp.sum(p, axis=-1, keepdims=True)


def _cross_prompt_kernel(q_ref, k_ref, v_ref, o_ref):
    for h in range(H_X):
        hl = slice(h * HD_X, (h + 1) * HD_X)
        p = _softmax_rows(_dot_nt(q_ref[0, :, hl], k_ref[0, :, hl]))
        o_ref[0, :, hl] = _dot(p.astype(BF16), v_ref[0, :, hl]).astype(o_ref.dtype)


def _cross_prompt(qc, mk, mv, bt):
    nb, l, _ = qc.shape
    m = mk.shape[1]
    tok = pl.BlockSpec((1, bt, D_MODEL), lambda b, i: (b, i, 0))
    mem = pl.BlockSpec((1, m, D_MODEL), lambda b, i: (b, 0, 0))
    return pl.pallas_call(
        _cross_prompt_kernel,
        out_shape=jax.ShapeDtypeStruct((nb, l, D_MODEL), BF16),
        grid=(nb, l // bt),
        in_specs=[tok, mem, mem],
        out_specs=tok,
        compiler_params=_cparams(2),
        name="cross_prompt",
    )(qc, mk, mv)


_X_SUB = 2 * H_X


def _cross_sample_kernel(q_ref, k_ref, v_ref, o_ref, *, lq):
    nsb, n_mem = k_ref.shape[0], k_ref.shape[1]
    n = n_mem * _X_SUB
    rows = lq * _X_SUB
    lane = lax.broadcasted_iota(jnp.int32, (rows, n), 1)
    sub = lax.broadcasted_iota(jnp.int32, (rows, n), 0)
    own = (lane % _X_SUB) == (sub % _X_SUB)
    cls = lane % _X_SUB
    for i in range(nsb):
        k5 = k_ref[i].reshape(n, LANES).astype(BF16)
        s5 = jnp.where(own, _dot_nt(q_ref[i], k5), 0.0)
        u = jnp.concatenate(
            [jnp.broadcast_to(jnp.sum(s5[t * _X_SUB:(t + 1) * _X_SUB], axis=0, keepdims=True), (_X_SUB, n))
             for t in range(lq)], axis=0)
        s = u + pltpu.roll(u, n - H_X, 1)
        m_full = jnp.zeros((rows, n), F32)
        for h in range(H_X):
            m_h = jnp.max(jnp.where(cls == h, s, NEG_BIG), axis=1, keepdims=True)
            m_full = jnp.where(cls == h, m_h, m_full)
        p = jnp.where(cls < H_X, jnp.exp(s - m_full), 0.0)
        l_full = jnp.ones((rows, n), F32)
        for h in range(H_X):
            l_h = jnp.sum(jnp.where(cls == h, p, 0.0), axis=1, keepdims=True)
            l_full = jnp.where(cls == h, l_h, l_full)
        p = p / l_full
        p5 = jnp.where(own, p + pltpu.roll(p, H_X, 1), 0.0).astype(BF16)
        v5 = v_ref[i].reshape(n, LANES).astype(BF16)
        o_ref[i] = _dot(p5, v5).astype(o_ref.dtype)


def _cross_sample(q5, mem_k, mem_v, lq, n_seq_blk):
    ns, rows, _ = q5.shape
    m = mem_k.shape[1]
    tok = pl.BlockSpec((n_seq_blk, rows, LANES), lambda i: (i, 0, 0))
    mem = pl.BlockSpec((n_seq_blk, m, _X_SUB, LANES), lambda i: (i, 0, 0, 0))
    return pl.pallas_call(
        functools.partial(_cross_sample_kernel, lq=lq),
        out_shape=jax.ShapeDtypeStruct((ns, rows, LANES), BF16),
        grid=(ns // n_seq_blk,),
        in_specs=[tok, mem, mem],
        out_specs=tok,
        compiler_params=_cparams(1),
        name="cross_sample",
    )(q5, mem_k, mem_v)


def _fox_sample_kernel(pt_ref, q_ref, kn_ref, vn_ref, lfn_ref, tril_ref, *rest, n_pages, lq):
    del pt_ref
    k_refs = rest[0:n_pages]
    v_refs = rest[n_pages:2 * n_pages]
    lf_refs = rest[2 * n_pages:3 * n_pages]
    o_ref = rest[3 * n_pages]
    kt_ref = rest[3 * n_pages + 1]
    vt_ref = rest[3 * n_pages + 2]

    for j in range(n_pages):
        kt_ref[:, :, j * PAGE_SIZE:(j + 1) * PAGE_SIZE] = k_refs[j][0].astype(BF16)
        vt_ref[:, :, j * PAGE_SIZE:(j + 1) * PAGE_SIZE] = v_refs[j][0].astype(BF16)

    lf_all = jnp.concatenate([r[0] for r in lf_refs], axis=0) * LOG2E
    hi, mid, lo = _split3(lf_all)
    tril = tril_ref[...]
    incl = _dot(hi.astype(BF16), tril) + _dot(mid.astype(BF16), tril) + _dot(lo.astype(BF16), tril)
    after = jnp.zeros((H_ATT, 1), F32)
    pieces = [None] * n_pages
    for j in reversed(range(n_pages)):
        sl = slice(j * H_ATT, (j + 1) * H_ATT)
        pieces[j] = incl[sl] - lf_all[sl] + after
        after = after + incl[sl][:, 0:1]
    bias = jnp.concatenate(pieces, axis=1)

    lfn = lfn_ref[0] * LOG2E
    lane = lax.broadcasted_iota(jnp.int32, (H_ATT, lq), 1)
    cn = jnp.zeros((H_ATT, lq), F32)
    for t in range(lq):
        cn = cn + jnp.where(lane >= t, lfn[:, t:t + 1], 0.0)
    causal = (lax.broadcasted_iota(jnp.int32, (lq, lq), 1)
              <= lax.broadcasted_iota(jnp.int32, (lq, lq), 0))

    for h in range(H_ATT):
        qh = q_ref[0, h]
        s = _dot(qh, kt_ref[h]) + bias[h:h + 1, :]
        s_new = jnp.where(causal, _dot_nt(qh, kn_ref[0, h]) - cn[h:h + 1, :], NEG_BIG)
        m = jnp.maximum(jnp.max(s, axis=-1, keepdims=True), jnp.max(s_new, axis=-1, keepdims=True))
        p = jnp.exp2(s - m)
        p_new = jnp.exp2(s_new - m)
        l = jnp.sum(p, axis=-1, keepdims=True) + jnp.sum(p_new, axis=-1, keepdims=True)
        o = _dot_nt(p.astype(BF16), vt_ref[h]) + _dot(p_new.astype(BF16), vn_ref[0, h])
        o_ref[0, h] = (o / l).astype(o_ref.dtype)


def _fox_sample(qh, kh, vh, lfn, cache_kt, cache_vt, cache_lft, page_table):
    ns, _, lq, _ = qh.shape
    n_pages = page_table.shape[1]
    past = n_pages * PAGE_SIZE
    tril = jnp.tril(jnp.ones((PAGE_SIZE, PAGE_SIZE), F32)).astype(BF16)
    tok = pl.BlockSpec((1, H_ATT, lq, HD_ATT), lambda i, pt: (i, 0, 0, 0))

    def page(j, shape):
        return pl.BlockSpec((1,) + shape, lambda i, pt: (pt[i * n_pages + j],) + (0,) * len(shape))

    in_specs = [tok, tok, tok, pl.BlockSpec((1, H_ATT, lq), lambda i, pt: (i, 0, 0)),
                pl.BlockSpec((PAGE_SIZE, PAGE_SIZE), lambda i, pt: (0, 0))]
    in_specs += [page(j, (H_ATT, HD_ATT, PAGE_SIZE)) for j in range(n_pages)]
    in_specs += [page(j, (H_ATT, HD_ATT, PAGE_SIZE)) for j in range(n_pages)]
    in_specs += [page(j, (H_ATT, PAGE_SIZE)) for j in range(n_pages)]
    grid_spec = pltpu.PrefetchScalarGridSpec(
        num_scalar_prefetch=1,
        grid=(ns,),
        in_specs=in_specs,
        out_specs=tok,
        scratch_shapes=[pltpu.VMEM((H_ATT, HD_ATT, past), BF16), pltpu.VMEM((H_ATT, HD_ATT, past), BF16)],
    )
    return pl.pallas_call(
        functools.partial(_fox_sample_kernel, n_pages=n_pages, lq=lq),
        out_shape=jax.ShapeDtypeStruct((ns, H_ATT, lq, HD_ATT), BF16),
        grid_spec=grid_spec,
        compiler_params=_cparams(1),
        name="fox_sample",
    )(page_table.reshape(-1), qh, kh, vh, lfn, tril,
      *([cache_kt] * n_pages), *([cache_vt] * n_pages), *([cache_lft] * n_pages))


def _ssd_sample_kernel(xbc_ref, z_ref, dtx_ref, cs_ref, st_ref, cw_ref, cb_ref, alx_ref, dsx_ref,
                       g_ref, y_ref, sto_ref, xp_ref, rows_ref, brow_ref, *, lq):
    nsb = xbc_ref.shape[0]
    assert nsb * (lq + 1) <= LANES
    kw = CONV_WIDTH - 1
    xp_ref[:, 0:kw, :] = cs_ref[...]
    xp_ref[:, kw:kw + lq, :] = xbc_ref[...]
    acc = cb_ref[...]
    for tap in range(CONV_WIDTH):
        acc = acc + xp_ref[:, tap:tap + lq, :] * cw_ref[tap:tap + 1, :]
    act = _silu(acc)
    xs = act[:, :, :D_SSM]
    bmat = act[:, :, D_SSM:D_SSM + N_BC_GROUPS * N_STATE]
    cmat = act[:, :, D_SSM + N_BC_GROUPS * N_STATE:]
    dt = dtx_ref[...]
    a = -jnp.exp(alx_ref[...]) * dt
    xdt = xs * dt
    acs = [a[:, 0:1, :]]
    for t in range(1, lq):
        acs.append(acs[-1] + a[:, t:t + 1, :])
    a_last = acs[lq - 1]

    ccat = jnp.concatenate([cmat[:, :, g * N_STATE:(g + 1) * N_STATE] for g in range(N_BC_GROUPS)],
                           axis=1).astype(BF16)
    state = st_ref[...]
    r = jnp.einsum("sgn,shn->sgh", ccat, state.astype(BF16), preferred_element_type=F32)
    group0 = lax.broadcasted_iota(jnp.int32, (1, 1, D_SSM), 2) < GROUP_W
    y_off = jnp.where(group0, r[:, 0:lq, :], r[:, lq:2 * lq, :])

    ys = []
    for t in range(lq):
        y_t = jnp.exp(acs[t]) * y_off[:, t:t + 1, :] + dsx_ref[...] * xs[:, t:t + 1, :]
        for s in range(t + 1):
            cb = jnp.concatenate(
                [jnp.broadcast_to(
                    jnp.sum(cmat[:, t:t + 1, g * N_STATE:(g + 1) * N_STATE]
                            * bmat[:, s:s + 1, g * N_STATE:(g + 1) * N_STATE], axis=-1, keepdims=True),
                    (nsb, 1, GROUP_W)) for g in range(N_BC_GROUPS)], axis=2)
            y_t = y_t + cb * jnp.exp(acs[t] - acs[s]) * xdt[:, s:s + 1, :]
        ys.append(y_t)
    y = jnp.concatenate(ys, axis=1)
    y_ref[...] = _ssd_gate_norm(y, z_ref[...], g_ref[...]).astype(y_ref.dtype)

    upd = jnp.concatenate([xdt[:, s:s + 1, :] * jnp.exp(a_last - acs[s]) for s in range(lq)], axis=1)
    e_last = jnp.exp(a_last)
    rows_ref[...] = jnp.zeros_like(rows_ref)
    brow_ref[...] = jnp.zeros_like(brow_ref)
    for i in range(nsb):
        rows_ref[i * lq:(i + 1) * lq, :] = upd[i]
        rows_ref[nsb * lq + i:nsb * lq + i + 1, :] = e_last[i]
        brow_ref[i * lq:(i + 1) * lq, :] = bmat[i]
    cols = rows_ref[...].T
    upd_cols = cols.astype(BF16)
    b_all = brow_ref[...]
    row_seq = lax.broadcasted_iota(jnp.int32, (LANES, N_BC_GROUPS * N_STATE), 0) // lq
    top = lax.broadcasted_iota(jnp.int32, (D_SSM, N_STATE), 0) < GROUP_W
    for i in range(nsb):
        b_rows = jnp.where(row_seq == i, b_all, 0.0).astype(BF16)
        m = _dot(upd_cols, b_rows)
        add = jnp.where(top, m[:, 0:N_STATE], m[:, N_STATE:2 * N_STATE])
        decay = jnp.broadcast_to(cols[:, nsb * lq + i:nsb * lq + i + 1], (D_SSM, N_STATE))
        sto_ref[i] = decay * state[i] + add


def _ssd_sample(xbc, z, dtx, conv_state, state, conv_w, conv_b, a_log_x, d_skip_x, norm_g, nsb):
    ns, lq, _ = xbc.shape
    seq = lambda r, w: pl.BlockSpec((nsb, r, w), lambda i: (i, 0, 0))
    const = lambda shape: pl.BlockSpec(shape, lambda i: (0,) * len(shape))
    return pl.pallas_call(
        functools.partial(_ssd_sample_kernel, lq=lq),
        out_shape=(jax.ShapeDtypeStruct((ns, lq, D_SSM), BF16),
                   jax.ShapeDtypeStruct((ns, D_SSM, N_STATE), F32)),
        grid=(ns // nsb,),
        in_specs=[seq(lq, CONV_CH), seq(lq, D_SSM), seq(lq, D_SSM), seq(CONV_WIDTH - 1, CONV_CH),
                  seq(D_SSM, N_STATE), const((CONV_WIDTH, CONV_CH)), const((1, CONV_CH)),
                  const((1, D_SSM)), const((1, D_SSM)), const((1, D_SSM))],
        out_specs=(seq(lq, D_SSM), seq(D_SSM, N_STATE)),
        scratch_shapes=[pltpu.VMEM((nsb, 8, CONV_CH), F32), pltpu.VMEM((LANES, D_SSM), F32),
                        pltpu.VMEM((LANES, N_BC_GROUPS * N_STATE), F32)],
        compiler_params=_cparams(1),
        name="ssd_sample",
    )(xbc, z, dtx, conv_state, state, conv_w, conv_b, a_log_x, d_skip_x, norm_g)


def _pick(pref, n):
    return pref if n % pref == 0 else n


def kernel(x_prompt, x_sample, mem_prompt, cache_k, cache_v, cache_logf, page_table, cache_mem_k, cache_mem_v, state_conv, state_ssm, norm_mix_g, w_in, b_forget, conv_w, conv_b, dt_bias, a_log, d_skip, ssm_norm_g, w_out, norm_cross_g, norm_mem_g, w_cq, w_ck, w_cv, w_co, norm_ffn_g, w_gate, w_up, w_down, final_norm_g):
    assert w_in.shape[0] == 1, "one layer"
    nb, l, _ = x_prompt.shape
    ns, lq, _ = x_sample.shape
    n_phys = cache_k.shape[1]
    row = lambda v: v.reshape(1, -1).astype(F32)
    colv = lambda v: v.reshape(-1, 1).astype(F32)
    per_ch = lambda v: jnp.repeat(v.astype(F32), P_SSM).reshape(1, D_SSM)

    w = w_in[0]
    cuts = [D_ATT, 2 * D_ATT, 3 * D_ATT, 3 * D_ATT + H_ATT, 3 * D_ATT + H_ATT + D_SSM,
            3 * D_ATT + H_ATT + D_SSM + CONV_CH]
    w_q, w_k, w_v, w_f, w_z, w_xbc, w_dt = jnp.split(w, cuts, axis=1)
    lane_pad = lambda a: jnp.pad(a, ((0, 0), (0, LANES - a.shape[1])))
    w_main = jnp.concatenate([w_q, w_k, w_z, w_xbc, jnp.repeat(w_dt, P_SSM, axis=1), lane_pad(w_f)],
                             axis=1).astype(BF16)
    w_small_t = jnp.concatenate([w_f, w_dt], axis=1).T.astype(BF16)
    in_params = (row(norm_mix_g[0]), w_main, w_v.T.astype(BF16), w_small_t, colv(b_forget[0]),
                 colv(dt_bias[0]), per_ch(dt_bias[0]), lane_pad(row(b_forget[0])))
    ssd_params = (conv_w[0], row(conv_b[0]), per_ch(a_log[0]))
    ssd_tail = (per_ch(d_skip[0]), row(ssm_norm_g[0]))
    w_out_b, w_cq_b, w_co_b = w_out[0].astype(BF16), w_cq[0].astype(BF16), w_co[0].astype(BF16)
    w_gate_b, w_up_b, w_down_b = w_gate[0].astype(BF16), w_up[0].astype(BF16), w_down[0].astype(BF16)
    g_cross, g_ffn, g_final = row(norm_cross_g[0]), row(norm_ffn_g[0]), row(final_norm_g)

    def tail(x, att, ssm, cross, bt):
        x1, qc = _mix_out(x, att, ssm, w_out_b, g_cross, w_cq_b, bt)
        o = cross(qc)
        return _ffn(x1, o, w_co_b, g_ffn, w_gate_b, w_up_b, w_down_b, g_final, bt)

    bt = _pick(512, l)
    qx, kx, k32, vtb, v32t, z, xbc, dtx, logft, dtt, c2t, kn = _in_projection(x_prompt, *in_params, bt)
    assert bt == _pick(512, l)
    att = _fox_prompt(qx, kx, vtb, c2t, kn, bt)
    ssm, st_p = _ssd_prompt(xbc, z, dtx, dtt, *ssd_params, colv(a_log[0]), *ssd_tail)
    mk32, mv32, mkb, mvb = _memory_kv(mem_prompt, row(norm_mem_g[0]), w_ck[0].astype(BF16),
                                      w_cv[0].astype(BF16))
    cross_p = lambda qc: _cross_prompt(qc.reshape(nb, l, D_MODEL), mkb, mvb, bt).reshape(nb * l, D_MODEL)
    y_prompt = tail(x_prompt.reshape(nb * l, D_MODEL), att.reshape(nb * l, D_ATT),
                    ssm.reshape(nb * l, D_SSM), cross_p, bt).reshape(nb, l, D_MODEL)

    ts = ns * lq
    bts = _pick(512, ts)
    qx_s, _, k32_s, vtb_s, v32t_s, z_s, xbc_s, dtx_s, logft_s, _, _, _ = _in_projection(
        x_sample.reshape(1, ts, D_MODEL), *in_params, bts)
    seq3 = lambda a: a.reshape(ns, lq, a.shape[-1])
    lfn = logft_s.reshape(H_ATT, ns, lq).transpose(1, 0, 2)
    head_major = lambda a: a.reshape(ns, lq, H_ATT, HD_ATT).transpose(0, 2, 1, 3)
    from_t = lambda a: a.reshape(H_ATT, HD_ATT, ns, lq)
    qb_s = qx_s.reshape(ts, D_ATT // LANES, 2, LANES)[:, :, 0, :]
    att_s = _fox_sample(head_major(qb_s), head_major(k32_s.astype(BF16)),
                        from_t(vtb_s).transpose(2, 0, 3, 1), lfn,
                        cache_k[0].transpose(0, 2, 3, 1), cache_v[0].transpose(0, 2, 3, 1),
                        cache_logf[0].transpose(0, 2, 1), page_table)
    att_s = att_s.transpose(0, 2, 1, 3)
    ssm_s, st_s = _ssd_sample(seq3(xbc_s), seq3(z_s), seq3(dtx_s), state_conv[0],
                              state_ssm[0].reshape(ns, D_SSM, N_STATE), *ssd_params, *ssd_tail,
                              _pick(8, ns))
    n_mem = mem_prompt.shape[1]
    stored = lambda a: a.reshape(ns, -1, H_X, 2, LANES).transpose(0, 1, 3, 2, 4).reshape(ns, -1, _X_SUB, LANES)
    cross_s = lambda qc: _cross_sample(
        stored(qc).reshape(ns, lq * _X_SUB, LANES), stored(cache_mem_k[0]), stored(cache_mem_v[0]), lq,
        _pick(4, ns)).reshape(ns, lq, 2, H_X, LANES).transpose(0, 1, 3, 2, 4).reshape(ts, D_MODEL)
    y_sample = tail(x_sample.reshape(ts, D_MODEL), att_s.reshape(ts, D_ATT), ssm_s.reshape(ts, D_SSM),
                    cross_s, bts).reshape(ns, lq, D_MODEL)

    heads = lambda a, n: a.reshape(1, n, -1, H_ATT, HD_ATT)
    kw = CONV_WIDTH - 1
    return (y_prompt, y_sample,
            heads(k32, nb), v32t.reshape(nb, H_ATT, HD_ATT, l).transpose(0, 3, 1, 2)[None],
            logft.transpose(0, 2, 1)[None],
            mk32.reshape(1, nb, n_mem, H_X, HD_X), mv32.reshape(1, nb, n_mem, H_X, HD_X),
            xbc[:, l - kw:, :][None], st_p.reshape(1, nb, H_SSM, P_SSM, N_STATE),
            heads(k32_s, ns), from_t(v32t_s).transpose(2, 3, 0, 1)[None],
            logft_s.reshape(H_ATT, ns, lq).transpose(1, 2, 0)[None],
            seq3(xbc_s)[:, lq - kw:, :][None], st_s.reshape(1, ns, H_SSM, P_SSM, N_STATE))
```

```python
import functools
import math

import numpy as np
import jax
import jax.numpy as jnp
from jax import lax
from jax.experimental import pallas as pl
from jax.experimental.pallas import tpu as pltpu

F32 = jnp.float32
BF16 = jnp.bfloat16

D_MODEL = 1024
D_ATT = 512
H_ATT = 8
HD_ATT = 64
D_SSM = 512
H_SSM = 8
P_SSM = 64
N_STATE = 128
N_BC_GROUPS = 2
GROUP_W = D_SSM // N_BC_GROUPS
CONV_WIDTH = 4
CONV_CH = D_SSM + 2 * N_BC_GROUPS * N_STATE
H_X = 4
HD_X = 256
EPS = 1e-6
ATT_SCALE = HD_ATT ** -0.5
X_SCALE = HD_X ** -0.5
SSD_CHUNK = 128
PAGE_SIZE = 128
NEG_BIG = -1e30
LOG2E = math.log2(math.e)

BOUND_SCALE = 1.03
BOUND_MARGIN = 2.0
SLACK_LIMIT = 80.0

LANES = 128
ONES_ROWS = 16
VMEM_LIMIT = 56 * 1024 * 1024

_Q0, _K0, _Z0, _XBC0, _DT0, _F0, _MAIN_COLS = 0, 512, 1024, 1536, 2560, 3072, 3200


def _cparams(n_axes):
    return pltpu.CompilerParams(dimension_semantics=("arbitrary",) * n_axes,
                                vmem_limit_bytes=VMEM_LIMIT)


def _rms(x, g):
    ms = jnp.mean(x * x, axis=-1, keepdims=True)
    return x * lax.rsqrt(ms + EPS) * g


def _softplus(x):
    return jnp.maximum(x, 0.0) + jnp.log1p(jnp.exp(-jnp.abs(x)))


def _log_sigmoid(x):
    return jnp.minimum(x, 0.0) - jnp.log1p(jnp.exp(-jnp.abs(x)))


def _silu(x):
    return x * (1.0 / (1.0 + jnp.exp(-x)))


def _split3(x):
    hi = x.astype(BF16).astype(F32)
    r1 = x - hi
    mid = r1.astype(BF16).astype(F32)
    lo = (r1 - mid).astype(BF16).astype(F32)
    return hi, mid, lo


def _dot(a, b):
    return jnp.dot(a, b, preferred_element_type=F32)


def _dot_nt(a, b):
    return lax.dot_general(a, b, (((1,), (1,)), ((), ())), preferred_element_type=F32)


def _dot_tn(a, b):
    return lax.dot_general(a, b, (((0,), (0,)), ((), ())), preferred_element_type=F32)


def _cumsum_lanes(x, tri_upper):
    hi, mid, lo = _split3(x)
    parts = jnp.concatenate([hi, mid, lo], axis=0).astype(BF16)
    r = _dot(parts, tri_upper)
    return r[0:8] + r[8:16] + r[16:24]


def _inproj_kernel(x_ref, g_ref, wm_ref, wvt_ref, wst_ref, bf_ref, dtb_ref, dtbx_ref, bfx_ref,
                   tril_ref, sel_ref, ones_ref, hsel_ref,
                   qx_ref, kx_ref, k32_ref, vtb_ref, v32t_ref, z_ref, xbc_ref, dtx_ref,
                   logft_ref, dtt_ref, c2t_ref, kn_ref, carry_ref, carry_t_ref, carry_kn_ref):
    @pl.when(pl.program_id(1) == 0)
    def _():
        carry_ref[...] = jnp.zeros_like(carry_ref)
        carry_t_ref[...] = jnp.zeros_like(carry_t_ref)
        carry_kn_ref[...] = jnp.zeros_like(carry_kn_ref)

    h = _rms(x_ref[0], g_ref[...]).astype(BF16)
    f_rows = _dot(h, wm_ref[:, _F0:_MAIN_COLS])
    small = _dot_nt(wst_ref[...], h)
    q = (_dot(h, wm_ref[:, _Q0:_K0]) * (ATT_SCALE * LOG2E)).astype(BF16)

    lf2 = _log_sigmoid(f_rows + bfx_ref[...]) * LOG2E
    parts = jnp.concatenate(_split3(lf2), axis=1).astype(BF16)
    r = _dot(tril_ref[...], parts)
    k = _dot(h, wm_ref[:, _K0:_Z0])
    k32_ref[0] = k
    kb = k.astype(BF16)

    logft = _log_sigmoid(small[0:8] + bf_ref[...])
    logft_ref[0] = logft
    dtt_ref[0] = _softplus(small[8:16] + dtb_ref[...])
    hi, mid, lo = _split3(logft * LOG2E)
    r_t = _dot_nt(jnp.concatenate([hi, mid, lo], axis=0).astype(BF16), tril_ref[...])
    vt = _dot_nt(wvt_ref[...], h)
    v32t_ref[0] = vt
    vtb_ref[0] = vt.astype(BF16)

    c2 = r[:, 0:LANES] + r[:, LANES:2 * LANES] + r[:, 2 * LANES:3 * LANES] + carry_ref[0:1, :]
    bt = c2.shape[0]
    carry_ref[...] = jnp.broadcast_to(c2[bt - 1:bt, :], carry_ref.shape)
    csplit = jnp.concatenate(_split3(c2), axis=1).astype(BF16)
    ext = (_dot(csplit, sel_ref[...]) + ones_ref[...]).astype(BF16)
    z_ref[0] = _dot(h, wm_ref[:, _Z0:_XBC0])

    c2t = r_t[0:8] + r_t[8:16] + r_t[16:24] + carry_t_ref[:, 0:1]
    c2t_ref[0] = c2t
    carry_t_ref[...] = jnp.broadcast_to(c2t[:, c2t.shape[1] - 1:], carry_t_ref.shape)
    kf = kb.astype(F32)
    ksq_t = _dot_nt(hsel_ref[...], (kf * kf).astype(BF16))
    xbc_ref[0] = _dot(h, wm_ref[:, _XBC0:_DT0])
    kn = jnp.maximum(carry_kn_ref[...], jnp.max(ksq_t, axis=1, keepdims=True))
    carry_kn_ref[...] = kn
    kn_ref[0] = kn
    dtx_ref[0] = _softplus(_dot(h, wm_ref[:, _DT0:_F0]) + dtbx_ref[...])
    for p in range(D_ATT // LANES):
        pl_ = slice(p * LANES, (p + 1) * LANES)
        kx_ref[0, :, 2 * p * LANES:(2 * p + 1) * LANES] = kb[:, pl_]
        kx_ref[0, :, (2 * p + 1) * LANES:(2 * p + 2) * LANES] = ext[:, pl_]
        qx_ref[0, :, 2 * p * LANES:(2 * p + 1) * LANES] = q[:, pl_]
        qx_ref[0, :, (2 * p + 1) * LANES:(2 * p + 2) * LANES] = ext[:, D_ATT + p * LANES:D_ATT + (p + 1) * LANES]


def _attention_extras():
    sel = np.zeros((3 * LANES, 2 * D_ATT), np.float32)
    ones = np.zeros((1, 2 * D_ATT), np.float32)
    for p in range(D_ATT // LANES):
        ones[0, p * LANES + 6:p * LANES + 12] = 1.0
        for hh in range(2):
            for s in range(3):
                sel[s * LANES + 2 * p + hh, p * LANES + 3 * hh + s] = 1.0
                sel[s * LANES + 2 * p + hh, D_ATT + p * LANES + 6 + 3 * hh + s] = 1.0
    return jnp.asarray(sel, BF16), jnp.asarray(ones, F32)


def _in_projection(x, g, w_main, w_vt, w_small_t, b_forget, dt_bias, dt_bias_x, b_forget_x, bt):
    nb, l, _ = x.shape
    tril = jnp.tril(jnp.ones((bt, bt), F32)).astype(BF16)
    sel, ones = _attention_extras()
    tok = lambda w: pl.BlockSpec((1, bt, w), lambda b, j: (b, j, 0))
    tok_t = lambda r: pl.BlockSpec((1, r, bt), lambda b, j: (b, 0, j))
    const = lambda shape: pl.BlockSpec(shape, lambda b, j: (0,) * len(shape))
    out_shape = (
        jax.ShapeDtypeStruct((nb, l, 2 * D_ATT), BF16),
        jax.ShapeDtypeStruct((nb, l, 2 * D_ATT), BF16),
        jax.ShapeDtypeStruct((nb, l, D_ATT), F32),
        jax.ShapeDtypeStruct((nb, D_ATT, l), BF16),
        jax.ShapeDtypeStruct((nb, D_ATT, l), F32),
        jax.ShapeDtypeStruct((nb, l, D_SSM), F32),
        jax.ShapeDtypeStruct((nb, l, CONV_CH), F32),
        jax.ShapeDtypeStruct((nb, l, D_SSM), F32),
        jax.ShapeDtypeStruct((nb, H_ATT, l), F32),
        jax.ShapeDtypeStruct((nb, H_SSM, l), F32),
        jax.ShapeDtypeStruct((nb, H_ATT, l), F32),
        jax.ShapeDtypeStruct((nb, H_ATT, (l // bt) * LANES), F32),
    )
    out_specs = (tok(2 * D_ATT), tok(2 * D_ATT), tok(D_ATT), tok_t(D_ATT), tok_t(D_ATT), tok(D_SSM),
                 tok(CONV_CH), tok(D_SSM), tok_t(H_ATT), tok_t(H_SSM), tok_t(H_ATT),
                 pl.BlockSpec((1, H_ATT, LANES), lambda b, j: (b, 0, j)))
    head_sel = jnp.asarray(np.repeat(np.eye(H_ATT, dtype=np.float32), HD_ATT, axis=1), BF16)
    return pl.pallas_call(
        _inproj_kernel,
        out_shape=out_shape,
        grid=(nb, l // bt),
        in_specs=[tok(D_MODEL), const((1, D_MODEL)), const((D_MODEL, _MAIN_COLS)),
                  const((D_ATT, D_MODEL)), const((2 * H_ATT, D_MODEL)), const((H_ATT, 1)),
                  const((H_SSM, 1)), const((1, D_SSM)), const((1, LANES)), const((bt, bt)),
                  const((3 * LANES, 2 * D_ATT)), const((1, 2 * D_ATT)), const((H_ATT, D_ATT))],
        out_specs=out_specs,
        scratch_shapes=[pltpu.VMEM((8, LANES), F32), pltpu.VMEM((H_ATT, LANES), F32),
                        pltpu.VMEM((H_ATT, LANES), F32)],
        compiler_params=_cparams(2),
        name="in_projection",
    )(x, g, w_main, w_vt, w_small_t, b_forget, dt_bias, dt_bias_x, b_forget_x, tril, sel, ones, head_sel)


def _fox_prompt_kernel(qx_ref, kx_ref, vt_ref, c2t_ref, kn_ref, o_ref, *, blk, kt):
    qi = pl.program_id(2)
    q = qx_ref[0, :, 0:LANES]
    cq = jnp.broadcast_to(qx_ref[0, 0:1, LANES:2 * LANES], (blk, LANES))
    lane = lax.broadcasted_iota(jnp.int32, (blk, LANES), 1)
    first = lane < HD_ATT
    zero = jnp.zeros_like(q)
    minus = jnp.full_like(q, -1.0)
    q_heads, bounds = [], []
    for hh in range(2):
        ext = jnp.where((lane >= 3 * hh) & (lane < 3 * hh + 3), minus,
                        jnp.where((lane >= 6 + 3 * hh) & (lane < 9 + 3 * hh), cq, zero))
        own = jnp.where(first, q, zero) if hh == 0 else jnp.where(first, zero, q)
        q_heads.append(jnp.concatenate([own, ext], axis=1))
        qsq = _dot_nt(jnp.ones((8, LANES), BF16), own * own)[0:1]
        c2q = c2t_ref[0, 0, hh:hh + 1, :]
        bounds.append(BOUND_SCALE * jnp.sqrt(qsq * kn_ref[0, 0, hh:hh + 1, 0:1])
                      + (c2q[:, 0:1] - c2q) + BOUND_MARGIN)

    def v_ones(hh, k0, n):
        vt = vt_ref[0, hh * HD_ATT:(hh + 1) * HD_ATT, pl.ds(k0, n)]
        return jnp.concatenate([vt, jnp.ones((ONES_ROWS, n), BF16)], axis=0)

    def causal(st, k0, n):
        key = lax.broadcasted_iota(jnp.int32, (n, blk), 0) + (k0 - qi * blk)
        qry = lax.broadcasted_iota(jnp.int32, (n, blk), 1)
        return jnp.where(key <= qry, st, NEG_BIG)

    def fast_tile(s, carry, masked):
        k0 = pl.multiple_of(s * kt, kt)
        kj = kx_ref[0, pl.ds(k0, kt), :]
        st_next = _dot_nt(kj, q_heads[0])
        out = []
        for hh in range(2):
            g, acc = carry[hh]
            st = st_next
            if hh == 0:
                st_next = _dot_nt(kj, q_heads[1])
            if masked:
                st = causal(st, k0, kt)
            g = jnp.maximum(g, jnp.max(st.reshape(kt // 8, 8, blk), axis=0))
            p = jnp.exp2(st - bounds[hh]).astype(BF16)
            out.append((g, acc + _dot(v_ones(hh, k0, kt), p)))
        return tuple(out)

    n_tiles = (qi * blk + blk + kt - 1) // kt
    init = tuple((jnp.full((8, blk), NEG_BIG, F32), jnp.zeros((HD_ATT + ONES_ROWS, blk), F32))
                 for _ in range(2))
    carry = lax.fori_loop(0, n_tiles - 1, lambda s, c: fast_tile(s, c, False), init)
    done = fast_tile(n_tiles - 1, carry, True)
    o_t = jnp.concatenate([acc[0:HD_ATT] / acc[HD_ATT:HD_ATT + 1] for _, acc in done], axis=0)
    o_ref[0] = o_t.T.astype(o_ref.dtype)
    slack = jnp.maximum(*[jnp.max(bounds[hh] - jnp.max(done[hh][0], axis=0, keepdims=True))
                          for hh in range(2)])

    @pl.when(slack > SLACK_LIMIT)
    def _():
        def exact_tile(j, carry, masked):
            k0 = pl.multiple_of(j * blk, blk)
            kj = kx_ref[0, pl.ds(k0, blk), :]
            out = []
            for hh in range(2):
                m, acc = carry[hh]
                st = _dot_nt(kj, q_heads[hh])
                if masked:
                    st = causal(st, k0, blk)
                m_new = jnp.maximum(m, jnp.max(st, axis=0, keepdims=True))
                p = jnp.exp2(st - m_new).astype(BF16)
                out.append((m_new, jnp.exp2(m - m_new) * acc + _dot(v_ones(hh, k0, blk), p)))
            return tuple(out)

        init_x = tuple((jnp.full((1, blk), NEG_BIG, F32), jnp.zeros((HD_ATT + ONES_ROWS, blk), F32))
                       for _ in range(2))
        carry_x = lax.fori_loop(0, qi, lambda j, c: exact_tile(j, c, False), init_x)
        done_x = exact_tile(qi, carry_x, True)
        o_x = jnp.concatenate([acc[0:HD_ATT] / acc[HD_ATT:HD_ATT + 1] for _, acc in done_x], axis=0)
        o_ref[0] = o_x.T.astype(o_ref.dtype)


def _fox_prompt(qx, kx, vtb, c2t, kn, blk):
    nb, l, _ = qx.shape
    n_pairs = D_ATT // LANES
    kt = 2 * blk if l % (2 * blk) == 0 else blk
    pair_rows = lambda a: a.reshape(nb, n_pairs, 2, a.shape[-1])
    return pl.pallas_call(
        functools.partial(_fox_prompt_kernel, blk=blk, kt=kt),
        out_shape=jax.ShapeDtypeStruct((nb, l, D_ATT), BF16),
        grid=(nb, n_pairs, l // blk),
        in_specs=[pl.BlockSpec((1, blk, 2 * LANES), lambda b, p, i: (b, i, p)),
                  pl.BlockSpec((1, l, 2 * LANES), lambda b, p, i: (b, 0, p)),
                  pl.BlockSpec((1, LANES, l), lambda b, p, i: (b, p, 0)),
                  pl.BlockSpec((1, 1, 2, blk), lambda b, p, i: (b, p, 0, i)),
                  pl.BlockSpec((1, 1, 2, LANES), lambda b, p, i: (b, p, 0, i))],
        out_specs=pl.BlockSpec((1, blk, LANES), lambda b, p, i: (b, i, p)),
        compiler_params=_cparams(3),
        name="fox_prompt",
    )(qx, kx, vtb, pair_rows(c2t), pair_rows(kn))


def _ssd_gate_norm(y, z, g):
    return _rms(y * _silu(z), g)


def _ssd_prompt_kernel(xbc_ref, z_ref, dtx_ref, dtt_ref, cw_ref, cb_ref, alx_ref, alc_ref,
                       dsx_ref, g_ref, tril_ref, triu_ref,
                       y_ref, st_ref, xp_ref, state_ref):
    q = SSD_CHUNK
    c = pl.program_id(1)

    @pl.when(c == 0)
    def _():
        xp_ref[...] = jnp.zeros_like(xp_ref)
        state_ref[...] = jnp.zeros_like(state_ref)

    xb = xbc_ref[0]
    prev = xp_ref[...]
    row8 = lax.broadcasted_iota(jnp.int32, (8, CONV_CH), 0)
    acc = cb_ref[...] + xb * cw_ref[CONV_WIDTH - 1:CONV_WIDTH, :]
    for back in range(1, CONV_WIDTH):
        sh = pltpu.roll(xb, back, 0)
        head = jnp.where(row8 < back, pltpu.roll(prev, back, 0), sh[0:8])
        shifted = jnp.concatenate([head, sh[8:]], axis=0)
        acc = acc + shifted * cw_ref[CONV_WIDTH - 1 - back:CONV_WIDTH - back, :]
    xp_ref[...] = xb[q - 8:q]
    act = _silu(acc)
    xs = act[:, :D_SSM]
    bmat = act[:, D_SSM:D_SSM + N_BC_GROUPS * N_STATE]
    cmat = act[:, D_SSM + N_BC_GROUPS * N_STATE:]

    dt = dtx_ref[0]
    a_x = -jnp.exp(alx_ref[...]) * dt
    hi, mid, lo = _split3(a_x)
    parts = jnp.concatenate([hi, mid, lo], axis=1).astype(BF16)
    r = _dot(tril_ref[...], parts)
    acs_x = r[:, 0:D_SSM] + r[:, D_SSM:2 * D_SSM] + r[:, 2 * D_SSM:3 * D_SSM]
    a_t = -jnp.exp(alc_ref[...]) * dtt_ref[0]
    acs_t = _cumsum_lanes(a_t, triu_ref[...])

    xdt = xs * dt
    a_last = acs_x[q - 1:q, :]
    xdt_end = (xdt * jnp.exp(a_last - acs_x)).astype(BF16)
    e_acs = jnp.exp(acs_x)
    xdt_b = xdt.astype(BF16)
    state_old = state_ref[...]
    state_b = state_old.astype(BF16)

    row = lax.broadcasted_iota(jnp.int32, (q, q), 0)
    col = lax.broadcasted_iota(jnp.int32, (q, q), 1)
    causal = row >= col
    first = lax.broadcasted_iota(jnp.int32, (q, LANES), 1) < P_SSM

    y_parts = []
    for g in range(N_BC_GROUPS):
        cg = cmat[:, g * N_STATE:(g + 1) * N_STATE].astype(BF16)
        bg = bmat[:, g * N_STATE:(g + 1) * N_STATE].astype(BF16)
        cb = _dot_nt(cg, bg)
        gl = slice(g * GROUP_W, (g + 1) * GROUP_W)
        y_off = _dot(cg, state_b[:, gl]) * e_acs[:, gl]
        for pr in range(GROUP_W // LANES):
            lo_lane = g * GROUP_W + pr * LANES
            pair = []
            for hh in range(2):
                h = lo_lane // P_SSM + hh
                seg = acs_x[:, h * P_SSM:h * P_SSM + 1] - acs_t[h:h + 1, :]
                decay = jnp.exp(jnp.where(causal, seg, -jnp.inf))
                gm = (cb * decay).astype(BF16)
                pair.append(_dot(gm, xdt_b[:, lo_lane:lo_lane + LANES]))
            y_parts.append(jnp.where(first, pair[0], pair[1]) + y_off[:, pr * LANES:(pr + 1) * LANES])
        state_ref[:, gl] = jnp.exp(a_last[:, gl]) * state_old[:, gl] + _dot_tn(bg, xdt_end[:, gl])
    y = jnp.concatenate(y_parts, axis=1) + dsx_ref[...] * xs
    y_ref[0] = _ssd_gate_norm(y, z_ref[0], g_ref[...]).astype(y_ref.dtype)

    @pl.when(c == pl.num_programs(1) - 1)
    def _():
        st_ref[0] = state_ref[...].T


def _ssd_prompt(xbc, z, dtx, dtt, conv_w, conv_b, a_log_x, a_log_c, d_skip_x, norm_g):
    nb, l, _ = xbc.shape
    q = SSD_CHUNK
    tril = jnp.tril(jnp.ones((q, q), F32)).astype(BF16)
    triu = jnp.triu(jnp.ones((q, q), F32)).astype(BF16)
    tok = lambda w: pl.BlockSpec((1, q, w), lambda b, c: (b, c, 0))
    const = lambda shape: pl.BlockSpec(shape, lambda b, c: (0,) * len(shape))
    return pl.pallas_call(
        _ssd_prompt_kernel,
        out_shape=(jax.ShapeDtypeStruct((nb, l, D_SSM), BF16),
                   jax.ShapeDtypeStruct((nb, D_SSM, N_STATE), F32)),
        grid=(nb, l // q),
        in_specs=[tok(CONV_CH), tok(D_SSM), tok(D_SSM),
                  pl.BlockSpec((1, H_SSM, q), lambda b, c: (b, 0, c)),
                  const((CONV_WIDTH, CONV_CH)), const((1, CONV_CH)), const((1, D_SSM)),
                  const((H_SSM, 1)), const((1, D_SSM)), const((1, D_SSM)),
                  const((q, q)), const((q, q))],
        out_specs=(tok(D_SSM), pl.BlockSpec((1, D_SSM, N_STATE), lambda b, c: (b, 0, 0))),
        scratch_shapes=[pltpu.VMEM((8, CONV_CH), F32), pltpu.VMEM((N_STATE, D_SSM), F32)],
        compiler_params=_cparams(2),
        name="ssd_prompt",
    )(xbc, z, dtx, dtt, conv_w, conv_b, a_log_x, a_log_c, d_skip_x, norm_g, tril, triu)


def _mix_out_kernel(x_ref, att_ref, ssm_ref, wo_ref, g_ref, wq_ref, x1_ref, qc_ref):
    x1 = x_ref[...] + _dot(att_ref[...], wo_ref[0:D_ATT, :]) + _dot(ssm_ref[...], wo_ref[D_ATT:, :])
    x1_ref[...] = x1
    h = _rms(x1, g_ref[...]).astype(BF16)
    qc_ref[...] = (_dot(h, wq_ref[...]) * X_SCALE).astype(BF16)


def _mix_out(x, att, ssm, w_out, g_cross, w_cq, bt):
    t = x.shape[0]
    tok = lambda w: pl.BlockSpec((bt, w), lambda i: (i, 0))
    const = lambda shape: pl.BlockSpec(shape, lambda i: (0,) * len(shape))
    return pl.pallas_call(
        _mix_out_kernel,
        out_shape=(jax.ShapeDtypeStruct((t, D_MODEL), F32), jax.ShapeDtypeStruct((t, D_MODEL), BF16)),
        grid=(t // bt,),
        in_specs=[tok(D_MODEL), tok(D_ATT), tok(D_SSM), const((D_MODEL, D_MODEL)),
                  const((1, D_MODEL)), const((D_MODEL, D_MODEL))],
        out_specs=(tok(D_MODEL), tok(D_MODEL)),
        compiler_params=_cparams(1),
        name="mix_out",
    )(x, att, ssm, w_out, g_cross, w_cq)


def _ffn_kernel(x1_ref, o_ref, wco_ref, g_ref, wg_ref, wu_ref, wd_ref, gf_ref, y_ref):
    x2 = x1_ref[...] + _dot(o_ref[...], wco_ref[...])
    h = _rms(x2, g_ref[...]).astype(BF16)
    u = (_silu(_dot(h, wg_ref[...])) * _dot(h, wu_ref[...])).astype(BF16)
    x3 = x2 + _dot(u, wd_ref[...])
    y_ref[...] = _rms(x3, gf_ref[...])


def _ffn(x1, o, w_co, g_ffn, w_gate, w_up, w_down, g_final, bt):
    t = x1.shape[0]
    d_ff = w_gate.shape[1]
    tok = lambda w: pl.BlockSpec((bt, w), lambda i: (i, 0))
    const = lambda shape: pl.BlockSpec(shape, lambda i: (0,) * len(shape),
                                       pipeline_mode=pl.Buffered(1))
    return pl.pallas_call(
        _ffn_kernel,
        out_shape=jax.ShapeDtypeStruct((t, D_MODEL), F32),
        grid=(t // bt,),
        in_specs=[tok(D_MODEL), tok(D_MODEL), const((D_MODEL, D_MODEL)), const((1, D_MODEL)),
                  const((D_MODEL, d_ff)), const((D_MODEL, d_ff)), const((d_ff, D_MODEL)),
                  const((1, D_MODEL))],
        out_specs=tok(D_MODEL),
        compiler_params=_cparams(1),
        name="cross_out_ffn",
    )(x1, o, w_co, g_ffn, w_gate, w_up, w_down, g_final)


def _memory_kv_kernel(mem_ref, g_ref, wk_ref, wv_ref, k32_ref, v32_ref, kb_ref, vb_ref):
    mn = _rms(mem_ref[0], g_ref[...]).astype(BF16)
    k = _dot(mn, wk_ref[...])
    v = _dot(mn, wv_ref[...])
    k32_ref[0] = k
    v32_ref[0] = v
    kb_ref[0] = k.astype(BF16)
    vb_ref[0] = v.astype(BF16)


def _memory_kv(mem, g, w_ck, w_cv):
    nb, m, _ = mem.shape
    blk = pl.BlockSpec((1, m, D_MODEL), lambda b: (b, 0, 0))
    const = lambda shape: pl.BlockSpec(shape, lambda b: (0,) * len(shape))
    f = jax.ShapeDtypeStruct((nb, m, D_MODEL), F32)
    h = jax.ShapeDtypeStruct((nb, m, D_MODEL), BF16)
    return pl.pallas_call(
        _memory_kv_kernel,
        out_shape=(f, f, h, h),
        grid=(nb,),
        in_specs=[blk, const((1, D_MODEL)), const((D_MODEL, D_MODEL)), const((D_MODEL, D_MODEL))],
        out_specs=(blk, blk, blk, blk),
        compiler_params=_cparams(1),
        name="memory_kv",
    )(mem, g, w_ck, w_cv)


def _softmax_rows(s):
    m = jnp.max(s, axis=-1, keepdims=True)
    p = jnp.exp(s - m)
    return p / jnp.sum(p, axis=-1, keepdims=True)


def _cross_prompt_kernel(q_ref, k_ref, v_ref, o_ref):
    for h in range(H_X):
        hl = slice(h * HD_X, (h + 1) * HD_X)
        p = _softmax_rows(_dot_nt(q_ref[0, :, hl], k_ref[0, :, hl]))
        o_ref[0, :, hl] = _dot(p.astype(BF16), v_ref[0, :, hl]).astype(o_ref.dtype)


def _cross_prompt(qc, mk, mv, bt):
    nb, l, _ = qc.shape
    m = mk.shape[1]
    tok = pl.BlockSpec((1, bt, D_MODEL), lambda b, i: (b, i, 0))
    mem = pl.BlockSpec((1, m, D_MODEL), lambda b, i: (b, 0, 0))
    return pl.pallas_call(
        _cross_prompt_kernel,
        out_shape=jax.ShapeDtypeStruct((nb, l, D_MODEL), BF16),
        grid=(nb, l // bt),
        in_specs=[tok, mem, mem],
        out_specs=tok,
        compiler_params=_cparams(2),
        name="cross_prompt",
    )(qc, mk, mv)


_X_SUB = 2 * H_X


def _cross_sample_kernel(q_ref, k_ref, v_ref, o_ref, *, lq):
    nsb, n_mem = k_ref.shape[0], k_ref.shape[1]
    n = n_mem * _X_SUB
    rows = lq * _X_SUB
    n_tiles = n // LANES
    cls = lax.broadcasted_iota(jnp.int32, (rows, LANES), 1) % _X_SUB
    own = cls == lax.broadcasted_iota(jnp.int32, (rows, LANES), 0) % _X_SUB
    valid = cls < H_X

    class_steps = (8, 16, 32, 64)

    def per_token(x):
        x3 = x.reshape(lq, _X_SUB, LANES)
        return jnp.broadcast_to(jnp.sum(x3, axis=1, keepdims=True), x3.shape).reshape(rows, LANES)

    seqs = range(nsb)
    s5 = [_dot_nt(q_ref[i], k_ref[i].reshape(n, LANES).astype(BF16)) for i in seqs]
    s_t = [[per_token(jnp.where(own, s5[i][:, j * LANES:(j + 1) * LANES], 0.0)) for j in range(n_tiles)]
           for i in seqs]
    s_t = [[u + pltpu.roll(u, LANES - H_X, 1) for u in s_t[i]] for i in seqs]
    m = [functools.reduce(jnp.maximum, s_t[i]) for i in seqs]
    for sh in class_steps:
        m = [jnp.maximum(x, pltpu.roll(x, sh, 1)) for x in m]
    p_t = [[jnp.where(valid, jnp.exp(s - m[i]), 0.0) for s in s_t[i]] for i in seqs]
    l = [functools.reduce(jnp.add, p_t[i]) for i in seqs]
    for sh in class_steps:
        l = [x + pltpu.roll(x, sh, 1) for x in l]
    inv = [1.0 / jnp.where(valid, l[i], 1.0) for i in seqs]
    pn = [[p * inv[i] for p in p_t[i]] for i in seqs]
    p5 = [jnp.concatenate([jnp.where(own, x + pltpu.roll(x, H_X, 1), 0.0).astype(BF16) for x in pn[i]], axis=1)
          for i in seqs]
    for i in seqs:
        v5 = v_ref[i].reshape(n, LANES).astype(BF16)
        o_ref[i] = _dot(p5[i], v5).astype(o_ref.dtype)


def _cross_sample(q5, mem_k, mem_v, lq, n_seq_blk):
    ns, rows, _ = q5.shape
    m = mem_k.shape[1]
    tok = pl.BlockSpec((n_seq_blk, rows, LANES), lambda i: (i, 0, 0))
    mem = pl.BlockSpec((n_seq_blk, m, _X_SUB, LANES), lambda i: (i, 0, 0, 0))
    return pl.pallas_call(
        functools.partial(_cross_sample_kernel, lq=lq),
        out_shape=jax.ShapeDtypeStruct((ns, rows, LANES), BF16),
        grid=(ns // n_seq_blk,),
        in_specs=[tok, mem, mem],
        out_specs=tok,
        compiler_params=_cparams(1),
        name="cross_sample",
    )(q5, mem_k, mem_v)


def _fox_sample_kernel(pt_ref, q_ref, kn_ref, vn_ref, lfn_ref, tril_ref, *rest, n_pages, lq):
    del pt_ref
    k_refs = rest[0:n_pages]
    v_refs = rest[n_pages:2 * n_pages]
    lf_refs = rest[2 * n_pages:3 * n_pages]
    o_ref = rest[3 * n_pages]
    kt_ref = rest[3 * n_pages + 1]
    vt_ref = rest[3 * n_pages + 2]

    for j in range(n_pages):
        kt_ref[:, :, j * PAGE_SIZE:(j + 1) * PAGE_SIZE] = k_refs[j][0].astype(BF16)
        vt_ref[:, :, j * PAGE_SIZE:(j + 1) * PAGE_SIZE] = v_refs[j][0].astype(BF16)

    lf_all = jnp.concatenate([r[0] for r in lf_refs], axis=0) * LOG2E
    hi, mid, lo = _split3(lf_all)
    tril = tril_ref[...]
    incl = _dot(hi.astype(BF16), tril) + _dot(mid.astype(BF16), tril) + _dot(lo.astype(BF16), tril)
    after = jnp.zeros((H_ATT, 1), F32)
    pieces = [None] * n_pages
    for j in reversed(range(n_pages)):
        sl = slice(j * H_ATT, (j + 1) * H_ATT)
        pieces[j] = incl[sl] - lf_all[sl] + after
        after = after + incl[sl][:, 0:1]
    bias = jnp.concatenate(pieces, axis=1)

    lfn = lfn_ref[0] * LOG2E
    lane = lax.broadcasted_iota(jnp.int32, (H_ATT, lq), 1)
    cn = jnp.zeros((H_ATT, lq), F32)
    for t in range(lq):
        cn = cn + jnp.where(lane >= t, lfn[:, t:t + 1], 0.0)
    causal = (lax.broadcasted_iota(jnp.int32, (lq, lq), 1)
              <= lax.broadcasted_iota(jnp.int32, (lq, lq), 0))

    scores = [(_dot(q_ref[0, h], kt_ref[h]), _dot_nt(q_ref[0, h], kn_ref[0, h])) for h in range(H_ATT)]
    probs = []
    for h, (s, s_new) in enumerate(scores):
        s = s + bias[h:h + 1, :]
        s_new = jnp.where(causal, s_new - cn[h:h + 1, :], NEG_BIG)
        m = jnp.maximum(jnp.max(s, axis=-1, keepdims=True), jnp.max(s_new, axis=-1, keepdims=True))
        p = jnp.exp2(s - m)
        p_new = jnp.exp2(s_new - m)
        l = jnp.sum(p, axis=-1, keepdims=True) + jnp.sum(p_new, axis=-1, keepdims=True)
        probs.append((p.astype(BF16), p_new.astype(BF16), l))
    for h, (p, p_new, l) in enumerate(probs):
        o = _dot_nt(p, vt_ref[h]) + _dot(p_new, vn_ref[0, h])
        o_ref[0, h] = (o / l).astype(o_ref.dtype)


def _fox_sample(qh, kh, vh, lfn, cache_kt, cache_vt, cache_lft, page_table):
    ns, _, lq, _ = qh.shape
    n_pages = page_table.shape[1]
    past = n_pages * PAGE_SIZE
    tril = jnp.tril(jnp.ones((PAGE_SIZE, PAGE_SIZE), F32)).astype(BF16)
    tok = pl.BlockSpec((1, H_ATT, lq, HD_ATT), lambda i, pt: (i, 0, 0, 0))

    def page(j, shape):
        return pl.BlockSpec((1,) + shape, lambda i, pt: (pt[i * n_pages + j],) + (0,) * len(shape))

    in_specs = [tok, tok, tok, pl.BlockSpec((1, H_ATT, lq), lambda i, pt: (i, 0, 0)),
                pl.BlockSpec((PAGE_SIZE, PAGE_SIZE), lambda i, pt: (0, 0))]
    in_specs += [page(j, (H_ATT, HD_ATT, PAGE_SIZE)) for j in range(n_pages)]
    in_specs += [page(j, (H_ATT, HD_ATT, PAGE_SIZE)) for j in range(n_pages)]
    in_specs += [page(j, (H_ATT, PAGE_SIZE)) for j in range(n_pages)]
    grid_spec = pltpu.PrefetchScalarGridSpec(
        num_scalar_prefetch=1,
        grid=(ns,),
        in_specs=in_specs,
        out_specs=tok,
        scratch_shapes=[pltpu.VMEM((H_ATT, HD_ATT, past), BF16), pltpu.VMEM((H_ATT, HD_ATT, past), BF16)],
    )
    return pl.pallas_call(
        functools.partial(_fox_sample_kernel, n_pages=n_pages, lq=lq),
        out_shape=jax.ShapeDtypeStruct((ns, H_ATT, lq, HD_ATT), BF16),
        grid_spec=grid_spec,
        compiler_params=_cparams(1),
        name="fox_sample",
    )(page_table.reshape(-1), qh, kh, vh, lfn, tril,
      *([cache_kt] * n_pages), *([cache_vt] * n_pages), *([cache_lft] * n_pages))


def _ssd_sample_kernel(xbc_ref, z_ref, dtx_ref, cs_ref, st_ref, cw_ref, cb_ref, alx_ref, dsx_ref,
                       g_ref, y_ref, sto_ref, xp_ref, rows_ref, brow_ref, *, lq):
    nsb = xbc_ref.shape[0]
    assert nsb * (lq + 1) <= LANES
    kw = CONV_WIDTH - 1
    xp_ref[:, 0:kw, :] = cs_ref[...]
    xp_ref[:, kw:kw + lq, :] = xbc_ref[...]
    acc = cb_ref[...]
    for tap in range(CONV_WIDTH):
        acc = acc + xp_ref[:, tap:tap + lq, :] * cw_ref[tap:tap + 1, :]
    act = _silu(acc)
    xs = act[:, :, :D_SSM]
    bmat = act[:, :, D_SSM:D_SSM + N_BC_GROUPS * N_STATE]
    cmat = act[:, :, D_SSM + N_BC_GROUPS * N_STATE:]
    dt = dtx_ref[...]
    a = -jnp.exp(alx_ref[...]) * dt
    xdt = xs * dt
    acs = [a[:, 0:1, :]]
    for t in range(1, lq):
        acs.append(acs[-1] + a[:, t:t + 1, :])
    a_last = acs[lq - 1]

    ccat = jnp.concatenate([cmat[:, :, g * N_STATE:(g + 1) * N_STATE] for g in range(N_BC_GROUPS)],
                           axis=1).astype(BF16)
    state = st_ref[...]
    r = jnp.einsum("sgn,shn->sgh", ccat, state.astype(BF16), preferred_element_type=F32)
    group0 = lax.broadcasted_iota(jnp.int32, (1, 1, D_SSM), 2) < GROUP_W
    y_off = jnp.where(group0, r[:, 0:lq, :], r[:, lq:2 * lq, :])

    ys = []
    for t in range(lq):
        y_t = jnp.exp(acs[t]) * y_off[:, t:t + 1, :] + dsx_ref[...] * xs[:, t:t + 1, :]
        for s in range(t + 1):
            cb = jnp.concatenate(
                [jnp.broadcast_to(
                    jnp.sum(cmat[:, t:t + 1, g * N_STATE:(g + 1) * N_STATE]
                            * bmat[:, s:s + 1, g * N_STATE:(g + 1) * N_STATE], axis=-1, keepdims=True),
                    (nsb, 1, GROUP_W)) for g in range(N_BC_GROUPS)], axis=2)
            y_t = y_t + cb * jnp.exp(acs[t] - acs[s]) * xdt[:, s:s + 1, :]
        ys.append(y_t)
    y = jnp.concatenate(ys, axis=1)
    y_ref[...] = _ssd_gate_norm(y, z_ref[...], g_ref[...]).astype(y_ref.dtype)

    upd = jnp.concatenate([xdt[:, s:s + 1, :] * jnp.exp(a_last - acs[s]) for s in range(lq)], axis=1)
    e_last = jnp.exp(a_last)
    rows_ref[...] = jnp.zeros_like(rows_ref)
    brow_ref[...] = jnp.zeros_like(brow_ref)
    for i in range(nsb):
        rows_ref[i * lq:(i + 1) * lq, :] = upd[i]
        rows_ref[nsb * lq + i:nsb * lq + i + 1, :] = e_last[i]
        brow_ref[i * lq:(i + 1) * lq, :] = bmat[i]
    cols = rows_ref[...].T
    upd_cols = cols.astype(BF16)
    b_all = brow_ref[...]
    row_seq = lax.broadcasted_iota(jnp.int32, (LANES, N_BC_GROUPS * N_STATE), 0) // lq
    top = lax.broadcasted_iota(jnp.int32, (D_SSM, N_STATE), 0) < GROUP_W
    for i in range(nsb):
        b_rows = jnp.where(row_seq == i, b_all, 0.0).astype(BF16)
        m = _dot(upd_cols, b_rows)
        add = jnp.where(top, m[:, 0:N_STATE], m[:, N_STATE:2 * N_STATE])
        decay = jnp.broadcast_to(cols[:, nsb * lq + i:nsb * lq + i + 1], (D_SSM, N_STATE))
        sto_ref[i] = decay * state[i] + add


def _ssd_sample(xbc, z, dtx, conv_state, state, conv_w, conv_b, a_log_x, d_skip_x, norm_g, nsb):
    ns, lq, _ = xbc.shape
    seq = lambda r, w: pl.BlockSpec((nsb, r, w), lambda i: (i, 0, 0))
    const = lambda shape: pl.BlockSpec(shape, lambda i: (0,) * len(shape))
    return pl.pallas_call(
        functools.partial(_ssd_sample_kernel, lq=lq),
        out_shape=(jax.ShapeDtypeStruct((ns, lq, D_SSM), BF16),
                   jax.ShapeDtypeStruct((ns, D_SSM, N_STATE), F32)),
        grid=(ns // nsb,),
        in_specs=[seq(lq, CONV_CH), seq(lq, D_SSM), seq(lq, D_SSM), seq(CONV_WIDTH - 1, CONV_CH),
                  seq(D_SSM, N_STATE), const((CONV_WIDTH, CONV_CH)), const((1, CONV_CH)),
                  const((1, D_SSM)), const((1, D_SSM)), const((1, D_SSM))],
        out_specs=(seq(lq, D_SSM), seq(D_SSM, N_STATE)),
        scratch_shapes=[pltpu.VMEM((nsb, 8, CONV_CH), F32), pltpu.VMEM((LANES, D_SSM), F32),
                        pltpu.VMEM((LANES, N_BC_GROUPS * N_STATE), F32)],
        compiler_params=_cparams(1),
        name="ssd_sample",
    )(xbc, z, dtx, conv_state, state, conv_w, conv_b, a_log_x, d_skip_x, norm_g)


def _pick(pref, n):
    return pref if n % pref == 0 else n


def kernel(x_prompt, x_sample, mem_prompt, cache_k, cache_v, cache_logf, page_table, cache_mem_k, cache_mem_v, state_conv, state_ssm, norm_mix_g, w_in, b_forget, conv_w, conv_b, dt_bias, a_log, d_skip, ssm_norm_g, w_out, norm_cross_g, norm_mem_g, w_cq, w_ck, w_cv, w_co, norm_ffn_g, w_gate, w_up, w_down, final_norm_g):
    assert w_in.shape[0] == 1, "one layer"
    nb, l, _ = x_prompt.shape
    ns, lq, _ = x_sample.shape
    n_phys = cache_k.shape[1]
    row = lambda v: v.reshape(1, -1).astype(F32)
    colv = lambda v: v.reshape(-1, 1).astype(F32)
    per_ch = lambda v: jnp.repeat(v.astype(F32), P_SSM).reshape(1, D_SSM)

    w = w_in[0]
    cuts = [D_ATT, 2 * D_ATT, 3 * D_ATT, 3 * D_ATT + H_ATT, 3 * D_ATT + H_ATT + D_SSM,
            3 * D_ATT + H_ATT + D_SSM + CONV_CH]
    w_q, w_k, w_v, w_f, w_z, w_xbc, w_dt = jnp.split(w, cuts, axis=1)
    lane_pad = lambda a: jnp.pad(a, ((0, 0), (0, LANES - a.shape[1])))
    w_main = jnp.concatenate([w_q, w_k, w_z, w_xbc, jnp.repeat(w_dt, P_SSM, axis=1), lane_pad(w_f)],
                             axis=1).astype(BF16)
    w_small_t = jnp.concatenate([w_f, w_dt], axis=1).T.astype(BF16)
    in_params = (row(norm_mix_g[0]), w_main, w_v.T.astype(BF16), w_small_t, colv(b_forget[0]),
                 colv(dt_bias[0]), per_ch(dt_bias[0]), lane_pad(row(b_forget[0])))
    ssd_params = (conv_w[0], row(conv_b[0]), per_ch(a_log[0]))
    ssd_tail = (per_ch(d_skip[0]), row(ssm_norm_g[0]))
    w_out_b, w_cq_b, w_co_b = w_out[0].astype(BF16), w_cq[0].astype(BF16), w_co[0].astype(BF16)
    w_gate_b, w_up_b, w_down_b = w_gate[0].astype(BF16), w_up[0].astype(BF16), w_down[0].astype(BF16)
    g_cross, g_ffn, g_final = row(norm_cross_g[0]), row(norm_ffn_g[0]), row(final_norm_g)

    def tail(x, att, ssm, cross, bt):
        x1, qc = _mix_out(x, att, ssm, w_out_b, g_cross, w_cq_b, bt)
        o = cross(qc)
        return _ffn(x1, o, w_co_b, g_ffn, w_gate_b, w_up_b, w_down_b, g_final, bt)

    bt = _pick(512, l)
    qx, kx, k32, vtb, v32t, z, xbc, dtx, logft, dtt, c2t, kn = _in_projection(x_prompt, *in_params, bt)
    assert bt == _pick(512, l)
    att = _fox_prompt(qx, kx, vtb, c2t, kn, bt)
    ssm, st_p = _ssd_prompt(xbc, z, dtx, dtt, *ssd_params, colv(a_log[0]), *ssd_tail)
    mk32, mv32, mkb, mvb = _memory_kv(mem_prompt, row(norm_mem_g[0]), w_ck[0].astype(BF16),
                                      w_cv[0].astype(BF16))
    cross_p = lambda qc: _cross_prompt(qc.reshape(nb, l, D_MODEL), mkb, mvb, bt).reshape(nb * l, D_MODEL)
    y_prompt = tail(x_prompt.reshape(nb * l, D_MODEL), att.reshape(nb * l, D_ATT),
                    ssm.reshape(nb * l, D_SSM), cross_p, bt).reshape(nb, l, D_MODEL)

    ts = ns * lq
    bts = _pick(512, ts)
    qx_s, _, k32_s, vtb_s, v32t_s, z_s, xbc_s, dtx_s, logft_s, _, _, _ = _in_projection(
        x_sample.reshape(1, ts, D_MODEL), *in_params, bts)
    seq3 = lambda a: a.reshape(ns, lq, a.shape[-1])
    lfn = logft_s.reshape(H_ATT, ns, lq).transpose(1, 0, 2)
    head_major = lambda a: a.reshape(ns, lq, H_ATT, HD_ATT).transpose(0, 2, 1, 3)
    from_t = lambda a: a.reshape(H_ATT, HD_ATT, ns, lq)
    qb_s = qx_s.reshape(ts, D_ATT // LANES, 2, LANES)[:, :, 0, :]
    att_s = _fox_sample(head_major(qb_s), head_major(k32_s.astype(BF16)),
                        from_t(vtb_s).transpose(2, 0, 3, 1), lfn,
                        cache_k[0].transpose(0, 2, 3, 1), cache_v[0].transpose(0, 2, 3, 1),
                        cache_logf[0].transpose(0, 2, 1), page_table)
    att_s = att_s.transpose(0, 2, 1, 3)
    ssm_s, st_s = _ssd_sample(seq3(xbc_s), seq3(z_s), seq3(dtx_s), state_conv[0],
                              state_ssm[0].reshape(ns, D_SSM, N_STATE), *ssd_params, *ssd_tail,
                              _pick(8, ns))
    n_mem = mem_prompt.shape[1]
    stored = lambda a: a.reshape(ns, -1, H_X, 2, LANES).transpose(0, 1, 3, 2, 4).reshape(ns, -1, _X_SUB, LANES)
    cross_s = lambda qc: _cross_sample(
        stored(qc).reshape(ns, lq * _X_SUB, LANES), stored(cache_mem_k[0]), stored(cache_mem_v[0]), lq,
        _pick(4, ns)).reshape(ns, lq, 2, H_X, LANES).transpose(0, 1, 3, 2, 4).reshape(ts, D_MODEL)
    y_sample = tail(x_sample.reshape(ts, D_MODEL), att_s.reshape(ts, D_ATT), ssm_s.reshape(ts, D_SSM),
                    cross_s, bts).reshape(ns, lq, D_MODEL)

    heads = lambda a, n: a.reshape(1, n, -1, H_ATT, HD_ATT)
    kw = CONV_WIDTH - 1
    return (y_prompt, y_sample,
            heads(k32, nb), v32t.reshape(nb, H_ATT, HD_ATT, l).transpose(0, 3, 1, 2)[None],
            logft.transpose(0, 2, 1)[None],
            mk32.reshape(1, nb, n_mem, H_X, HD_X), mv32.reshape(1, nb, n_mem, H_X, HD_X),
            xbc[:, l - kw:, :][None], st_p.reshape(1, nb, H_SSM, P_SSM, N_STATE),
            heads(k32_s, ns), from_t(v32t_s).transpose(2, 3, 0, 1)[None],
            logft_s.reshape(H_ATT, ns, lq).transpose(1, 2, 0)[None],
            seq3(xbc_s)[:, lq - kw:, :][None], st_s.reshape(1, ns, H_SSM, P_SSM, N_STATE))
```

```python
import functools
import math

import numpy as np
import jax
import jax.numpy as jnp
from jax import lax
from jax.experimental import pallas as pl
from jax.experimental.pallas import tpu as pltpu

F32 = jnp.float32
BF16 = jnp.bfloat16

D_MODEL = 1024
D_ATT = 512
H_ATT = 8
HD_ATT = 64
D_SSM = 512
H_SSM = 8
P_SSM = 64
N_STATE = 128
N_BC_GROUPS = 2
GROUP_W = D_SSM // N_BC_GROUPS
CONV_WIDTH = 4
CONV_CH = D_SSM + 2 * N_BC_GROUPS * N_STATE
H_X = 4
HD_X = 256
EPS = 1e-6
ATT_SCALE = HD_ATT ** -0.5
X_SCALE = HD_X ** -0.5
SSD_CHUNK = 128
PAGE_SIZE = 128
NEG_BIG = -1e30
LOG2E = math.log2(math.e)

BOUND_SCALE = 1.03
BOUND_MARGIN = 2.0
SLACK_LIMIT = 80.0
UNDERFLOW_LOG2 = 154.0

LANES = 128
ONES_ROWS = 16
VMEM_LIMIT = 56 * 1024 * 1024

_Q0, _K0, _Z0, _XBC0, _DT0, _F0, _MAIN_COLS = 0, 512, 1024, 1536, 2560, 3072, 3200


def _cparams(n_axes):
    return pltpu.CompilerParams(dimension_semantics=("arbitrary",) * n_axes,
                                vmem_limit_bytes=VMEM_LIMIT)


def _rms(x, g):
    ms = jnp.mean(x * x, axis=-1, keepdims=True)
    return x * lax.rsqrt(ms + EPS) * g


def _softplus(x):
    return jnp.maximum(x, 0.0) + jnp.log1p(jnp.exp(-jnp.abs(x)))


def _log_sigmoid(x):
    return jnp.minimum(x, 0.0) - jnp.log1p(jnp.exp(-jnp.abs(x)))


def _silu(x):
    return x * (1.0 / (1.0 + jnp.exp(-x)))


def _split3(x):
    hi = x.astype(BF16).astype(F32)
    r1 = x - hi
    mid = r1.astype(BF16).astype(F32)
    lo = (r1 - mid).astype(BF16).astype(F32)
    return hi, mid, lo


def _dot(a, b):
    return jnp.dot(a, b, preferred_element_type=F32)


def _dot_nt(a, b):
    return lax.dot_general(a, b, (((1,), (1,)), ((), ())), preferred_element_type=F32)


def _dot_tn(a, b):
    return lax.dot_general(a, b, (((0,), (0,)), ((), ())), preferred_element_type=F32)


def _cumsum_lanes(x, tri_upper):
    hi, mid, lo = _split3(x)
    parts = jnp.concatenate([hi, mid, lo], axis=0).astype(BF16)
    r = _dot(parts, tri_upper)
    return r[0:8] + r[8:16] + r[16:24]


def _inproj_kernel(x_ref, g_ref, wm_ref, wvt_ref, wst_ref, bf_ref, dtb_ref, dtbx_ref, bfx_ref,
                   tril_ref, sel_ref, ones_ref, hsel_ref,
                   qx_ref, kx_ref, k32_ref, vtb_ref, v32t_ref, z_ref, xbc_ref, dtx_ref,
                   logft_ref, dtt_ref, c2t_ref, kn_ref, carry_ref, carry_t_ref, carry_kn_ref):
    @pl.when(pl.program_id(1) == 0)
    def _():
        carry_ref[...] = jnp.zeros_like(carry_ref)
        carry_t_ref[...] = jnp.zeros_like(carry_t_ref)
        carry_kn_ref[...] = jnp.zeros_like(carry_kn_ref)

    h = _rms(x_ref[0], g_ref[...]).astype(BF16)
    f_rows = _dot(h, wm_ref[:, _F0:_MAIN_COLS])
    small = _dot_nt(wst_ref[...], h)
    q = (_dot(h, wm_ref[:, _Q0:_K0]) * (ATT_SCALE * LOG2E)).astype(BF16)

    lf2 = _log_sigmoid(f_rows + bfx_ref[...]) * LOG2E
    parts = jnp.concatenate(_split3(lf2), axis=1).astype(BF16)
    r = _dot(tril_ref[...], parts)
    k = _dot(h, wm_ref[:, _K0:_Z0])
    k32_ref[0] = k
    kb = k.astype(BF16)

    logft = _log_sigmoid(small[0:8] + bf_ref[...])
    logft_ref[0] = logft
    dtt_ref[0] = _softplus(small[8:16] + dtb_ref[...])
    hi, mid, lo = _split3(logft * LOG2E)
    r_t = _dot_nt(jnp.concatenate([hi, mid, lo], axis=0).astype(BF16), tril_ref[...])
    vt = _dot_nt(wvt_ref[...], h)
    v32t_ref[0] = vt
    vtb_ref[0] = vt.astype(BF16)

    c2 = r[:, 0:LANES] + r[:, LANES:2 * LANES] + r[:, 2 * LANES:3 * LANES] + carry_ref[0:1, :]
    bt = c2.shape[0]
    carry_ref[...] = jnp.broadcast_to(c2[bt - 1:bt, :], carry_ref.shape)
    csplit = jnp.concatenate(_split3(c2), axis=1).astype(BF16)
    ext = (_dot(csplit, sel_ref[...]) + ones_ref[...]).astype(BF16)
    z_ref[0] = _dot(h, wm_ref[:, _Z0:_XBC0])

    c2t = r_t[0:8] + r_t[8:16] + r_t[16:24] + carry_t_ref[:, 0:1]
    c2t_ref[0] = c2t
    carry_t_ref[...] = jnp.broadcast_to(c2t[:, c2t.shape[1] - 1:], carry_t_ref.shape)
    kf = kb.astype(F32)
    ksq_t = _dot_nt(hsel_ref[...], (kf * kf).astype(BF16))
    xbc_ref[0] = _dot(h, wm_ref[:, _XBC0:_DT0])
    kn = jnp.maximum(carry_kn_ref[...], jnp.max(ksq_t, axis=1, keepdims=True))
    carry_kn_ref[...] = kn
    kn_ref[0] = kn
    dtx_ref[0] = _softplus(_dot(h, wm_ref[:, _DT0:_F0]) + dtbx_ref[...])
    for p in range(D_ATT // LANES):
        pl_ = slice(p * LANES, (p + 1) * LANES)
        kx_ref[0, :, 2 * p * LANES:(2 * p + 1) * LANES] = kb[:, pl_]
        kx_ref[0, :, (2 * p + 1) * LANES:(2 * p + 2) * LANES] = ext[:, pl_]
        qx_ref[0, :, 2 * p * LANES:(2 * p + 1) * LANES] = q[:, pl_]
        qx_ref[0, :, (2 * p + 1) * LANES:(2 * p + 2) * LANES] = ext[:, D_ATT + p * LANES:D_ATT + (p + 1) * LANES]


def _attention_extras():
    sel = np.zeros((3 * LANES, 2 * D_ATT), np.float32)
    ones = np.zeros((1, 2 * D_ATT), np.float32)
    for p in range(D_ATT // LANES):
        ones[0, p * LANES + 6:p * LANES + 12] = 1.0
        for hh in range(2):
            for s in range(3):
                sel[s * LANES + 2 * p + hh, p * LANES + 3 * hh + s] = 1.0
                sel[s * LANES + 2 * p + hh, D_ATT + p * LANES + 6 + 3 * hh + s] = 1.0
    return jnp.asarray(sel, BF16), jnp.asarray(ones, F32)


def _in_projection(x, g, w_main, w_vt, w_small_t, b_forget, dt_bias, dt_bias_x, b_forget_x, bt):
    nb, l, _ = x.shape
    tril = jnp.tril(jnp.ones((bt, bt), F32)).astype(BF16)
    sel, ones = _attention_extras()
    tok = lambda w: pl.BlockSpec((1, bt, w), lambda b, j: (b, j, 0))
    tok_t = lambda r: pl.BlockSpec((1, r, bt), lambda b, j: (b, 0, j))
    const = lambda shape: pl.BlockSpec(shape, lambda b, j: (0,) * len(shape))
    out_shape = (
        jax.ShapeDtypeStruct((nb, l, 2 * D_ATT), BF16),
        jax.ShapeDtypeStruct((nb, l, 2 * D_ATT), BF16),
        jax.ShapeDtypeStruct((nb, l, D_ATT), F32),
        jax.ShapeDtypeStruct((nb, D_ATT, l), BF16),
        jax.ShapeDtypeStruct((nb, D_ATT, l), F32),
        jax.ShapeDtypeStruct((nb, l, D_SSM), F32),
        jax.ShapeDtypeStruct((nb, l, CONV_CH), F32),
        jax.ShapeDtypeStruct((nb, l, D_SSM), F32),
        jax.ShapeDtypeStruct((nb, H_ATT, l), F32),
        jax.ShapeDtypeStruct((nb, H_SSM, l), F32),
        jax.ShapeDtypeStruct((nb, H_ATT, l), F32),
        jax.ShapeDtypeStruct((nb, H_ATT, (l // bt) * LANES), F32),
    )
    out_specs = (tok(2 * D_ATT), tok(2 * D_ATT), tok(D_ATT), tok_t(D_ATT), tok_t(D_ATT), tok(D_SSM),
                 tok(CONV_CH), tok(D_SSM), tok_t(H_ATT), tok_t(H_SSM), tok_t(H_ATT),
                 pl.BlockSpec((1, H_ATT, LANES), lambda b, j: (b, 0, j)))
    head_sel = jnp.asarray(np.repeat(np.eye(H_ATT, dtype=np.float32), HD_ATT, axis=1), BF16)
    return pl.pallas_call(
        _inproj_kernel,
        out_shape=out_shape,
        grid=(nb, l // bt),
        in_specs=[tok(D_MODEL), const((1, D_MODEL)), const((D_MODEL, _MAIN_COLS)),
                  const((D_ATT, D_MODEL)), const((2 * H_ATT, D_MODEL)), const((H_ATT, 1)),
                  const((H_SSM, 1)), const((1, D_SSM)), const((1, LANES)), const((bt, bt)),
                  const((3 * LANES, 2 * D_ATT)), const((1, 2 * D_ATT)), const((H_ATT, D_ATT))],
        out_specs=out_specs,
        scratch_shapes=[pltpu.VMEM((8, LANES), F32), pltpu.VMEM((H_ATT, LANES), F32),
                        pltpu.VMEM((H_ATT, LANES), F32)],
        compiler_params=_cparams(2),
        name="in_projection",
    )(x, g, w_main, w_vt, w_small_t, b_forget, dt_bias, dt_bias_x, b_forget_x, tril, sel, ones, head_sel)


def _fox_prompt_kernel(qx_ref, kx_ref, vt_ref, c2t_ref, c2all_ref, kn_ref, o_ref, *, blk, kt, n_split):
    qi = pl.program_id(2)
    q = qx_ref[0, :, 0:LANES]
    cq = jnp.broadcast_to(qx_ref[0, 0:1, LANES:2 * LANES], (blk, LANES))
    lane = lax.broadcasted_iota(jnp.int32, (blk, LANES), 1)
    first = lane < HD_ATT
    zero = jnp.zeros_like(q)
    minus = jnp.full_like(q, -1.0)
    q_heads, bounds = [], []
    for hh in range(2):
        ext = jnp.where((lane >= 3 * hh) & (lane < 3 * hh + 3), minus,
                        jnp.where((lane >= 6 + 3 * hh) & (lane < 9 + 3 * hh), cq, zero))
        own = jnp.where(first, q, zero) if hh == 0 else jnp.where(first, zero, q)
        q_heads.append(jnp.concatenate([own, ext], axis=1))
        qsq = _dot_nt(jnp.ones((8, LANES), BF16), own * own)[0:1]
        c2q = c2t_ref[0, 0, hh:hh + 1, :]
        bounds.append(BOUND_SCALE * jnp.sqrt(qsq * kn_ref[0, 0, hh:hh + 1, 0:1])
                      + (c2q[:, 0:1] - c2q) + BOUND_MARGIN)

    def v_ones(hh, k0, n):
        vt = vt_ref[0, hh * HD_ATT:(hh + 1) * HD_ATT, pl.ds(k0, n)]
        return jnp.concatenate([vt, jnp.ones((ONES_ROWS, n), BF16)], axis=0)

    def causal(st, k0, n, q_off=0, q_n=blk):
        key = lax.broadcasted_iota(jnp.int32, (n, q_n), 0) + (k0 - qi * blk - q_off)
        qry = lax.broadcasted_iota(jnp.int32, (n, q_n), 1)
        return jnp.where(key <= qry, st, NEG_BIG)

    qw = blk // n_split
    chains = [(hh, c) for c in range(n_split) for hh in range(2)]
    q_parts = [q_heads[hh][c * qw:(c + 1) * qw, :] for hh, c in chains]
    b_parts = [bounds[hh][:, c * qw:(c + 1) * qw] for hh, c in chains]

    def fast_tile(s, carry, masked):
        k0 = pl.multiple_of(s * kt, kt)
        kj = kx_ref[0, pl.ds(k0, kt), :]
        st_next = _dot_nt(kj, q_parts[0])
        out = []
        for i, ((hh, c), (g, acc)) in enumerate(zip(chains, carry)):
            st = st_next
            if i + 1 < len(chains):
                st_next = _dot_nt(kj, q_parts[i + 1])
            if masked:
                st = causal(st, k0, kt, c * qw, qw)
            g = jnp.maximum(g, jnp.max(st.reshape(kt // 8, 8, qw), axis=0))
            p = jnp.exp2(st - b_parts[i]).astype(BF16)
            out.append((g, acc + _dot(v_ones(hh, k0, kt), p)))
        return tuple(out)

    n_tiles = (qi * blk + blk + kt - 1) // kt
    n_dead = functools.reduce(jnp.minimum, [
        jnp.sum((c2t_ref[0, 0, hh:hh + 1, 0:1] - c2all_ref[0, 0, hh:hh + 1, :] < -UNDERFLOW_LOG2).astype(F32))
        for hh in range(2)])
    first_tile = jnp.minimum(n_dead.astype(jnp.int32) // kt, n_tiles - 1)
    init = tuple((jnp.full((8, qw), NEG_BIG, F32), jnp.zeros((HD_ATT + ONES_ROWS, qw), F32))
                 for _ in chains)
    carry = lax.fori_loop(first_tile, n_tiles - 1, lambda s, c: fast_tile(s, c, False), init)
    done = fast_tile(n_tiles - 1, carry, True)
    by_head = [[done[chains.index((hh, c))] for c in range(n_split)] for hh in range(2)]
    o_t = jnp.concatenate(
        [jnp.concatenate([acc[0:HD_ATT] / acc[HD_ATT:HD_ATT + 1] for _, acc in by_head[hh]], axis=1)
         for hh in range(2)], axis=0)
    o_ref[0] = o_t.T.astype(o_ref.dtype)
    slack = functools.reduce(jnp.maximum, [jnp.max(b_parts[i] - jnp.max(done[i][0], axis=0, keepdims=True))
                                           for i in range(len(chains))])

    @pl.when(slack > SLACK_LIMIT)
    def _():
        def exact_tile(j, carry, masked):
            k0 = pl.multiple_of(j * blk, blk)
            kj = kx_ref[0, pl.ds(k0, blk), :]
            out = []
            for hh in range(2):
                m, acc = carry[hh]
                st = _dot_nt(kj, q_heads[hh])
                if masked:
                    st = causal(st, k0, blk)
                m_new = jnp.maximum(m, jnp.max(st, axis=0, keepdims=True))
                p = jnp.exp2(st - m_new).astype(BF16)
                out.append((m_new, jnp.exp2(m - m_new) * acc + _dot(v_ones(hh, k0, blk), p)))
            return tuple(out)

        init_x = tuple((jnp.full((1, blk), NEG_BIG, F32), jnp.zeros((HD_ATT + ONES_ROWS, blk), F32))
                       for _ in range(2))
        carry_x = lax.fori_loop(0, qi, lambda j, c: exact_tile(j, c, False), init_x)
        done_x = exact_tile(qi, carry_x, True)
        o_x = jnp.concatenate([acc[0:HD_ATT] / acc[HD_ATT:HD_ATT + 1] for _, acc in done_x], axis=0)
        o_ref[0] = o_x.T.astype(o_ref.dtype)


def _fox_prompt(qx, kx, vtb, c2t, kn, blk, kn_blk):
    nb, l, _ = qx.shape
    n_pairs = D_ATT // LANES
    kt = blk if blk >= 1024 or l % (2 * blk) else 2 * blk
    assert blk % kn_blk == 0 and l % kt == 0
    kn_per_blk = blk // kn_blk
    pair_rows = lambda a: a.reshape(nb, n_pairs, 2, a.shape[-1])
    return pl.pallas_call(
        functools.partial(_fox_prompt_kernel, blk=blk, kt=kt, n_split=1),
        out_shape=jax.ShapeDtypeStruct((nb, l, D_ATT), BF16),
        grid=(nb, n_pairs, l // blk),
        in_specs=[pl.BlockSpec((1, blk, 2 * LANES), lambda b, p, i: (b, i, p)),
                  pl.BlockSpec((1, l, 2 * LANES), lambda b, p, i: (b, 0, p)),
                  pl.BlockSpec((1, LANES, l), lambda b, p, i: (b, p, 0)),
                  pl.BlockSpec((1, 1, 2, blk), lambda b, p, i: (b, p, 0, i)),
                  pl.BlockSpec((1, 1, 2, l), lambda b, p, i: (b, p, 0, 0)),
                  pl.BlockSpec((1, 1, 2, LANES), lambda b, p, i: (b, p, 0, (i + 1) * kn_per_blk - 1))],
        out_specs=pl.BlockSpec((1, blk, LANES), lambda b, p, i: (b, i, p)),
        compiler_params=_cparams(3),
        name="fox_prompt",
    )(qx, kx, vtb, pair_rows(c2t), pair_rows(c2t), pair_rows(kn))


def _ssd_gate_norm(y, z, g):
    return _rms(y * _silu(z), g)


def _ssd_prompt_kernel(xbc_ref, z_ref, dtx_ref, dtt_ref, cw_ref, cb_ref, alx_ref, alc_ref,
                       dsx_ref, g_ref, tril_ref, triu_ref,
                       y_ref, st_ref, xp_ref, state_ref):
    q = SSD_CHUNK
    c = pl.program_id(1)

    @pl.when(c == 0)
    def _():
        xp_ref[...] = jnp.zeros_like(xp_ref)
        state_ref[...] = jnp.zeros_like(state_ref)

    xb = xbc_ref[0]
    prev = xp_ref[...]
    row8 = lax.broadcasted_iota(jnp.int32, (8, CONV_CH), 0)
    acc = cb_ref[...] + xb * cw_ref[CONV_WIDTH - 1:CONV_WIDTH, :]
    for back in range(1, CONV_WIDTH):
        sh = pltpu.roll(xb, back, 0)
        head = jnp.where(row8 < back, pltpu.roll(prev, back, 0), sh[0:8])
        shifted = jnp.concatenate([head, sh[8:]], axis=0)
        acc = acc + shifted * cw_ref[CONV_WIDTH - 1 - back:CONV_WIDTH - back, :]
    xp_ref[...] = xb[q - 8:q]
    act = _silu(acc)
    xs = act[:, :D_SSM]
    bmat = act[:, D_SSM:D_SSM + N_BC_GROUPS * N_STATE]
    cmat = act[:, D_SSM + N_BC_GROUPS * N_STATE:]

    dt = dtx_ref[0]
    a_x = -jnp.exp(alx_ref[...]) * dt
    hi, mid, lo = _split3(a_x)
    parts = jnp.concatenate([hi, mid, lo], axis=1).astype(BF16)
    r = _dot(tril_ref[...], parts)
    acs_x = r[:, 0:D_SSM] + r[:, D_SSM:2 * D_SSM] + r[:, 2 * D_SSM:3 * D_SSM]
    a_t = -jnp.exp(alc_ref[...]) * dtt_ref[0]
    acs_t = _cumsum_lanes(a_t, triu_ref[...])

    xdt = xs * dt
    a_last = acs_x[q - 1:q, :]
    xdt_end = (xdt * jnp.exp(a_last - acs_x)).astype(BF16)
    e_acs = jnp.exp(acs_x)
    xdt_b = xdt.astype(BF16)
    state_old = state_ref[...]
    state_b = state_old.astype(BF16)

    row = lax.broadcasted_iota(jnp.int32, (q, q), 0)
    col = lax.broadcasted_iota(jnp.int32, (q, q), 1)
    causal = row >= col
    first = lax.broadcasted_iota(jnp.int32, (q, LANES), 1) < P_SSM

    y_parts = []
    for g in range(N_BC_GROUPS):
        cg = cmat[:, g * N_STATE:(g + 1) * N_STATE].astype(BF16)
        bg = bmat[:, g * N_STATE:(g + 1) * N_STATE].astype(BF16)
        cb = _dot_nt(cg, bg)
        gl = slice(g * GROUP_W, (g + 1) * GROUP_W)
        y_off = _dot(cg, state_b[:, gl]) * e_acs[:, gl]
        for pr in range(GROUP_W // LANES):
            lo_lane = g * GROUP_W + pr * LANES
            pair = []
            for hh in range(2):
                h = lo_lane // P_SSM + hh
                seg = acs_x[:, h * P_SSM:h * P_SSM + 1] - acs_t[h:h + 1, :]
                decay = jnp.exp(jnp.where(causal, seg, -jnp.inf))
                gm = (cb * decay).astype(BF16)
                pair.append(_dot(gm, xdt_b[:, lo_lane:lo_lane + LANES]))
            y_parts.append(jnp.where(first, pair[0], pair[1]) + y_off[:, pr * LANES:(pr + 1) * LANES])
        state_ref[:, gl] = jnp.exp(a_last[:, gl]) * state_old[:, gl] + _dot_tn(bg, xdt_end[:, gl])
    y = jnp.concatenate(y_parts, axis=1) + dsx_ref[...] * xs
    y_ref[0] = _ssd_gate_norm(y, z_ref[0], g_ref[...]).astype(y_ref.dtype)

    @pl.when(c == pl.num_programs(1) - 1)
    def _():
        st_ref[0] = state_ref[...].T


def _ssd_prompt(xbc, z, dtx, dtt, conv_w, conv_b, a_log_x, a_log_c, d_skip_x, norm_g):
    nb, l, _ = xbc.shape
    q = SSD_CHUNK
    tril = jnp.tril(jnp.ones((q, q), F32)).astype(BF16)
    triu = jnp.triu(jnp.ones((q, q), F32)).astype(BF16)
    tok = lambda w: pl.BlockSpec((1, q, w), lambda b, c: (b, c, 0))
    const = lambda shape: pl.BlockSpec(shape, lambda b, c: (0,) * len(shape))
    return pl.pallas_call(
        _ssd_prompt_kernel,
        out_shape=(jax.ShapeDtypeStruct((nb, l, D_SSM), BF16),
                   jax.ShapeDtypeStruct((nb, D_SSM, N_STATE), F32)),
        grid=(nb, l // q),
        in_specs=[tok(CONV_CH), tok(D_SSM), tok(D_SSM),
                  pl.BlockSpec((1, H_SSM, q), lambda b, c: (b, 0, c)),
                  const((CONV_WIDTH, CONV_CH)), const((1, CONV_CH)), const((1, D_SSM)),
                  const((H_SSM, 1)), const((1, D_SSM)), const((1, D_SSM)),
                  const((q, q)), const((q, q))],
        out_specs=(tok(D_SSM), pl.BlockSpec((1, D_SSM, N_STATE), lambda b, c: (b, 0, 0))),
        scratch_shapes=[pltpu.VMEM((8, CONV_CH), F32), pltpu.VMEM((N_STATE, D_SSM), F32)],
        compiler_params=_cparams(2),
        name="ssd_prompt",
    )(xbc, z, dtx, dtt, conv_w, conv_b, a_log_x, a_log_c, d_skip_x, norm_g, tril, triu)


def _mix_out_kernel(x_ref, att_ref, ssm_ref, wo_ref, g_ref, wq_ref, x1_ref, qc_ref):
    x1 = x_ref[...] + _dot(att_ref[...], wo_ref[0:D_ATT, :]) + _dot(ssm_ref[...], wo_ref[D_ATT:, :])
    x1_ref[...] = x1
    h = _rms(x1, g_ref[...]).astype(BF16)
    qc_ref[...] = (_dot(h, wq_ref[...]) * X_SCALE).astype(BF16)


def _mix_out(x, att, ssm, w_out, g_cross, w_cq, bt):
    t = x.shape[0]
    tok = lambda w: pl.BlockSpec((bt, w), lambda i: (i, 0))
    const = lambda shape: pl.BlockSpec(shape, lambda i: (0,) * len(shape))
    return pl.pallas_call(
        _mix_out_kernel,
        out_shape=(jax.ShapeDtypeStruct((t, D_MODEL), F32), jax.ShapeDtypeStruct((t, D_MODEL), BF16)),
        grid=(t // bt,),
        in_specs=[tok(D_MODEL), tok(D_ATT), tok(D_SSM), const((D_MODEL, D_MODEL)),
                  const((1, D_MODEL)), const((D_MODEL, D_MODEL))],
        out_specs=(tok(D_MODEL), tok(D_MODEL)),
        compiler_params=_cparams(1),
        name="mix_out",
    )(x, att, ssm, w_out, g_cross, w_cq)


def _ffn_kernel(x1_ref, o_ref, wco_ref, g_ref, wg_ref, wu_ref, wd_ref, gf_ref, y_ref):
    x2 = x1_ref[...] + _dot(o_ref[...], wco_ref[...])
    h = _rms(x2, g_ref[...]).astype(BF16)
    u = (_silu(_dot(h, wg_ref[...])) * _dot(h, wu_ref[...])).astype(BF16)
    x3 = x2 + _dot(u, wd_ref[...])
    y_ref[...] = _rms(x3, gf_ref[...])


def _ffn(x1, o, w_co, g_ffn, w_gate, w_up, w_down, g_final, bt):
    t = x1.shape[0]
    d_ff = w_gate.shape[1]
    tok = lambda w: pl.BlockSpec((bt, w), lambda i: (i, 0))
    const = lambda shape: pl.BlockSpec(shape, lambda i: (0,) * len(shape),
                                       pipeline_mode=pl.Buffered(1))
    return pl.pallas_call(
        _ffn_kernel,
        out_shape=jax.ShapeDtypeStruct((t, D_MODEL), F32),
        grid=(t // bt,),
        in_specs=[tok(D_MODEL), tok(D_MODEL), const((D_MODEL, D_MODEL)), const((1, D_MODEL)),
                  const((D_MODEL, d_ff)), const((D_MODEL, d_ff)), const((d_ff, D_MODEL)),
                  const((1, D_MODEL))],
        out_specs=tok(D_MODEL),
        compiler_params=_cparams(1),
        name="cross_out_ffn",
    )(x1, o, w_co, g_ffn, w_gate, w_up, w_down, g_final)


def _memory_kv_kernel(mem_ref, g_ref, wk_ref, wv_ref, k32_ref, v32_ref, kb_ref, vb_ref):
    mn = _rms(mem_ref[0], g_ref[...]).astype(BF16)
    k = _dot(mn, wk_ref[...])
    v = _dot(mn, wv_ref[...])
    k32_ref[0] = k
    v32_ref[0] = v
    kb_ref[0] = k.astype(BF16)
    vb_ref[0] = v.astype(BF16)


def _memory_kv(mem, g, w_ck, w_cv):
    nb, m, _ = mem.shape
    blk = pl.BlockSpec((1, m, D_MODEL), lambda b: (b, 0, 0))
    const = lambda shape: pl.BlockSpec(shape, lambda b: (0,) * len(shape))
    f = jax.ShapeDtypeStruct((nb, m, D_MODEL), F32)
    h = jax.ShapeDtypeStruct((nb, m, D_MODEL), BF16)
    return pl.pallas_call(
        _memory_kv_kernel,
        out_shape=(f, f, h, h),
        grid=(nb,),
        in_specs=[blk, const((1, D_MODEL)), const((D_MODEL, D_MODEL)), const((D_MODEL, D_MODEL))],
        out_specs=(blk, blk, blk, blk),
        compiler_params=_cparams(1),
        name="memory_kv",
    )(mem, g, w_ck, w_cv)


def _softmax_rows(s):
    m = jnp.max(s, axis=-1, keepdims=True)
    p = jnp.exp(s - m)
    return p / jnp.sum(p, axis=-1, keepdims=True)


def _cross_prompt_kernel(q_ref, k_ref, v_ref, o_ref):
    for h in range(H_X):
        hl = slice(h * HD_X, (h + 1) * HD_X)
        p = _softmax_rows(_dot_nt(q_ref[0, :, hl], k_ref[0, :, hl]))
        o_ref[0, :, hl] = _dot(p.astype(BF16), v_ref[0, :, hl]).astype(o_ref.dtype)


def _cross_prompt(qc, mk, mv, bt):
    nb, l, _ = qc.shape
    m = mk.shape[1]
    tok = pl.BlockSpec((1, bt, D_MODEL), lambda b, i: (b, i, 0))
    mem = pl.BlockSpec((1, m, D_MODEL), lambda b, i: (b, 0, 0))
    return pl.pallas_call(
        _cross_prompt_kernel,
        out_shape=jax.ShapeDtypeStruct((nb, l, D_MODEL), BF16),
        grid=(nb, l // bt),
        in_specs=[tok, mem, mem],
        out_specs=tok,
        compiler_params=_cparams(2),
        name="cross_prompt",
    )(qc, mk, mv)


_X_SUB = 2 * H_X


def _cross_sample_kernel(q_ref, k_ref, v_ref, o_ref, *, lq):
    nsb, n_mem = k_ref.shape[0], k_ref.shape[1]
    n = n_mem * _X_SUB
    rows = lq * _X_SUB
    n_tiles = n // LANES
    cls = lax.broadcasted_iota(jnp.int32, (rows, LANES), 1) % _X_SUB
    own = cls == lax.broadcasted_iota(jnp.int32, (rows, LANES), 0) % _X_SUB
    valid = cls < H_X

    class_steps = (8, 16, 32, 64)

    def per_token(x):
        x3 = x.reshape(lq, _X_SUB, LANES)
        return jnp.broadcast_to(jnp.sum(x3, axis=1, keepdims=True), x3.shape).reshape(rows, LANES)

    seqs = range(nsb)
    s5 = [_dot_nt(q_ref[i], k_ref[i].reshape(n, LANES).astype(BF16)) for i in seqs]
    s_t = [[per_token(jnp.where(own, s5[i][:, j * LANES:(j + 1) * LANES], 0.0)) for j in range(n_tiles)]
           for i in seqs]
    s_t = [[u + pltpu.roll(u, LANES - H_X, 1) for u in s_t[i]] for i in seqs]
    m = [functools.reduce(jnp.maximum, s_t[i]) for i in seqs]
    for sh in class_steps:
        m = [jnp.maximum(x, pltpu.roll(x, sh, 1)) for x in m]
    p_t = [[jnp.where(valid, jnp.exp(s - m[i]), 0.0) for s in s_t[i]] for i in seqs]
    l = [functools.reduce(jnp.add, p_t[i]) for i in seqs]
    for sh in class_steps:
        l = [x + pltpu.roll(x, sh, 1) for x in l]
    inv = [1.0 / jnp.where(valid, l[i], 1.0) for i in seqs]
    pn = [[p * inv[i] for p in p_t[i]] for i in seqs]
    p5 = [jnp.concatenate([jnp.where(own, x + pltpu.roll(x, H_X, 1), 0.0).astype(BF16) for x in pn[i]], axis=1)
          for i in seqs]
    for i in seqs:
        v5 = v_ref[i].reshape(n, LANES).astype(BF16)
        o_ref[i] = _dot(p5[i], v5).astype(o_ref.dtype)


def _cross_sample(q5, mem_k, mem_v, lq, n_seq_blk):
    ns, rows, _ = q5.shape
    m = mem_k.shape[1]
    tok = pl.BlockSpec((n_seq_blk, rows, LANES), lambda i: (i, 0, 0))
    mem = pl.BlockSpec((n_seq_blk, m, _X_SUB, LANES), lambda i: (i, 0, 0, 0))
    return pl.pallas_call(
        functools.partial(_cross_sample_kernel, lq=lq),
        out_shape=jax.ShapeDtypeStruct((ns, rows, LANES), BF16),
        grid=(ns // n_seq_blk,),
        in_specs=[tok, mem, mem],
        out_specs=tok,
        compiler_params=_cparams(1),
        name="cross_sample",
    )(q5, mem_k, mem_v)


def _fox_sample_kernel(pt_ref, q_ref, kn_ref, vn_ref, lfn_ref, tril_ref, *rest, n_pages, lq):
    del pt_ref
    k_refs = rest[0:n_pages]
    v_refs = rest[n_pages:2 * n_pages]
    lf_refs = rest[2 * n_pages:3 * n_pages]
    o_ref = rest[3 * n_pages]
    kt_ref = rest[3 * n_pages + 1]
    vt_ref = rest[3 * n_pages + 2]

    for j in range(n_pages):
        kt_ref[:, :, j * PAGE_SIZE:(j + 1) * PAGE_SIZE] = k_refs[j][0].astype(BF16)
        vt_ref[:, :, j * PAGE_SIZE:(j + 1) * PAGE_SIZE] = v_refs[j][0].astype(BF16)

    lf_all = jnp.concatenate([r[0] for r in lf_refs], axis=0) * LOG2E
    hi, mid, lo = _split3(lf_all)
    tril = tril_ref[...]
    incl = _dot(hi.astype(BF16), tril) + _dot(mid.astype(BF16), tril) + _dot(lo.astype(BF16), tril)
    after = jnp.zeros((H_ATT, 1), F32)
    pieces = [None] * n_pages
    for j in reversed(range(n_pages)):
        sl = slice(j * H_ATT, (j + 1) * H_ATT)
        pieces[j] = incl[sl] - lf_all[sl] + after
        after = after + incl[sl][:, 0:1]
    bias = jnp.concatenate(pieces, axis=1)

    lfn = lfn_ref[0] * LOG2E
    lane = lax.broadcasted_iota(jnp.int32, (H_ATT, lq), 1)
    cn = jnp.zeros((H_ATT, lq), F32)
    for t in range(lq):
        cn = cn + jnp.where(lane >= t, lfn[:, t:t + 1], 0.0)
    causal = (lax.broadcasted_iota(jnp.int32, (lq, lq), 1)
              <= lax.broadcasted_iota(jnp.int32, (lq, lq), 0))

    scores = [(_dot(q_ref[0, h], kt_ref[h]), _dot_nt(q_ref[0, h], kn_ref[0, h])) for h in range(H_ATT)]
    probs = []
    for h, (s, s_new) in enumerate(scores):
        s = s + bias[h:h + 1, :]
        s_new = jnp.where(causal, s_new - cn[h:h + 1, :], NEG_BIG)
        m = jnp.maximum(jnp.max(s, axis=-1, keepdims=True), jnp.max(s_new, axis=-1, keepdims=True))
        p = jnp.exp2(s - m)
        p_new = jnp.exp2(s_new - m)
        l = jnp.sum(p, axis=-1, keepdims=True) + jnp.sum(p_new, axis=-1, keepdims=True)
        probs.append((p.astype(BF16), p_new.astype(BF16), l))
    for h, (p, p_new, l) in enumerate(probs):
        o = _dot_nt(p, vt_ref[h]) + _dot(p_new, vn_ref[0, h])
        o_ref[0, h] = (o / l).astype(o_ref.dtype)


def _fox_sample(qh, kh, vh, lfn, cache_kt, cache_vt, cache_lft, page_table):
    ns, _, lq, _ = qh.shape
    n_pages = page_table.shape[1]
    past = n_pages * PAGE_SIZE
    tril = jnp.tril(jnp.ones((PAGE_SIZE, PAGE_SIZE), F32)).astype(BF16)
    tok = pl.BlockSpec((1, H_ATT, lq, HD_ATT), lambda i, pt: (i, 0, 0, 0))

    def page(j, shape):
        return pl.BlockSpec((1,) + shape, lambda i, pt: (pt[i * n_pages + j],) + (0,) * len(shape))

    in_specs = [tok, tok, tok, pl.BlockSpec((1, H_ATT, lq), lambda i, pt: (i, 0, 0)),
                pl.BlockSpec((PAGE_SIZE, PAGE_SIZE), lambda i, pt: (0, 0))]
    in_specs += [page(j, (H_ATT, HD_ATT, PAGE_SIZE)) for j in range(n_pages)]
    in_specs += [page(j, (H_ATT, HD_ATT, PAGE_SIZE)) for j in range(n_pages)]
    in_specs += [page(j, (H_ATT, PAGE_SIZE)) for j in range(n_pages)]
    grid_spec = pltpu.PrefetchScalarGridSpec(
        num_scalar_prefetch=1,
        grid=(ns,),
        in_specs=in_specs,
        out_specs=tok,
        scratch_shapes=[pltpu.VMEM((H_ATT, HD_ATT, past), BF16), pltpu.VMEM((H_ATT, HD_ATT, past), BF16)],
    )
    return pl.pallas_call(
        functools.partial(_fox_sample_kernel, n_pages=n_pages, lq=lq),
        out_shape=jax.ShapeDtypeStruct((ns, H_ATT, lq, HD_ATT), BF16),
        grid_spec=grid_spec,
        compiler_params=_cparams(1),
        name="fox_sample",
    )(page_table.reshape(-1), qh, kh, vh, lfn, tril,
      *([cache_kt] * n_pages), *([cache_vt] * n_pages), *([cache_lft] * n_pages))


def _ssd_sample_kernel(xbc_ref, z_ref, dtx_ref, cs_ref, st_ref, cw_ref, cb_ref, alx_ref, dsx_ref,
                       g_ref, y_ref, sto_ref, xp_ref, rows_ref, brow_ref, *, lq):
    nsb = xbc_ref.shape[0]
    assert nsb * (lq + 1) <= LANES
    kw = CONV_WIDTH - 1
    xp_ref[:, 0:kw, :] = cs_ref[...]
    xp_ref[:, kw:kw + lq, :] = xbc_ref[...]
    acc = cb_ref[...]
    for tap in range(CONV_WIDTH):
        acc = acc + xp_ref[:, tap:tap + lq, :] * cw_ref[tap:tap + 1, :]
    act = _silu(acc)
    xs = act[:, :, :D_SSM]
    bmat = act[:, :, D_SSM:D_SSM + N_BC_GROUPS * N_STATE]
    cmat = act[:, :, D_SSM + N_BC_GROUPS * N_STATE:]
    dt = dtx_ref[...]
    a = -jnp.exp(alx_ref[...]) * dt
    xdt = xs * dt
    acs = [a[:, 0:1, :]]
    for t in range(1, lq):
        acs.append(acs[-1] + a[:, t:t + 1, :])
    a_last = acs[lq - 1]

    ccat = jnp.concatenate([cmat[:, :, g * N_STATE:(g + 1) * N_STATE] for g in range(N_BC_GROUPS)],
                           axis=1).astype(BF16)
    state = st_ref[...]
    r = jnp.einsum("sgn,shn->sgh", ccat, state.astype(BF16), preferred_element_type=F32)
    group0 = lax.broadcasted_iota(jnp.int32, (1, 1, D_SSM), 2) < GROUP_W
    y_off = jnp.where(group0, r[:, 0:lq, :], r[:, lq:2 * lq, :])

    ys = []
    for t in range(lq):
        y_t = jnp.exp(acs[t]) * y_off[:, t:t + 1, :] + dsx_ref[...] * xs[:, t:t + 1, :]
        for s in range(t + 1):
            cb = jnp.concatenate(
                [jnp.broadcast_to(
                    jnp.sum(cmat[:, t:t + 1, g * N_STATE:(g + 1) * N_STATE]
                            * bmat[:, s:s + 1, g * N_STATE:(g + 1) * N_STATE], axis=-1, keepdims=True),
                    (nsb, 1, GROUP_W)) for g in range(N_BC_GROUPS)], axis=2)
            y_t = y_t + cb * jnp.exp(acs[t] - acs[s]) * xdt[:, s:s + 1, :]
        ys.append(y_t)
    y = jnp.concatenate(ys, axis=1)
    y_ref[...] = _ssd_gate_norm(y, z_ref[...], g_ref[...]).astype(y_ref.dtype)

    upd = jnp.concatenate([xdt[:, s:s + 1, :] * jnp.exp(a_last - acs[s]) for s in range(lq)], axis=1)
    e_last = jnp.exp(a_last)
    rows_ref[...] = jnp.zeros_like(rows_ref)
    brow_ref[...] = jnp.zeros_like(brow_ref)
    for i in range(nsb):
        rows_ref[i * lq:(i + 1) * lq, :] = upd[i]
        rows_ref[nsb * lq + i:nsb * lq + i + 1, :] = e_last[i]
        brow_ref[i * lq:(i + 1) * lq, :] = bmat[i]
    cols = rows_ref[...].T
    upd_cols = cols.astype(BF16)
    b_all = brow_ref[...]
    row_seq = lax.broadcasted_iota(jnp.int32, (LANES, N_BC_GROUPS * N_STATE), 0) // lq
    top = lax.broadcasted_iota(jnp.int32, (D_SSM, N_STATE), 0) < GROUP_W
    for i in range(nsb):
        b_rows = jnp.where(row_seq == i, b_all, 0.0).astype(BF16)
        m = _dot(upd_cols, b_rows)
        add = jnp.where(top, m[:, 0:N_STATE], m[:, N_STATE:2 * N_STATE])
        decay = jnp.broadcast_to(cols[:, nsb * lq + i:nsb * lq + i + 1], (D_SSM, N_STATE))
        sto_ref[i] = decay * state[i] + add


def _ssd_sample(xbc, z, dtx, conv_state, state, conv_w, conv_b, a_log_x, d_skip_x, norm_g, nsb):
    ns, lq, _ = xbc.shape
    seq = lambda r, w: pl.BlockSpec((nsb, r, w), lambda i: (i, 0, 0))
    const = lambda shape: pl.BlockSpec(shape, lambda i: (0,) * len(shape))
    return pl.pallas_call(
        functools.partial(_ssd_sample_kernel, lq=lq),
        out_shape=(jax.ShapeDtypeStruct((ns, lq, D_SSM), BF16),
                   jax.ShapeDtypeStruct((ns, D_SSM, N_STATE), F32)),
        grid=(ns // nsb,),
        in_specs=[seq(lq, CONV_CH), seq(lq, D_SSM), seq(lq, D_SSM), seq(CONV_WIDTH - 1, CONV_CH),
                  seq(D_SSM, N_STATE), const((CONV_WIDTH, CONV_CH)), const((1, CONV_CH)),
                  const((1, D_SSM)), const((1, D_SSM)), const((1, D_SSM))],
        out_specs=(seq(lq, D_SSM), seq(D_SSM, N_STATE)),
        scratch_shapes=[pltpu.VMEM((nsb, 8, CONV_CH), F32), pltpu.VMEM((LANES, D_SSM), F32),
                        pltpu.VMEM((LANES, N_BC_GROUPS * N_STATE), F32)],
        compiler_params=_cparams(1),
        name="ssd_sample",
    )(xbc, z, dtx, conv_state, state, conv_w, conv_b, a_log_x, d_skip_x, norm_g)


def _pick(pref, n):
    return pref if n % pref == 0 else n


def kernel(x_prompt, x_sample, mem_prompt, cache_k, cache_v, cache_logf, page_table, cache_mem_k, cache_mem_v, state_conv, state_ssm, norm_mix_g, w_in, b_forget, conv_w, conv_b, dt_bias, a_log, d_skip, ssm_norm_g, w_out, norm_cross_g, norm_mem_g, w_cq, w_ck, w_cv, w_co, norm_ffn_g, w_gate, w_up, w_down, final_norm_g):
    assert w_in.shape[0] == 1, "one layer"
    nb, l, _ = x_prompt.shape
    ns, lq, _ = x_sample.shape
    n_phys = cache_k.shape[1]
    row = lambda v: v.reshape(1, -1).astype(F32)
    colv = lambda v: v.reshape(-1, 1).astype(F32)
    per_ch = lambda v: jnp.repeat(v.astype(F32), P_SSM).reshape(1, D_SSM)

    w = w_in[0]
    cuts = [D_ATT, 2 * D_ATT, 3 * D_ATT, 3 * D_ATT + H_ATT, 3 * D_ATT + H_ATT + D_SSM,
            3 * D_ATT + H_ATT + D_SSM + CONV_CH]
    w_q, w_k, w_v, w_f, w_z, w_xbc, w_dt = jnp.split(w, cuts, axis=1)
    lane_pad = lambda a: jnp.pad(a, ((0, 0), (0, LANES - a.shape[1])))
    w_main = jnp.concatenate([w_q, w_k, w_z, w_xbc, jnp.repeat(w_dt, P_SSM, axis=1), lane_pad(w_f)],
                             axis=1).astype(BF16)
    w_small_t = jnp.concatenate([w_f, w_dt], axis=1).T.astype(BF16)
    in_params = (row(norm_mix_g[0]), w_main, w_v.T.astype(BF16), w_small_t, colv(b_forget[0]),
                 colv(dt_bias[0]), per_ch(dt_bias[0]), lane_pad(row(b_forget[0])))
    ssd_params = (conv_w[0], row(conv_b[0]), per_ch(a_log[0]))
    ssd_tail = (per_ch(d_skip[0]), row(ssm_norm_g[0]))
    w_out_b, w_cq_b, w_co_b = w_out[0].astype(BF16), w_cq[0].astype(BF16), w_co[0].astype(BF16)
    w_gate_b, w_up_b, w_down_b = w_gate[0].astype(BF16), w_up[0].astype(BF16), w_down[0].astype(BF16)
    g_cross, g_ffn, g_final = row(norm_cross_g[0]), row(norm_ffn_g[0]), row(final_norm_g)

    def tail(x, att, ssm, cross, bt):
        x1, qc = _mix_out(x, att, ssm, w_out_b, g_cross, w_cq_b, bt)
        o = cross(qc)
        return _ffn(x1, o, w_co_b, g_ffn, w_gate_b, w_up_b, w_down_b, g_final, bt)

    bt = _pick(512, l)
    qx, kx, k32, vtb, v32t, z, xbc, dtx, logft, dtt, c2t, kn = _in_projection(x_prompt, *in_params, bt)
    att = _fox_prompt(qx, kx, vtb, c2t, kn, _pick(1024, l), bt)
    ssm, st_p = _ssd_prompt(xbc, z, dtx, dtt, *ssd_params, colv(a_log[0]), *ssd_tail)
    mk32, mv32, mkb, mvb = _memory_kv(mem_prompt, row(norm_mem_g[0]), w_ck[0].astype(BF16),
                                      w_cv[0].astype(BF16))
    cross_p = lambda qc: _cross_prompt(qc.reshape(nb, l, D_MODEL), mkb, mvb, bt).reshape(nb * l, D_MODEL)
    y_prompt = tail(x_prompt.reshape(nb * l, D_MODEL), att.reshape(nb * l, D_ATT),
                    ssm.reshape(nb * l, D_SSM), cross_p, bt).reshape(nb, l, D_MODEL)

    ts = ns * lq
    bts = _pick(512, ts)
    qx_s, _, k32_s, vtb_s, v32t_s, z_s, xbc_s, dtx_s, logft_s, _, _, _ = _in_projection(
        x_sample.reshape(1, ts, D_MODEL), *in_params, bts)
    seq3 = lambda a: a.reshape(ns, lq, a.shape[-1])
    lfn = logft_s.reshape(H_ATT, ns, lq).transpose(1, 0, 2)
    head_major = lambda a: a.reshape(ns, lq, H_ATT, HD_ATT).transpose(0, 2, 1, 3)
    from_t = lambda a: a.reshape(H_ATT, HD_ATT, ns, lq)
    qb_s = qx_s.reshape(ts, D_ATT // LANES, 2, LANES)[:, :, 0, :]
    att_s = _fox_sample(head_major(qb_s), head_major(k32_s.astype(BF16)),
                        from_t(vtb_s).transpose(2, 0, 3, 1), lfn,
                        cache_k[0].transpose(0, 2, 3, 1), cache_v[0].transpose(0, 2, 3, 1),
                        cache_logf[0].transpose(0, 2, 1), page_table)
    att_s = att_s.transpose(0, 2, 1, 3)
    ssm_s, st_s = _ssd_sample(seq3(xbc_s), seq3(z_s), seq3(dtx_s), state_conv[0],
                              state_ssm[0].reshape(ns, D_SSM, N_STATE), *ssd_params, *ssd_tail,
                              _pick(8, ns))
    n_mem = mem_prompt.shape[1]
    stored = lambda a: a.reshape(ns, -1, H_X, 2, LANES).transpose(0, 1, 3, 2, 4).reshape(ns, -1, _X_SUB, LANES)
    cross_s = lambda qc: _cross_sample(
        stored(qc).reshape(ns, lq * _X_SUB, LANES), stored(cache_mem_k[0]), stored(cache_mem_v[0]), lq,
        _pick(4, ns)).reshape(ns, lq, 2, H_X, LANES).transpose(0, 1, 3, 2, 4).reshape(ts, D_MODEL)
    y_sample = tail(x_sample.reshape(ts, D_MODEL), att_s.reshape(ts, D_ATT), ssm_s.reshape(ts, D_SSM),
                    cross_s, bts).reshape(ns, lq, D_MODEL)

    heads = lambda a, n: a.reshape(1, n, -1, H_ATT, HD_ATT)
    kw = CONV_WIDTH - 1
    return (y_prompt, y_sample,
            heads(k32, nb), v32t.reshape(nb, H_ATT, HD_ATT, l).transpose(0, 3, 1, 2)[None],
            logft.transpose(0, 2, 1)[None],
            mk32.reshape(1, nb, n_mem, H_X, HD_X), mv32.reshape(1, nb, n_mem, H_X, HD_X),
            xbc[:, l - kw:, :][None], st_p.reshape(1, nb, H_SSM, P_SSM, N_STATE),
            heads(k32_s, ns), from_t(v32t_s).transpose(2, 3, 0, 1)[None],
            logft_s.reshape(H_ATT, ns, lq).transpose(1, 2, 0)[None],
            seq3(xbc_s)[:, lq - kw:, :][None], st_s.reshape(1, ns, H_SSM, P_SSM, N_STATE))
```

```python
import functools
import math

import numpy as np
import jax
import jax.numpy as jnp
from jax import lax
from jax.experimental import pallas as pl
from jax.experimental.pallas import tpu as pltpu

F32 = jnp.float32
BF16 = jnp.bfloat16

D_MODEL = 1024
D_ATT = 512
H_ATT = 8
HD_ATT = 64
D_SSM = 512
H_SSM = 8
P_SSM = 64
N_STATE = 128
N_BC_GROUPS = 2
GROUP_W = D_SSM // N_BC_GROUPS
CONV_WIDTH = 4
CONV_CH = D_SSM + 2 * N_BC_GROUPS * N_STATE
H_X = 4
HD_X = 256
EPS = 1e-6
ATT_SCALE = HD_ATT ** -0.5
X_SCALE = HD_X ** -0.5
SSD_CHUNK = 128
PAGE_SIZE = 128
NEG_BIG = -1e30
LOG2E = math.log2(math.e)

BOUND_SCALE = 1.03
BOUND_MARGIN = 2.0
SLACK_LIMIT = 80.0
UNDERFLOW_LOG2 = 154.0

LANES = 128
ONES_ROWS = 16
VMEM_LIMIT = 56 * 1024 * 1024

_Q0, _K0, _Z0, _XBC0, _DT0, _F0, _MAIN_COLS = 0, 512, 1024, 1536, 2560, 3072, 3200


def _cparams(n_axes):
    return pltpu.CompilerParams(dimension_semantics=("arbitrary",) * n_axes,
                                vmem_limit_bytes=VMEM_LIMIT)


def _rms(x, g):
    ms = jnp.mean(x * x, axis=-1, keepdims=True)
    return x * lax.rsqrt(ms + EPS) * g


def _softplus(x):
    return jnp.maximum(x, 0.0) + jnp.log1p(jnp.exp(-jnp.abs(x)))


def _log_sigmoid(x):
    return jnp.minimum(x, 0.0) - jnp.log1p(jnp.exp(-jnp.abs(x)))


def _silu(x):
    return x * (1.0 / (1.0 + jnp.exp(-x)))


def _split3(x):
    hi = x.astype(BF16).astype(F32)
    r1 = x - hi
    mid = r1.astype(BF16).astype(F32)
    lo = (r1 - mid).astype(BF16).astype(F32)
    return hi, mid, lo


def _dot(a, b):
    return jnp.dot(a, b, preferred_element_type=F32)


def _dot_nt(a, b):
    return lax.dot_general(a, b, (((1,), (1,)), ((), ())), preferred_element_type=F32)


def _dot_tn(a, b):
    return lax.dot_general(a, b, (((0,), (0,)), ((), ())), preferred_element_type=F32)


def _cumsum_lanes(x, tri_upper):
    hi, mid, lo = _split3(x)
    parts = jnp.concatenate([hi, mid, lo], axis=0).astype(BF16)
    r = _dot(parts, tri_upper)
    return r[0:8] + r[8:16] + r[16:24]


def _inproj_kernel(x_ref, g_ref, wm_ref, wvt_ref, wst_ref, bf_ref, dtb_ref, dtbx_ref, bfx_ref,
                   tril_ref, sel_ref, ones_ref, hsel_ref,
                   qx_ref, kx_ref, k32_ref, vtb_ref, v32t_ref, z_ref, xbc_ref, dtx_ref,
                   logft_ref, dtt_ref, c2t_ref, kn_ref, carry_ref, carry_t_ref, carry_kn_ref):
    @pl.when(pl.program_id(1) == 0)
    def _():
        carry_ref[...] = jnp.zeros_like(carry_ref)
        carry_t_ref[...] = jnp.zeros_like(carry_t_ref)
        carry_kn_ref[...] = jnp.zeros_like(carry_kn_ref)

    h = _rms(x_ref[0], g_ref[...]).astype(BF16)
    f_rows = _dot(h, wm_ref[:, _F0:_MAIN_COLS])
    small = _dot_nt(wst_ref[...], h)
    q = (_dot(h, wm_ref[:, _Q0:_K0]) * (ATT_SCALE * LOG2E)).astype(BF16)

    lf2 = _log_sigmoid(f_rows + bfx_ref[...]) * LOG2E
    parts = jnp.concatenate(_split3(lf2), axis=1).astype(BF16)
    r = _dot(tril_ref[...], parts)
    k = _dot(h, wm_ref[:, _K0:_Z0])
    k32_ref[0] = k
    kb = k.astype(BF16)

    logft = _log_sigmoid(small[0:8] + bf_ref[...])
    logft_ref[0] = logft
    dtt_ref[0] = _softplus(small[8:16] + dtb_ref[...])
    hi, mid, lo = _split3(logft * LOG2E)
    r_t = _dot_nt(jnp.concatenate([hi, mid, lo], axis=0).astype(BF16), tril_ref[...])
    vt = _dot_nt(wvt_ref[...], h)
    v32t_ref[0] = vt
    vtb_ref[0] = vt.astype(BF16)

    c2 = r[:, 0:LANES] + r[:, LANES:2 * LANES] + r[:, 2 * LANES:3 * LANES] + carry_ref[0:1, :]
    bt = c2.shape[0]
    carry_ref[...] = jnp.broadcast_to(c2[bt - 1:bt, :], carry_ref.shape)
    csplit = jnp.concatenate(_split3(c2), axis=1).astype(BF16)
    ext = (_dot(csplit, sel_ref[...]) + ones_ref[...]).astype(BF16)
    z_ref[0] = _dot(h, wm_ref[:, _Z0:_XBC0])

    c2t = r_t[0:8] + r_t[8:16] + r_t[16:24] + carry_t_ref[:, 0:1]
    c2t_ref[0] = c2t
    carry_t_ref[...] = jnp.broadcast_to(c2t[:, c2t.shape[1] - 1:], carry_t_ref.shape)
    kf = kb.astype(F32)
    ksq_t = _dot_nt(hsel_ref[...], (kf * kf).astype(BF16))
    xbc_ref[0] = _dot(h, wm_ref[:, _XBC0:_DT0])
    kn = jnp.maximum(carry_kn_ref[...], jnp.max(ksq_t, axis=1, keepdims=True))
    carry_kn_ref[...] = kn
    kn_ref[0] = kn
    dtx_ref[0] = _softplus(_dot(h, wm_ref[:, _DT0:_F0]) + dtbx_ref[...])
    for p in range(D_ATT // LANES):
        pl_ = slice(p * LANES, (p + 1) * LANES)
        kx_ref[0, :, 2 * p * LANES:(2 * p + 1) * LANES] = kb[:, pl_]
        kx_ref[0, :, (2 * p + 1) * LANES:(2 * p + 2) * LANES] = ext[:, pl_]
        qx_ref[0, :, 2 * p * LANES:(2 * p + 1) * LANES] = q[:, pl_]
        qx_ref[0, :, (2 * p + 1) * LANES:(2 * p + 2) * LANES] = ext[:, D_ATT + p * LANES:D_ATT + (p + 1) * LANES]


def _attention_extras():
    sel = np.zeros((3 * LANES, 2 * D_ATT), np.float32)
    ones = np.zeros((1, 2 * D_ATT), np.float32)
    for p in range(D_ATT // LANES):
        ones[0, p * LANES + 6:p * LANES + 12] = 1.0
        for hh in range(2):
            for s in range(3):
                sel[s * LANES + 2 * p + hh, p * LANES + 3 * hh + s] = 1.0
                sel[s * LANES + 2 * p + hh, D_ATT + p * LANES + 6 + 3 * hh + s] = 1.0
    return jnp.asarray(sel, BF16), jnp.asarray(ones, F32)


def _in_projection(x, g, w_main, w_vt, w_small_t, b_forget, dt_bias, dt_bias_x, b_forget_x, bt):
    nb, l, _ = x.shape
    tril = jnp.tril(jnp.ones((bt, bt), F32)).astype(BF16)
    sel, ones = _attention_extras()
    tok = lambda w: pl.BlockSpec((1, bt, w), lambda b, j: (b, j, 0))
    tok_t = lambda r: pl.BlockSpec((1, r, bt), lambda b, j: (b, 0, j))
    const = lambda shape: pl.BlockSpec(shape, lambda b, j: (0,) * len(shape))
    out_shape = (
        jax.ShapeDtypeStruct((nb, l, 2 * D_ATT), BF16),
        jax.ShapeDtypeStruct((nb, l, 2 * D_ATT), BF16),
        jax.ShapeDtypeStruct((nb, l, D_ATT), F32),
        jax.ShapeDtypeStruct((nb, D_ATT, l), BF16),
        jax.ShapeDtypeStruct((nb, D_ATT, l), F32),
        jax.ShapeDtypeStruct((nb, l, D_SSM), F32),
        jax.ShapeDtypeStruct((nb, l, CONV_CH), F32),
        jax.ShapeDtypeStruct((nb, l, D_SSM), F32),
        jax.ShapeDtypeStruct((nb, H_ATT, l), F32),
        jax.ShapeDtypeStruct((nb, H_SSM, l), F32),
        jax.ShapeDtypeStruct((nb, H_ATT, l), F32),
        jax.ShapeDtypeStruct((nb, H_ATT, (l // bt) * LANES), F32),
    )
    out_specs = (tok(2 * D_ATT), tok(2 * D_ATT), tok(D_ATT), tok_t(D_ATT), tok_t(D_ATT), tok(D_SSM),
                 tok(CONV_CH), tok(D_SSM), tok_t(H_ATT), tok_t(H_SSM), tok_t(H_ATT),
                 pl.BlockSpec((1, H_ATT, LANES), lambda b, j: (b, 0, j)))
    head_sel = jnp.asarray(np.repeat(np.eye(H_ATT, dtype=np.float32), HD_ATT, axis=1), BF16)
    return pl.pallas_call(
        _inproj_kernel,
        out_shape=out_shape,
        grid=(nb, l // bt),
        in_specs=[tok(D_MODEL), const((1, D_MODEL)), const((D_MODEL, _MAIN_COLS)),
                  const((D_ATT, D_MODEL)), const((2 * H_ATT, D_MODEL)), const((H_ATT, 1)),
                  const((H_SSM, 1)), const((1, D_SSM)), const((1, LANES)), const((bt, bt)),
                  const((3 * LANES, 2 * D_ATT)), const((1, 2 * D_ATT)), const((H_ATT, D_ATT))],
        out_specs=out_specs,
        scratch_shapes=[pltpu.VMEM((8, LANES), F32), pltpu.VMEM((H_ATT, LANES), F32),
                        pltpu.VMEM((H_ATT, LANES), F32)],
        compiler_params=_cparams(2),
        name="in_projection",
    )(x, g, w_main, w_vt, w_small_t, b_forget, dt_bias, dt_bias_x, b_forget_x, tril, sel, ones, head_sel)


def _fox_prompt_kernel(qx_ref, kx_ref, vt_ref, c2t_ref, c2all_ref, kn_ref, o_ref, *, blk, kt, n_split):
    qi = pl.program_id(2)
    q = qx_ref[0, :, 0:LANES]
    cq = jnp.broadcast_to(qx_ref[0, 0:1, LANES:2 * LANES], (blk, LANES))
    lane = lax.broadcasted_iota(jnp.int32, (blk, LANES), 1)
    first = lane < HD_ATT
    zero = jnp.zeros_like(q)
    minus = jnp.full_like(q, -1.0)
    q_heads, bounds = [], []
    for hh in range(2):
        ext = jnp.where((lane >= 3 * hh) & (lane < 3 * hh + 3), minus,
                        jnp.where((lane >= 6 + 3 * hh) & (lane < 9 + 3 * hh), cq, zero))
        own = jnp.where(first, q, zero) if hh == 0 else jnp.where(first, zero, q)
        q_heads.append(jnp.concatenate([own, ext], axis=1))
        qsq = _dot_nt(jnp.ones((8, LANES), BF16), own * own)[0:1]
        c2q = c2t_ref[0, 0, hh:hh + 1, :]
        bounds.append(BOUND_SCALE * jnp.sqrt(qsq * kn_ref[0, 0, hh:hh + 1, 0:1])
                      + (c2q[:, 0:1] - c2q) + BOUND_MARGIN)

    def v_ones(hh, k0, n):
        vt = vt_ref[0, hh * HD_ATT:(hh + 1) * HD_ATT, pl.ds(k0, n)]
        return jnp.concatenate([vt, jnp.ones((ONES_ROWS, n), BF16)], axis=0)

    def causal(st, k0, n, q_off=0, q_n=blk):
        key = lax.broadcasted_iota(jnp.int32, (n, q_n), 0) + (k0 - qi * blk - q_off)
        qry = lax.broadcasted_iota(jnp.int32, (n, q_n), 1)
        return jnp.where(key <= qry, st, NEG_BIG)

    qw = blk // n_split
    chains = [(hh, c) for c in range(n_split) for hh in range(2)]
    q_parts = [q_heads[hh][c * qw:(c + 1) * qw, :] for hh, c in chains]
    b_parts = [bounds[hh][:, c * qw:(c + 1) * qw] for hh, c in chains]

    def fast_tile(s, carry, masked):
        k0 = pl.multiple_of(s * kt, kt)
        kj = kx_ref[0, pl.ds(k0, kt), :]
        st_next = _dot_nt(kj, q_parts[0])
        out = []
        for i, ((hh, c), (g, acc)) in enumerate(zip(chains, carry)):
            st = st_next
            if i + 1 < len(chains):
                st_next = _dot_nt(kj, q_parts[i + 1])
            if masked:
                st = causal(st, k0, kt, c * qw, qw)
            g = jnp.maximum(g, jnp.max(st.reshape(kt // 8, 8, qw), axis=0))
            p = jnp.exp2(st - b_parts[i]).astype(BF16)
            out.append((g, acc + _dot(v_ones(hh, k0, kt), p)))
        return tuple(out)

    n_tiles = (qi * blk + blk + kt - 1) // kt
    n_dead = functools.reduce(jnp.minimum, [
        jnp.sum((c2t_ref[0, 0, hh:hh + 1, 0:1] - c2all_ref[0, 0, hh:hh + 1, :] < -UNDERFLOW_LOG2).astype(F32))
        for hh in range(2)])
    first_tile = jnp.minimum(n_dead.astype(jnp.int32) // kt, n_tiles - 1)
    init = tuple((jnp.full((8, qw), NEG_BIG, F32), jnp.zeros((HD_ATT + ONES_ROWS, qw), F32))
                 for _ in chains)
    carry = lax.fori_loop(first_tile, n_tiles - 1, lambda s, c: fast_tile(s, c, False), init)

    def diagonal_tile(carry):
        half = blk // 2
        k0 = pl.multiple_of(qi * blk, blk)
        kj = kx_ref[0, pl.ds(k0, blk), :]
        parts = [(hh, q_off, n_keys) for hh in range(2) for q_off, n_keys in ((0, half), (half, blk))]
        sts = [_dot_nt(kj[0:n_keys], q_heads[hh][q_off:q_off + half, :]) for hh, q_off, n_keys in parts]
        new = {}
        for (hh, q_off, n_keys), st in zip(parts, sts):
            g, acc = carry[hh]
            st = causal(st, k0, n_keys, q_off, half)
            g_part = jnp.maximum(g[:, q_off:q_off + half], jnp.max(st.reshape(n_keys // 8, 8, half), axis=0))
            p = jnp.exp2(st - bounds[hh][:, q_off:q_off + half]).astype(BF16)
            new[hh, q_off] = (g_part, acc[:, q_off:q_off + half] + _dot(v_ones(hh, k0, n_keys), p))
        return tuple(tuple(jnp.concatenate([new[hh, 0][i], new[hh, half][i]], axis=1) for i in range(2))
                     for hh in range(2))

    done = diagonal_tile(carry) if (kt == blk and n_split == 1 and blk % 16 == 0) else fast_tile(n_tiles - 1, carry, True)
    by_head = [[done[chains.index((hh, c))] for c in range(n_split)] for hh in range(2)]
    o_t = jnp.concatenate(
        [jnp.concatenate([acc[0:HD_ATT] / acc[HD_ATT:HD_ATT + 1] for _, acc in by_head[hh]], axis=1)
         for hh in range(2)], axis=0)
    o_ref[0] = o_t.T.astype(o_ref.dtype)
    slack = functools.reduce(jnp.maximum, [jnp.max(b_parts[i] - jnp.max(done[i][0], axis=0, keepdims=True))
                                           for i in range(len(chains))])

    @pl.when(slack > SLACK_LIMIT)
    def _():
        def exact_tile(j, carry, masked):
            k0 = pl.multiple_of(j * blk, blk)
            kj = kx_ref[0, pl.ds(k0, blk), :]
            out = []
            for hh in range(2):
                m, acc = carry[hh]
                st = _dot_nt(kj, q_heads[hh])
                if masked:
                    st = causal(st, k0, blk)
                m_new = jnp.maximum(m, jnp.max(st, axis=0, keepdims=True))
                p = jnp.exp2(st - m_new).astype(BF16)
                out.append((m_new, jnp.exp2(m - m_new) * acc + _dot(v_ones(hh, k0, blk), p)))
            return tuple(out)

        init_x = tuple((jnp.full((1, blk), NEG_BIG, F32), jnp.zeros((HD_ATT + ONES_ROWS, blk), F32))
                       for _ in range(2))
        carry_x = lax.fori_loop(0, qi, lambda j, c: exact_tile(j, c, False), init_x)
        done_x = exact_tile(qi, carry_x, True)
        o_x = jnp.concatenate([acc[0:HD_ATT] / acc[HD_ATT:HD_ATT + 1] for _, acc in done_x], axis=0)
        o_ref[0] = o_x.T.astype(o_ref.dtype)


def _fox_prompt(qx, kx, vtb, c2t, kn, blk, kn_blk):
    nb, l, _ = qx.shape
    n_pairs = D_ATT // LANES
    kt = blk if blk >= 1024 or l % (2 * blk) else 2 * blk
    assert blk % kn_blk == 0 and l % kt == 0
    kn_per_blk = blk // kn_blk
    pair_rows = lambda a: a.reshape(nb, n_pairs, 2, a.shape[-1])
    return pl.pallas_call(
        functools.partial(_fox_prompt_kernel, blk=blk, kt=kt, n_split=1),
        out_shape=jax.ShapeDtypeStruct((nb, l, D_ATT), BF16),
        grid=(nb, n_pairs, l // blk),
        in_specs=[pl.BlockSpec((1, blk, 2 * LANES), lambda b, p, i: (b, i, p)),
                  pl.BlockSpec((1, l, 2 * LANES), lambda b, p, i: (b, 0, p)),
                  pl.BlockSpec((1, LANES, l), lambda b, p, i: (b, p, 0)),
                  pl.BlockSpec((1, 1, 2, blk), lambda b, p, i: (b, p, 0, i)),
                  pl.BlockSpec((1, 1, 2, l), lambda b, p, i: (b, p, 0, 0)),
                  pl.BlockSpec((1, 1, 2, LANES), lambda b, p, i: (b, p, 0, (i + 1) * kn_per_blk - 1))],
        out_specs=pl.BlockSpec((1, blk, LANES), lambda b, p, i: (b, i, p)),
        compiler_params=_cparams(3),
        name="fox_prompt",
    )(qx, kx, vtb, pair_rows(c2t), pair_rows(c2t), pair_rows(kn))


def _ssd_gate_norm(y, z, g):
    return _rms(y * _silu(z), g)


def _ssd_prompt_kernel(xbc_ref, z_ref, dtx_ref, dtt_ref, cw_ref, cb_ref, alx_ref, alc_ref,
                       dsx_ref, g_ref, tril_ref, triu_ref,
                       y_ref, st_ref, xp_ref, state_ref):
    q = SSD_CHUNK
    c = pl.program_id(1)

    @pl.when(c == 0)
    def _():
        xp_ref[...] = jnp.zeros_like(xp_ref)
        state_ref[...] = jnp.zeros_like(state_ref)

    xb = xbc_ref[0]
    prev = xp_ref[...]
    row8 = lax.broadcasted_iota(jnp.int32, (8, CONV_CH), 0)
    acc = cb_ref[...] + xb * cw_ref[CONV_WIDTH - 1:CONV_WIDTH, :]
    for back in range(1, CONV_WIDTH):
        sh = pltpu.roll(xb, back, 0)
        head = jnp.where(row8 < back, pltpu.roll(prev, back, 0), sh[0:8])
        shifted = jnp.concatenate([head, sh[8:]], axis=0)
        acc = acc + shifted * cw_ref[CONV_WIDTH - 1 - back:CONV_WIDTH - back, :]
    xp_ref[...] = xb[q - 8:q]
    act = _silu(acc)
    xs = act[:, :D_SSM]
    bmat = act[:, D_SSM:D_SSM + N_BC_GROUPS * N_STATE]
    cmat = act[:, D_SSM + N_BC_GROUPS * N_STATE:]

    dt = dtx_ref[0]
    a_x = -jnp.exp(alx_ref[...]) * dt
    hi, mid, lo = _split3(a_x)
    parts = jnp.concatenate([hi, mid, lo], axis=1).astype(BF16)
    r = _dot(tril_ref[...], parts)
    acs_x = r[:, 0:D_SSM] + r[:, D_SSM:2 * D_SSM] + r[:, 2 * D_SSM:3 * D_SSM]
    a_t = -jnp.exp(alc_ref[...]) * dtt_ref[0]
    acs_t = _cumsum_lanes(a_t, triu_ref[...])

    xdt = xs * dt
    a_last = acs_x[q - 1:q, :]
    xdt_end = (xdt * jnp.exp(a_last - acs_x)).astype(BF16)
    e_acs = jnp.exp(acs_x)
    xdt_b = xdt.astype(BF16)
    state_old = state_ref[...]
    state_b = state_old.astype(BF16)

    row = lax.broadcasted_iota(jnp.int32, (q, q), 0)
    col = lax.broadcasted_iota(jnp.int32, (q, q), 1)
    causal = row >= col
    first = lax.broadcasted_iota(jnp.int32, (q, LANES), 1) < P_SSM

    y_parts = []
    for g in range(N_BC_GROUPS):
        cg = cmat[:, g * N_STATE:(g + 1) * N_STATE].astype(BF16)
        bg = bmat[:, g * N_STATE:(g + 1) * N_STATE].astype(BF16)
        cb = _dot_nt(cg, bg)
        gl = slice(g * GROUP_W, (g + 1) * GROUP_W)
        y_off = _dot(cg, state_b[:, gl]) * e_acs[:, gl]
        for pr in range(GROUP_W // LANES):
            lo_lane = g * GROUP_W + pr * LANES
            pair = []
            for hh in range(2):
                h = lo_lane // P_SSM + hh
                seg = acs_x[:, h * P_SSM:h * P_SSM + 1] - acs_t[h:h + 1, :]
                decay = jnp.exp(jnp.where(causal, seg, -jnp.inf))
                gm = (cb * decay).astype(BF16)
                pair.append(_dot(gm, xdt_b[:, lo_lane:lo_lane + LANES]))
            y_parts.append(jnp.where(first, pair[0], pair[1]) + y_off[:, pr * LANES:(pr + 1) * LANES])
        state_ref[:, gl] = jnp.exp(a_last[:, gl]) * state_old[:, gl] + _dot_tn(bg, xdt_end[:, gl])
    y = jnp.concatenate(y_parts, axis=1) + dsx_ref[...] * xs
    y_ref[0] = _ssd_gate_norm(y, z_ref[0], g_ref[...]).astype(y_ref.dtype)

    @pl.when(c == pl.num_programs(1) - 1)
    def _():
        st_ref[0] = state_ref[...].T


def _ssd_prompt(xbc, z, dtx, dtt, conv_w, conv_b, a_log_x, a_log_c, d_skip_x, norm_g):
    nb, l, _ = xbc.shape
    q = SSD_CHUNK
    tril = jnp.tril(jnp.ones((q, q), F32)).astype(BF16)
    triu = jnp.triu(jnp.ones((q, q), F32)).astype(BF16)
    tok = lambda w: pl.BlockSpec((1, q, w), lambda b, c: (b, c, 0))
    const = lambda shape: pl.BlockSpec(shape, lambda b, c: (0,) * len(shape))
    return pl.pallas_call(
        _ssd_prompt_kernel,
        out_shape=(jax.ShapeDtypeStruct((nb, l, D_SSM), BF16),
                   jax.ShapeDtypeStruct((nb, D_SSM, N_STATE), F32)),
        grid=(nb, l // q),
        in_specs=[tok(CONV_CH), tok(D_SSM), tok(D_SSM),
                  pl.BlockSpec((1, H_SSM, q), lambda b, c: (b, 0, c)),
                  const((CONV_WIDTH, CONV_CH)), const((1, CONV_CH)), const((1, D_SSM)),
                  const((H_SSM, 1)), const((1, D_SSM)), const((1, D_SSM)),
                  const((q, q)), const((q, q))],
        out_specs=(tok(D_SSM), pl.BlockSpec((1, D_SSM, N_STATE), lambda b, c: (b, 0, 0))),
        scratch_shapes=[pltpu.VMEM((8, CONV_CH), F32), pltpu.VMEM((N_STATE, D_SSM), F32)],
        compiler_params=_cparams(2),
        name="ssd_prompt",
    )(xbc, z, dtx, dtt, conv_w, conv_b, a_log_x, a_log_c, d_skip_x, norm_g, tril, triu)


def _mix_out_kernel(x_ref, att_ref, ssm_ref, wo_ref, g_ref, wq_ref, x1_ref, qc_ref):
    x1 = x_ref[...] + _dot(att_ref[...], wo_ref[0:D_ATT, :]) + _dot(ssm_ref[...], wo_ref[D_ATT:, :])
    x1_ref[...] = x1
    h = _rms(x1, g_ref[...]).astype(BF16)
    qc_ref[...] = (_dot(h, wq_ref[...]) * X_SCALE).astype(BF16)


def _mix_out(x, att, ssm, w_out, g_cross, w_cq, bt):
    t = x.shape[0]
    tok = lambda w: pl.BlockSpec((bt, w), lambda i: (i, 0))
    const = lambda shape: pl.BlockSpec(shape, lambda i: (0,) * len(shape))
    return pl.pallas_call(
        _mix_out_kernel,
        out_shape=(jax.ShapeDtypeStruct((t, D_MODEL), F32), jax.ShapeDtypeStruct((t, D_MODEL), BF16)),
        grid=(t // bt,),
        in_specs=[tok(D_MODEL), tok(D_ATT), tok(D_SSM), const((D_MODEL, D_MODEL)),
                  const((1, D_MODEL)), const((D_MODEL, D_MODEL))],
        out_specs=(tok(D_MODEL), tok(D_MODEL)),
        compiler_params=_cparams(1),
        name="mix_out",
    )(x, att, ssm, w_out, g_cross, w_cq)


def _ffn_kernel(x1_ref, o_ref, wco_ref, g_ref, wg_ref, wu_ref, wd_ref, gf_ref, y_ref):
    x2 = x1_ref[...] + _dot(o_ref[...], wco_ref[...])
    h = _rms(x2, g_ref[...]).astype(BF16)
    u = (_silu(_dot(h, wg_ref[...])) * _dot(h, wu_ref[...])).astype(BF16)
    x3 = x2 + _dot(u, wd_ref[...])
    y_ref[...] = _rms(x3, gf_ref[...])


def _ffn(x1, o, w_co, g_ffn, w_gate, w_up, w_down, g_final, bt):
    t = x1.shape[0]
    d_ff = w_gate.shape[1]
    tok = lambda w: pl.BlockSpec((bt, w), lambda i: (i, 0))
    const = lambda shape: pl.BlockSpec(shape, lambda i: (0,) * len(shape),
                                       pipeline_mode=pl.Buffered(1))
    return pl.pallas_call(
        _ffn_kernel,
        out_shape=jax.ShapeDtypeStruct((t, D_MODEL), F32),
        grid=(t // bt,),
        in_specs=[tok(D_MODEL), tok(D_MODEL), const((D_MODEL, D_MODEL)), const((1, D_MODEL)),
                  const((D_MODEL, d_ff)), const((D_MODEL, d_ff)), const((d_ff, D_MODEL)),
                  const((1, D_MODEL))],
        out_specs=tok(D_MODEL),
        compiler_params=_cparams(1),
        name="cross_out_ffn",
    )(x1, o, w_co, g_ffn, w_gate, w_up, w_down, g_final)


def _memory_kv_kernel(mem_ref, g_ref, wk_ref, wv_ref, k32_ref, v32_ref, kb_ref, vb_ref):
    mn = _rms(mem_ref[0], g_ref[...]).astype(BF16)
    k = _dot(mn, wk_ref[...])
    v = _dot(mn, wv_ref[...])
    k32_ref[0] = k
    v32_ref[0] = v
    kb_ref[0] = k.astype(BF16)
    vb_ref[0] = v.astype(BF16)


def _memory_kv(mem, g, w_ck, w_cv):
    nb, m, _ = mem.shape
    blk = pl.BlockSpec((1, m, D_MODEL), lambda b: (b, 0, 0))
    const = lambda shape: pl.BlockSpec(shape, lambda b: (0,) * len(shape))
    f = jax.ShapeDtypeStruct((nb, m, D_MODEL), F32)
    h = jax.ShapeDtypeStruct((nb, m, D_MODEL), BF16)
    return pl.pallas_call(
        _memory_kv_kernel,
        out_shape=(f, f, h, h),
        grid=(nb,),
        in_specs=[blk, const((1, D_MODEL)), const((D_MODEL, D_MODEL)), const((D_MODEL, D_MODEL))],
        out_specs=(blk, blk, blk, blk),
        compiler_params=_cparams(1),
        name="memory_kv",
    )(mem, g, w_ck, w_cv)


def _softmax_rows(s):
    m = jnp.max(s, axis=-1, keepdims=True)
    p = jnp.exp(s - m)
    return p / jnp.sum(p, axis=-1, keepdims=True)


def _cross_prompt_kernel(q_ref, k_ref, v_ref, o_ref):
    for h in range(H_X):
        hl = slice(h * HD_X, (h + 1) * HD_X)
        p = _softmax_rows(_dot_nt(q_ref[0, :, hl], k_ref[0, :, hl]))
        o_ref[0, :, hl] = _dot(p.astype(BF16), v_ref[0, :, hl]).astype(o_ref.dtype)


def _cross_prompt(qc, mk, mv, bt):
    nb, l, _ = qc.shape
    m = mk.shape[1]
    tok = pl.BlockSpec((1, bt, D_MODEL), lambda b, i: (b, i, 0))
    mem = pl.BlockSpec((1, m, D_MODEL), lambda b, i: (b, 0, 0))
    return pl.pallas_call(
        _cross_prompt_kernel,
        out_shape=jax.ShapeDtypeStruct((nb, l, D_MODEL), BF16),
        grid=(nb, l // bt),
        in_specs=[tok, mem, mem],
        out_specs=tok,
        compiler_params=_cparams(2),
        name="cross_prompt",
    )(qc, mk, mv)


_X_SUB = 2 * H_X


def _cross_sample_kernel(q_ref, k_ref, v_ref, o_ref, *, lq):
    nsb, n_mem = k_ref.shape[0], k_ref.shape[1]
    n = n_mem * _X_SUB
    rows = lq * _X_SUB
    n_tiles = n // LANES
    cls = lax.broadcasted_iota(jnp.int32, (rows, LANES), 1) % _X_SUB
    own = cls == lax.broadcasted_iota(jnp.int32, (rows, LANES), 0) % _X_SUB
    valid = cls < H_X

    class_steps = (8, 16, 32, 64)

    def per_token(x):
        x3 = x.reshape(lq, _X_SUB, LANES)
        return jnp.broadcast_to(jnp.sum(x3, axis=1, keepdims=True), x3.shape).reshape(rows, LANES)

    seqs = range(nsb)
    s5 = [_dot_nt(q_ref[i], k_ref[i].reshape(n, LANES).astype(BF16)) for i in seqs]
    s_t = [[per_token(jnp.where(own, s5[i][:, j * LANES:(j + 1) * LANES], 0.0)) for j in range(n_tiles)]
           for i in seqs]
    s_t = [[u + pltpu.roll(u, LANES - H_X, 1) for u in s_t[i]] for i in seqs]
    m = [functools.reduce(jnp.maximum, s_t[i]) for i in seqs]
    for sh in class_steps:
        m = [jnp.maximum(x, pltpu.roll(x, sh, 1)) for x in m]
    p_t = [[jnp.where(valid, jnp.exp(s - m[i]), 0.0) for s in s_t[i]] for i in seqs]
    l = [functools.reduce(jnp.add, p_t[i]) for i in seqs]
    for sh in class_steps:
        l = [x + pltpu.roll(x, sh, 1) for x in l]
    inv = [1.0 / jnp.where(valid, l[i], 1.0) for i in seqs]
    pn = [[p * inv[i] for p in p_t[i]] for i in seqs]
    p5 = [jnp.concatenate([jnp.where(own, x + pltpu.roll(x, H_X, 1), 0.0).astype(BF16) for x in pn[i]], axis=1)
          for i in seqs]
    for i in seqs:
        v5 = v_ref[i].reshape(n, LANES).astype(BF16)
        o_ref[i] = _dot(p5[i], v5).astype(o_ref.dtype)


def _cross_sample(q5, mem_k, mem_v, lq, n_seq_blk):
    ns, rows, _ = q5.shape
    m = mem_k.shape[1]
    tok = pl.BlockSpec((n_seq_blk, rows, LANES), lambda i: (i, 0, 0))
    mem = pl.BlockSpec((n_seq_blk, m, _X_SUB, LANES), lambda i: (i, 0, 0, 0))
    return pl.pallas_call(
        functools.partial(_cross_sample_kernel, lq=lq),
        out_shape=jax.ShapeDtypeStruct((ns, rows, LANES), BF16),
        grid=(ns // n_seq_blk,),
        in_specs=[tok, mem, mem],
        out_specs=tok,
        compiler_params=_cparams(1),
        name="cross_sample",
    )(q5, mem_k, mem_v)


def _fox_sample_kernel(pt_ref, q_ref, kn_ref, vn_ref, lfn_ref, tril_ref, *rest, n_pages, lq, nsb):
    del pt_ref
    per_seq = 3 * n_pages
    k_refs = [rest[u * per_seq:u * per_seq + n_pages] for u in range(nsb)]
    v_refs = [rest[u * per_seq + n_pages:u * per_seq + 2 * n_pages] for u in range(nsb)]
    lf_refs = [rest[u * per_seq + 2 * n_pages:(u + 1) * per_seq] for u in range(nsb)]
    o_ref, kt_ref, vt_ref = rest[nsb * per_seq:nsb * per_seq + 3]
    seqs = range(nsb)

    for u in seqs:
        for j in range(n_pages):
            kt_ref[u, :, :, j * PAGE_SIZE:(j + 1) * PAGE_SIZE] = k_refs[u][j][0].astype(BF16)
            vt_ref[u, :, :, j * PAGE_SIZE:(j + 1) * PAGE_SIZE] = v_refs[u][j][0].astype(BF16)

    tril = tril_ref[...]
    lf_all = [jnp.concatenate([r[0] for r in lf_refs[u]], axis=0) * LOG2E for u in seqs]
    splits = [_split3(x) for x in lf_all]
    incl = [_dot(hi.astype(BF16), tril) + _dot(mid.astype(BF16), tril) + _dot(lo.astype(BF16), tril)
            for hi, mid, lo in splits]
    bias = []
    for u in seqs:
        after = jnp.zeros((H_ATT, 1), F32)
        pieces = [None] * n_pages
        for j in reversed(range(n_pages)):
            sl = slice(j * H_ATT, (j + 1) * H_ATT)
            pieces[j] = incl[u][sl] - lf_all[u][sl] + after
            after = after + incl[u][sl][:, 0:1]
        bias.append(jnp.concatenate(pieces, axis=1))

    lane = lax.broadcasted_iota(jnp.int32, (H_ATT, lq), 1)
    cn = []
    for u in seqs:
        lfn = lfn_ref[u] * LOG2E
        c = jnp.zeros((H_ATT, lq), F32)
        for t in range(lq):
            c = c + jnp.where(lane >= t, lfn[:, t:t + 1], 0.0)
        cn.append(c)
    causal = (lax.broadcasted_iota(jnp.int32, (lq, lq), 1)
              <= lax.broadcasted_iota(jnp.int32, (lq, lq), 0))

    units = [(u, h) for u in seqs for h in range(H_ATT)]
    scores = [(_dot(q_ref[u, h], kt_ref[u, h]), _dot_nt(q_ref[u, h], kn_ref[u, h])) for u, h in units]
    probs = []
    for (u, h), (s, s_new) in zip(units, scores):
        s = s + bias[u][h:h + 1, :]
        s_new = jnp.where(causal, s_new - cn[u][h:h + 1, :], NEG_BIG)
        m = jnp.maximum(jnp.max(s, axis=-1, keepdims=True), jnp.max(s_new, axis=-1, keepdims=True))
        p = jnp.exp2(s - m)
        p_new = jnp.exp2(s_new - m)
        l = jnp.sum(p, axis=-1, keepdims=True) + jnp.sum(p_new, axis=-1, keepdims=True)
        probs.append((p.astype(BF16), p_new.astype(BF16), l))
    for (u, h), (p, p_new, l) in zip(units, probs):
        o = _dot_nt(p, vt_ref[u, h]) + _dot(p_new, vn_ref[u, h])
        o_ref[u, h] = (o / l).astype(o_ref.dtype)


def _fox_sample(qh, kh, vh, lfn, cache_kt, cache_vt, cache_lft, page_table, nsb):
    ns, _, lq, _ = qh.shape
    n_pages = page_table.shape[1]
    past = n_pages * PAGE_SIZE
    tril = jnp.tril(jnp.ones((PAGE_SIZE, PAGE_SIZE), F32)).astype(BF16)
    tok = pl.BlockSpec((nsb, H_ATT, lq, HD_ATT), lambda i, pt: (i, 0, 0, 0))

    def page(u, j, shape):
        return pl.BlockSpec((1,) + shape,
                            lambda i, pt: (pt[(i * nsb + u) * n_pages + j],) + (0,) * len(shape))

    in_specs = [tok, tok, tok, pl.BlockSpec((nsb, H_ATT, lq), lambda i, pt: (i, 0, 0)),
                pl.BlockSpec((PAGE_SIZE, PAGE_SIZE), lambda i, pt: (0, 0))]
    pages = []
    for u in range(nsb):
        in_specs += [page(u, j, (H_ATT, HD_ATT, PAGE_SIZE)) for j in range(n_pages)]
        in_specs += [page(u, j, (H_ATT, HD_ATT, PAGE_SIZE)) for j in range(n_pages)]
        in_specs += [page(u, j, (H_ATT, PAGE_SIZE)) for j in range(n_pages)]
        pages += [cache_kt] * n_pages + [cache_vt] * n_pages + [cache_lft] * n_pages
    grid_spec = pltpu.PrefetchScalarGridSpec(
        num_scalar_prefetch=1,
        grid=(ns // nsb,),
        in_specs=in_specs,
        out_specs=tok,
        scratch_shapes=[pltpu.VMEM((nsb, H_ATT, HD_ATT, past), BF16),
                        pltpu.VMEM((nsb, H_ATT, HD_ATT, past), BF16)],
    )
    return pl.pallas_call(
        functools.partial(_fox_sample_kernel, n_pages=n_pages, lq=lq, nsb=nsb),
        out_shape=jax.ShapeDtypeStruct((ns, H_ATT, lq, HD_ATT), BF16),
        grid_spec=grid_spec,
        compiler_params=_cparams(1),
        name="fox_sample",
    )(page_table.reshape(-1), qh, kh, vh, lfn, tril, *pages)


def _ssd_sample_kernel(xbc_ref, z_ref, dtx_ref, cs_ref, st_ref, cw_ref, cb_ref, alx_ref, dsx_ref,
                       g_ref, y_ref, sto_ref, xp_ref, rows_ref, brow_ref, *, lq):
    nsb = xbc_ref.shape[0]
    assert nsb * (lq + 1) <= LANES
    kw = CONV_WIDTH - 1
    xp_ref[:, 0:kw, :] = cs_ref[...]
    xp_ref[:, kw:kw + lq, :] = xbc_ref[...]
    acc = cb_ref[...]
    for tap in range(CONV_WIDTH):
        acc = acc + xp_ref[:, tap:tap + lq, :] * cw_ref[tap:tap + 1, :]
    act = _silu(acc)
    xs = act[:, :, :D_SSM]
    bmat = act[:, :, D_SSM:D_SSM + N_BC_GROUPS * N_STATE]
    cmat = act[:, :, D_SSM + N_BC_GROUPS * N_STATE:]
    dt = dtx_ref[...]
    a = -jnp.exp(alx_ref[...]) * dt
    xdt = xs * dt
    acs = [a[:, 0:1, :]]
    for t in range(1, lq):
        acs.append(acs[-1] + a[:, t:t + 1, :])
    a_last = acs[lq - 1]

    ccat = jnp.concatenate([cmat[:, :, g * N_STATE:(g + 1) * N_STATE] for g in range(N_BC_GROUPS)],
                           axis=1).astype(BF16)
    state = st_ref[...]
    r = jnp.einsum("sgn,shn->sgh", ccat, state.astype(BF16), preferred_element_type=F32)
    group0 = lax.broadcasted_iota(jnp.int32, (1, 1, D_SSM), 2) < GROUP_W
    y_off = jnp.where(group0, r[:, 0:lq, :], r[:, lq:2 * lq, :])

    ys = []
    for t in range(lq):
        y_t = jnp.exp(acs[t]) * y_off[:, t:t + 1, :] + dsx_ref[...] * xs[:, t:t + 1, :]
        for s in range(t + 1):
            cb = jnp.concatenate(
                [jnp.broadcast_to(
                    jnp.sum(cmat[:, t:t + 1, g * N_STATE:(g + 1) * N_STATE]
                            * bmat[:, s:s + 1, g * N_STATE:(g + 1) * N_STATE], axis=-1, keepdims=True),
                    (nsb, 1, GROUP_W)) for g in range(N_BC_GROUPS)], axis=2)
            y_t = y_t + cb * jnp.exp(acs[t] - acs[s]) * xdt[:, s:s + 1, :]
        ys.append(y_t)
    y = jnp.concatenate(ys, axis=1)
    y_ref[...] = _ssd_gate_norm(y, z_ref[...], g_ref[...]).astype(y_ref.dtype)

    upd = jnp.concatenate([xdt[:, s:s + 1, :] * jnp.exp(a_last - acs[s]) for s in range(lq)], axis=1)
    e_last = jnp.exp(a_last)
    rows_ref[...] = jnp.zeros_like(rows_ref)
    brow_ref[...] = jnp.zeros_like(brow_ref)
    for i in range(nsb):
        rows_ref[i * lq:(i + 1) * lq, :] = upd[i]
        rows_ref[nsb * lq + i:nsb * lq + i + 1, :] = e_last[i]
        brow_ref[i * lq:(i + 1) * lq, :] = bmat[i]
    cols = rows_ref[...].T
    upd_cols = cols.astype(BF16)
    b_all = brow_ref[...]
    row_seq = lax.broadcasted_iota(jnp.int32, (LANES, N_BC_GROUPS * N_STATE), 0) // lq
    top = lax.broadcasted_iota(jnp.int32, (D_SSM, N_STATE), 0) < GROUP_W
    for i in range(nsb):
        b_rows = jnp.where(row_seq == i, b_all, 0.0).astype(BF16)
        m = _dot(upd_cols, b_rows)
        add = jnp.where(top, m[:, 0:N_STATE], m[:, N_STATE:2 * N_STATE])
        decay = jnp.broadcast_to(cols[:, nsb * lq + i:nsb * lq + i + 1], (D_SSM, N_STATE))
        sto_ref[i] = decay * state[i] + add


def _ssd_sample(xbc, z, dtx, conv_state, state, conv_w, conv_b, a_log_x, d_skip_x, norm_g, nsb):
    ns, lq, _ = xbc.shape
    seq = lambda r, w: pl.BlockSpec((nsb, r, w), lambda i: (i, 0, 0))
    const = lambda shape: pl.BlockSpec(shape, lambda i: (0,) * len(shape))
    return pl.pallas_call(
        functools.partial(_ssd_sample_kernel, lq=lq),
        out_shape=(jax.ShapeDtypeStruct((ns, lq, D_SSM), BF16),
                   jax.ShapeDtypeStruct((ns, D_SSM, N_STATE), F32)),
        grid=(ns // nsb,),
        in_specs=[seq(lq, CONV_CH), seq(lq, D_SSM), seq(lq, D_SSM), seq(CONV_WIDTH - 1, CONV_CH),
                  seq(D_SSM, N_STATE), const((CONV_WIDTH, CONV_CH)), const((1, CONV_CH)),
                  const((1, D_SSM)), const((1, D_SSM)), const((1, D_SSM))],
        out_specs=(seq(lq, D_SSM), seq(D_SSM, N_STATE)),
        scratch_shapes=[pltpu.VMEM((nsb, 8, CONV_CH), F32), pltpu.VMEM((LANES, D_SSM), F32),
                        pltpu.VMEM((LANES, N_BC_GROUPS * N_STATE), F32)],
        compiler_params=_cparams(1),
        name="ssd_sample",
    )(xbc, z, dtx, conv_state, state, conv_w, conv_b, a_log_x, d_skip_x, norm_g)


def _pick(pref, n):
    return pref if n % pref == 0 else n


def kernel(x_prompt, x_sample, mem_prompt, cache_k, cache_v, cache_logf, page_table, cache_mem_k, cache_mem_v, state_conv, state_ssm, norm_mix_g, w_in, b_forget, conv_w, conv_b, dt_bias, a_log, d_skip, ssm_norm_g, w_out, norm_cross_g, norm_mem_g, w_cq, w_ck, w_cv, w_co, norm_ffn_g, w_gate, w_up, w_down, final_norm_g):
    assert w_in.shape[0] == 1, "one layer"
    nb, l, _ = x_prompt.shape
    ns, lq, _ = x_sample.shape
    n_phys = cache_k.shape[1]
    row = lambda v: v.reshape(1, -1).astype(F32)
    colv = lambda v: v.reshape(-1, 1).astype(F32)
    per_ch = lambda v: jnp.repeat(v.astype(F32), P_SSM).reshape(1, D_SSM)

    w = w_in[0]
    cuts = [D_ATT, 2 * D_ATT, 3 * D_ATT, 3 * D_ATT + H_ATT, 3 * D_ATT + H_ATT + D_SSM,
            3 * D_ATT + H_ATT + D_SSM + CONV_CH]
    w_q, w_k, w_v, w_f, w_z, w_xbc, w_dt = jnp.split(w, cuts, axis=1)
    lane_pad = lambda a: jnp.pad(a, ((0, 0), (0, LANES - a.shape[1])))
    w_main = jnp.concatenate([w_q, w_k, w_z, w_xbc, jnp.repeat(w_dt, P_SSM, axis=1), lane_pad(w_f)],
                             axis=1).astype(BF16)
    w_small_t = jnp.concatenate([w_f, w_dt], axis=1).T.astype(BF16)
    in_params = (row(norm_mix_g[0]), w_main, w_v.T.astype(BF16), w_small_t, colv(b_forget[0]),
                 colv(dt_bias[0]), per_ch(dt_bias[0]), lane_pad(row(b_forget[0])))
    ssd_params = (conv_w[0], row(conv_b[0]), per_ch(a_log[0]))
    ssd_tail = (per_ch(d_skip[0]), row(ssm_norm_g[0]))
    w_out_b, w_cq_b, w_co_b = w_out[0].astype(BF16), w_cq[0].astype(BF16), w_co[0].astype(BF16)
    w_gate_b, w_up_b, w_down_b = w_gate[0].astype(BF16), w_up[0].astype(BF16), w_down[0].astype(BF16)
    g_cross, g_ffn, g_final = row(norm_cross_g[0]), row(norm_ffn_g[0]), row(final_norm_g)

    def tail(x, att, ssm, cross, bt):
        x1, qc = _mix_out(x, att, ssm, w_out_b, g_cross, w_cq_b, _pick(2 * bt, x.shape[0]))
        o = cross(qc)
        return _ffn(x1, o, w_co_b, g_ffn, w_gate_b, w_up_b, w_down_b, g_final, bt)

    bt = _pick(512, l)
    qx, kx, k32, vtb, v32t, z, xbc, dtx, logft, dtt, c2t, kn = _in_projection(x_prompt, *in_params, bt)
    att = _fox_prompt(qx, kx, vtb, c2t, kn, _pick(1024, l), bt)
    ssm, st_p = _ssd_prompt(xbc, z, dtx, dtt, *ssd_params, colv(a_log[0]), *ssd_tail)
    mk32, mv32, mkb, mvb = _memory_kv(mem_prompt, row(norm_mem_g[0]), w_ck[0].astype(BF16),
                                      w_cv[0].astype(BF16))
    cross_p = lambda qc: _cross_prompt(qc.reshape(nb, l, D_MODEL), mkb, mvb, bt).reshape(nb * l, D_MODEL)
    y_prompt = tail(x_prompt.reshape(nb * l, D_MODEL), att.reshape(nb * l, D_ATT),
                    ssm.reshape(nb * l, D_SSM), cross_p, bt).reshape(nb, l, D_MODEL)

    ts = ns * lq
    bts = _pick(512, ts)
    qx_s, _, k32_s, vtb_s, v32t_s, z_s, xbc_s, dtx_s, logft_s, _, _, _ = _in_projection(
        x_sample.reshape(1, ts, D_MODEL), *in_params, bts)
    seq3 = lambda a: a.reshape(ns, lq, a.shape[-1])
    lfn = logft_s.reshape(H_ATT, ns, lq).transpose(1, 0, 2)
    head_major = lambda a: a.reshape(ns, lq, H_ATT, HD_ATT).transpose(0, 2, 1, 3)
    from_t = lambda a: a.reshape(H_ATT, HD_ATT, ns, lq)
    qb_s = qx_s.reshape(ts, D_ATT // LANES, 2, LANES)[:, :, 0, :]
    att_s = _fox_sample(head_major(qb_s), head_major(k32_s.astype(BF16)),
                        from_t(vtb_s).transpose(2, 0, 3, 1), lfn,
                        cache_k[0].transpose(0, 2, 3, 1), cache_v[0].transpose(0, 2, 3, 1),
                        cache_logf[0].transpose(0, 2, 1), page_table, _pick(2, ns))
    att_s = att_s.transpose(0, 2, 1, 3)
    ssm_s, st_s = _ssd_sample(seq3(xbc_s), seq3(z_s), seq3(dtx_s), state_conv[0],
                              state_ssm[0].reshape(ns, D_SSM, N_STATE), *ssd_params, *ssd_tail,
                              _pick(8, ns))
    n_mem = mem_prompt.shape[1]
    stored = lambda a: a.reshape(ns, -1, H_X, 2, LANES).transpose(0, 1, 3, 2, 4).reshape(ns, -1, _X_SUB, LANES)
    cross_s = lambda qc: _cross_sample(
        stored(qc).reshape(ns, lq * _X_SUB, LANES), stored(cache_mem_k[0]), stored(cache_mem_v[0]), lq,
        _pick(4, ns)).reshape(ns, lq, 2, H_X, LANES).transpose(0, 1, 3, 2, 4).reshape(ts, D_MODEL)
    y_sample = tail(x_sample.reshape(ts, D_MODEL), att_s.reshape(ts, D_ATT), ssm_s.reshape(ts, D_SSM),
                    cross_s, bts).reshape(ns, lq, D_MODEL)

    heads = lambda a, n: a.reshape(1, n, -1, H_ATT, HD_ATT)
    kw = CONV_WIDTH - 1
    return (y_prompt, y_sample,
            heads(k32, nb), v32t.reshape(nb, H_ATT, HD_ATT, l).transpose(0, 3, 1, 2)[None],
            logft.transpose(0, 2, 1)[None],
            mk32.reshape(1, nb, n_mem, H_X, HD_X), mv32.reshape(1, nb, n_mem, H_X, HD_X),
            xbc[:, l - kw:, :][None], st_p.reshape(1, nb, H_SSM, P_SSM, N_STATE),
            heads(k32_s, ns), from_t(v32t_s).transpose(2, 3, 0, 1)[None],
            logft_s.reshape(H_ATT, ns, lq).transpose(1, 2, 0)[None],
            seq3(xbc_s)[:, lq - kw:, :][None], st_s.reshape(1, ns, H_SSM, P_SSM, N_STATE))
```

```python
import functools
import math

import numpy as np
import jax
import jax.numpy as jnp
from jax import lax
from jax.experimental import pallas as pl
from jax.experimental.pallas import tpu as pltpu

F32 = jnp.float32
BF16 = jnp.bfloat16

D_MODEL = 1024
D_ATT = 512
H_ATT = 8
HD_ATT = 64
D_SSM = 512
H_SSM = 8
P_SSM = 64
N_STATE = 128
N_BC_GROUPS = 2
GROUP_W = D_SSM // N_BC_GROUPS
CONV_WIDTH = 4
CONV_CH = D_SSM + 2 * N_BC_GROUPS * N_STATE
H_X = 4
HD_X = 256
EPS = 1e-6
ATT_SCALE = HD_ATT ** -0.5
X_SCALE = HD_X ** -0.5
SSD_CHUNK = 128
PAGE_SIZE = 128
NEG_BIG = -1e30
LOG2E = math.log2(math.e)

BOUND_SCALE = 1.03
BOUND_MARGIN = 2.0
SLACK_LIMIT = 80.0
UNDERFLOW_LOG2 = 154.0

LANES = 128
ONES_ROWS = 16
VMEM_LIMIT = 56 * 1024 * 1024

_Q0, _K0, _Z0, _XBC0, _DT0, _F0, _MAIN_COLS = 0, 512, 1024, 1536, 2560, 3072, 3200


def _cparams(n_axes):
    return pltpu.CompilerParams(dimension_semantics=("arbitrary",) * n_axes,
                                vmem_limit_bytes=VMEM_LIMIT)


def _rms(x, g):
    ms = jnp.mean(x * x, axis=-1, keepdims=True)
    return x * lax.rsqrt(ms + EPS) * g


def _softplus(x):
    return jnp.maximum(x, 0.0) + jnp.log1p(jnp.exp(-jnp.abs(x)))


def _log_sigmoid(x):
    return jnp.minimum(x, 0.0) - jnp.log1p(jnp.exp(-jnp.abs(x)))


def _silu(x):
    return x * (1.0 / (1.0 + jnp.exp(-x)))


def _split3(x):
    hi = x.astype(BF16).astype(F32)
    r1 = x - hi
    mid = r1.astype(BF16).astype(F32)
    lo = (r1 - mid).astype(BF16).astype(F32)
    return hi, mid, lo


def _dot(a, b):
    return jnp.dot(a, b, preferred_element_type=F32)


def _dot_nt(a, b):
    return lax.dot_general(a, b, (((1,), (1,)), ((), ())), preferred_element_type=F32)


def _dot_tn(a, b):
    return lax.dot_general(a, b, (((0,), (0,)), ((), ())), preferred_element_type=F32)


def _cumsum_lanes(x, tri_upper):
    hi, mid, lo = _split3(x)
    parts = jnp.concatenate([hi, mid, lo], axis=0).astype(BF16)
    r = _dot(parts, tri_upper)
    return r[0:8] + r[8:16] + r[16:24]


def _inproj_kernel(x_ref, g_ref, wm_ref, wvt_ref, wst_ref, bf_ref, dtb_ref, dtbx_ref, bfx_ref,
                   tril_ref, sel_ref, ones_ref, hsel_ref, *rest, with_ssd):
    if with_ssd:
        (cw_ref, cb_ref, alx_ref, alc_ref, dsx_ref, gssm_ref, trilq_ref, triuq_ref,
         qx_ref, kx_ref, k32_ref, vtb_ref, v32t_ref, logft_ref, c2t_ref, kn_ref, yssm_ref, st_ref, tail_ref,
         carry_ref, carry_t_ref, carry_kn_ref, xp_ref, state_ref) = rest
    else:
        (qx_ref, kx_ref, k32_ref, vtb_ref, v32t_ref, z_ref, xbc_ref, dtx_ref, logft_ref, dtt_ref,
         c2t_ref, kn_ref, carry_ref, carry_t_ref, carry_kn_ref) = rest

    @pl.when(pl.program_id(1) == 0)
    def _():
        carry_ref[...] = jnp.zeros_like(carry_ref)
        carry_t_ref[...] = jnp.zeros_like(carry_t_ref)
        carry_kn_ref[...] = jnp.zeros_like(carry_kn_ref)
        if with_ssd:
            xp_ref[...] = jnp.zeros_like(xp_ref)
            state_ref[...] = jnp.zeros_like(state_ref)

    h = _rms(x_ref[0], g_ref[...]).astype(BF16)
    bt = h.shape[0]
    xbc = _dot(h, wm_ref[:, _XBC0:_DT0])
    z = _dot(h, wm_ref[:, _Z0:_XBC0])
    dtx = _softplus(_dot(h, wm_ref[:, _DT0:_F0]) + dtbx_ref[...])
    small = _dot_nt(wst_ref[...], h)
    dtt = _softplus(small[8:16] + dtb_ref[...])
    if with_ssd:
        ssd_consts = (cw_ref[...], cb_ref[...], alx_ref[...], alc_ref[...], dsx_ref[...], gssm_ref[...],
                      trilq_ref[...], triuq_ref[...])
        ssd_state = (xp_ref[...], state_ref[...])
        n_chunks = bt // SSD_CHUNK

        def ssd(c, carry):
            rows = slice(c * SSD_CHUNK, (c + 1) * SSD_CHUNK)
            y, tail, state = _ssd_chunk(xbc[rows], z[rows], dtx[rows], dtt[:, rows], *carry, *ssd_consts)
            yssm_ref[0, rows, :] = y
            return tail, state
    else:
        z_ref[0], xbc_ref[0], dtx_ref[0], dtt_ref[0] = z, xbc, dtx, dtt
        n_chunks = 0
        ssd = None
    ssd_slots = [[c for c in range(n_chunks) if c * 4 // n_chunks == i] for i in range(4)]

    def ssd_here(slot, carry):
        for c in ssd_slots[slot]:
            carry = ssd(c, carry)
        return carry

    if with_ssd:
        ssd_state = ssd_here(0, ssd_state)
    f_rows = _dot(h, wm_ref[:, _F0:_MAIN_COLS])
    q = (_dot(h, wm_ref[:, _Q0:_K0]) * (ATT_SCALE * LOG2E)).astype(BF16)

    lf2 = _log_sigmoid(f_rows + bfx_ref[...]) * LOG2E
    parts = jnp.concatenate(_split3(lf2), axis=1).astype(BF16)
    r = _dot(tril_ref[...], parts)
    if with_ssd:
        ssd_state = ssd_here(1, ssd_state)
    k = _dot(h, wm_ref[:, _K0:_Z0])
    k32_ref[0] = k
    kb = k.astype(BF16)

    logft = _log_sigmoid(small[0:8] + bf_ref[...])
    logft_ref[0] = logft
    hi, mid, lo = _split3(logft * LOG2E)
    r_t = _dot_nt(jnp.concatenate([hi, mid, lo], axis=0).astype(BF16), tril_ref[...])
    if with_ssd:
        ssd_state = ssd_here(2, ssd_state)
    vt = _dot_nt(wvt_ref[...], h)
    v32t_ref[0] = vt
    vtb_ref[0] = vt.astype(BF16)

    c2 = r[:, 0:LANES] + r[:, LANES:2 * LANES] + r[:, 2 * LANES:3 * LANES] + carry_ref[0:1, :]
    carry_ref[...] = jnp.broadcast_to(c2[bt - 1:bt, :], carry_ref.shape)
    csplit = jnp.concatenate(_split3(c2), axis=1).astype(BF16)
    ext = (_dot(csplit, sel_ref[...]) + ones_ref[...]).astype(BF16)
    if with_ssd:
        tail, state = ssd_here(3, ssd_state)
        xp_ref[...] = tail
        state_ref[...] = state
        tail_ref[0] = tail

        @pl.when(pl.program_id(1) == pl.num_programs(1) - 1)
        def _():
            st_ref[0] = state.T

    c2t = r_t[0:8] + r_t[8:16] + r_t[16:24] + carry_t_ref[:, 0:1]
    c2t_ref[0] = c2t
    carry_t_ref[...] = jnp.broadcast_to(c2t[:, c2t.shape[1] - 1:], carry_t_ref.shape)
    kf = kb.astype(F32)
    ksq_t = _dot_nt(hsel_ref[...], (kf * kf).astype(BF16))
    kn = jnp.maximum(carry_kn_ref[...], jnp.max(ksq_t, axis=1, keepdims=True))
    carry_kn_ref[...] = kn
    kn_ref[0] = kn
    for p in range(D_ATT // LANES):
        pl_ = slice(p * LANES, (p + 1) * LANES)
        kx_ref[0, :, 2 * p * LANES:(2 * p + 1) * LANES] = kb[:, pl_]
        kx_ref[0, :, (2 * p + 1) * LANES:(2 * p + 2) * LANES] = ext[:, pl_]
        qx_ref[0, :, 2 * p * LANES:(2 * p + 1) * LANES] = q[:, pl_]
        qx_ref[0, :, (2 * p + 1) * LANES:(2 * p + 2) * LANES] = ext[:, D_ATT + p * LANES:D_ATT + (p + 1) * LANES]


def _attention_extras():
    sel = np.zeros((3 * LANES, 2 * D_ATT), np.float32)
    ones = np.zeros((1, 2 * D_ATT), np.float32)
    for p in range(D_ATT // LANES):
        ones[0, p * LANES + 6:p * LANES + 12] = 1.0
        for hh in range(2):
            for s in range(3):
                sel[s * LANES + 2 * p + hh, p * LANES + 3 * hh + s] = 1.0
                sel[s * LANES + 2 * p + hh, D_ATT + p * LANES + 6 + 3 * hh + s] = 1.0
    return jnp.asarray(sel, BF16), jnp.asarray(ones, F32)


def _in_projection(x, g, w_main, w_vt, w_small_t, b_forget, dt_bias, dt_bias_x, b_forget_x, bt, ssd=None):
    nb, l, _ = x.shape
    tril = jnp.tril(jnp.ones((bt, bt), F32)).astype(BF16)
    sel, ones = _attention_extras()
    tok = lambda w: pl.BlockSpec((1, bt, w), lambda b, j: (b, j, 0))
    tok_t = lambda r: pl.BlockSpec((1, r, bt), lambda b, j: (b, 0, j))
    const = lambda shape: pl.BlockSpec(shape, lambda b, j: (0,) * len(shape))
    per_batch = lambda r, w: pl.BlockSpec((1, r, w), lambda b, j: (b, 0, 0))
    f32 = lambda *s: jax.ShapeDtypeStruct(s, F32)
    bf16 = lambda *s: jax.ShapeDtypeStruct(s, BF16)
    outs = {
        "qx": (bf16(nb, l, 2 * D_ATT), tok(2 * D_ATT)), "kx": (bf16(nb, l, 2 * D_ATT), tok(2 * D_ATT)),
        "k32": (f32(nb, l, D_ATT), tok(D_ATT)),
        "vtb": (bf16(nb, D_ATT, l), tok_t(D_ATT)), "v32t": (f32(nb, D_ATT, l), tok_t(D_ATT)),
        "z": (f32(nb, l, D_SSM), tok(D_SSM)), "xbc": (f32(nb, l, CONV_CH), tok(CONV_CH)),
        "dtx": (f32(nb, l, D_SSM), tok(D_SSM)),
        "logft": (f32(nb, H_ATT, l), tok_t(H_ATT)), "dtt": (f32(nb, H_SSM, l), tok_t(H_SSM)),
        "c2t": (f32(nb, H_ATT, l), tok_t(H_ATT)),
        "kn": (f32(nb, H_ATT, (l // bt) * LANES), pl.BlockSpec((1, H_ATT, LANES), lambda b, j: (b, 0, j))),
        "yssm": (bf16(nb, l, D_SSM), tok(D_SSM)), "state": (f32(nb, D_SSM, N_STATE), per_batch(D_SSM, N_STATE)),
        "tail": (f32(nb, 8, CONV_CH), per_batch(8, CONV_CH)),
    }
    if ssd is None:
        names = ("qx", "kx", "k32", "vtb", "v32t", "z", "xbc", "dtx", "logft", "dtt", "c2t", "kn")
        extra_in, extra_specs, extra_scratch = (), [], []
    else:
        names = ("qx", "kx", "k32", "vtb", "v32t", "logft", "c2t", "kn", "yssm", "state", "tail")
        q = SSD_CHUNK
        extra_in = tuple(ssd) + (jnp.tril(jnp.ones((q, q), F32)).astype(BF16),
                                 jnp.triu(jnp.ones((q, q), F32)).astype(BF16))
        extra_specs = [const((CONV_WIDTH, CONV_CH)), const((1, CONV_CH)), const((1, D_SSM)), const((H_SSM, 1)),
                       const((1, D_SSM)), const((1, D_SSM)), const((q, q)), const((q, q))]
        extra_scratch = [pltpu.VMEM((8, CONV_CH), F32), pltpu.VMEM((N_STATE, D_SSM), F32)]
    head_sel = jnp.asarray(np.repeat(np.eye(H_ATT, dtype=np.float32), HD_ATT, axis=1), BF16)
    res = pl.pallas_call(
        functools.partial(_inproj_kernel, with_ssd=ssd is not None),
        out_shape=tuple(outs[n][0] for n in names),
        grid=(nb, l // bt),
        in_specs=[tok(D_MODEL), const((1, D_MODEL)), const((D_MODEL, _MAIN_COLS)),
                  const((D_ATT, D_MODEL)), const((2 * H_ATT, D_MODEL)), const((H_ATT, 1)),
                  const((H_SSM, 1)), const((1, D_SSM)), const((1, LANES)), const((bt, bt)),
                  const((3 * LANES, 2 * D_ATT)), const((1, 2 * D_ATT)), const((H_ATT, D_ATT))] + extra_specs,
        out_specs=tuple(outs[n][1] for n in names),
        scratch_shapes=[pltpu.VMEM((8, LANES), F32), pltpu.VMEM((H_ATT, LANES), F32),
                        pltpu.VMEM((H_ATT, LANES), F32)] + extra_scratch,
        compiler_params=_cparams(2),
        name="in_projection_ssd" if ssd is not None else "in_projection",
    )(x, g, w_main, w_vt, w_small_t, b_forget, dt_bias, dt_bias_x, b_forget_x, tril, sel, ones, head_sel,
      *extra_in)
    return dict(zip(names, res))


def _fox_prompt_kernel(qx_ref, kx_ref, vt_ref, c2t_ref, c2all_ref, kn_ref, o_ref, *, blk, kt, n_split):
    qi = pl.program_id(2)
    q = qx_ref[0, :, 0:LANES]
    cq = jnp.broadcast_to(qx_ref[0, 0:1, LANES:2 * LANES], (blk, LANES))
    lane = lax.broadcasted_iota(jnp.int32, (blk, LANES), 1)
    first = lane < HD_ATT
    zero = jnp.zeros_like(q)
    minus = jnp.full_like(q, -1.0)
    q_heads, bounds = [], []
    for hh in range(2):
        ext = jnp.where((lane >= 3 * hh) & (lane < 3 * hh + 3), minus,
                        jnp.where((lane >= 6 + 3 * hh) & (lane < 9 + 3 * hh), cq, zero))
        own = jnp.where(first, q, zero) if hh == 0 else jnp.where(first, zero, q)
        q_heads.append(jnp.concatenate([own, ext], axis=1))
        qsq = _dot_nt(jnp.ones((8, LANES), BF16), own * own)[0:1]
        c2q = c2t_ref[0, 0, hh:hh + 1, :]
        bounds.append(BOUND_SCALE * jnp.sqrt(qsq * kn_ref[0, 0, hh:hh + 1, 0:1])
                      + (c2q[:, 0:1] - c2q) + BOUND_MARGIN)

    def v_ones(hh, k0, n):
        vt = vt_ref[0, hh * HD_ATT:(hh + 1) * HD_ATT, pl.ds(k0, n)]
        return jnp.concatenate([vt, jnp.ones((ONES_ROWS, n), BF16)], axis=0)

    def causal(st, k0, n, q_off=0, q_n=blk):
        key = lax.broadcasted_iota(jnp.int32, (n, q_n), 0) + (k0 - qi * blk - q_off)
        qry = lax.broadcasted_iota(jnp.int32, (n, q_n), 1)
        return jnp.where(key <= qry, st, NEG_BIG)

    qw = blk // n_split
    chains = [(hh, c) for c in range(n_split) for hh in range(2)]
    q_parts = [q_heads[hh][c * qw:(c + 1) * qw, :] for hh, c in chains]
    b_parts = [bounds[hh][:, c * qw:(c + 1) * qw] for hh, c in chains]

    def fast_tile(s, carry, masked):
        k0 = pl.multiple_of(s * kt, kt)
        kj = kx_ref[0, pl.ds(k0, kt), :]
        st_next = _dot_nt(kj, q_parts[0])
        out = []
        for i, ((hh, c), (g, acc)) in enumerate(zip(chains, carry)):
            st = st_next
            if i + 1 < len(chains):
                st_next = _dot_nt(kj, q_parts[i + 1])
            if masked:
                st = causal(st, k0, kt, c * qw, qw)
            g = jnp.maximum(g, jnp.max(st.reshape(kt // 8, 8, qw), axis=0))
            p = jnp.exp2(st - b_parts[i]).astype(BF16)
            out.append((g, acc + _dot(v_ones(hh, k0, kt), p)))
        return tuple(out)

    n_tiles = (qi * blk + blk + kt - 1) // kt
    n_dead = functools.reduce(jnp.minimum, [
        jnp.sum((c2t_ref[0, 0, hh:hh + 1, 0:1] - c2all_ref[0, 0, hh:hh + 1, :] < -UNDERFLOW_LOG2).astype(F32))
        for hh in range(2)])
    first_tile = jnp.minimum(n_dead.astype(jnp.int32) // kt, n_tiles - 1)
    init = tuple((jnp.full((8, qw), NEG_BIG, F32), jnp.zeros((HD_ATT + ONES_ROWS, qw), F32))
                 for _ in chains)
    carry = lax.fori_loop(first_tile, n_tiles - 1, lambda s, c: fast_tile(s, c, False), init)

    def diagonal_tile(carry):
        half = blk // 2
        k0 = pl.multiple_of(qi * blk, blk)
        kj = kx_ref[0, pl.ds(k0, blk), :]
        parts = [(hh, q_off, n_keys) for hh in range(2) for q_off, n_keys in ((0, half), (half, blk))]
        sts = [_dot_nt(kj[0:n_keys], q_heads[hh][q_off:q_off + half, :]) for hh, q_off, n_keys in parts]
        new = {}
        for (hh, q_off, n_keys), st in zip(parts, sts):
            g, acc = carry[hh]
            st = causal(st, k0, n_keys, q_off, half)
            g_part = jnp.maximum(g[:, q_off:q_off + half], jnp.max(st.reshape(n_keys // 8, 8, half), axis=0))
            p = jnp.exp2(st - bounds[hh][:, q_off:q_off + half]).astype(BF16)
            new[hh, q_off] = (g_part, acc[:, q_off:q_off + half] + _dot(v_ones(hh, k0, n_keys), p))
        return tuple(tuple(jnp.concatenate([new[hh, 0][i], new[hh, half][i]], axis=1) for i in range(2))
                     for hh in range(2))

    done = diagonal_tile(carry) if (kt == blk and n_split == 1 and blk % 16 == 0) else fast_tile(n_tiles - 1, carry, True)
    by_head = [[done[chains.index((hh, c))] for c in range(n_split)] for hh in range(2)]
    o_t = jnp.concatenate(
        [jnp.concatenate([acc[0:HD_ATT] / acc[HD_ATT:HD_ATT + 1] for _, acc in by_head[hh]], axis=1)
         for hh in range(2)], axis=0)
    o_ref[0] = o_t.T.astype(o_ref.dtype)
    slack = functools.reduce(jnp.maximum, [jnp.max(b_parts[i] - jnp.max(done[i][0], axis=0, keepdims=True))
                                           for i in range(len(chains))])

    @pl.when(slack > SLACK_LIMIT)
    def _():
        def exact_tile(j, carry, masked):
            k0 = pl.multiple_of(j * blk, blk)
            kj = kx_ref[0, pl.ds(k0, blk), :]
            out = []
            for hh in range(2):
                m, acc = carry[hh]
                st = _dot_nt(kj, q_heads[hh])
                if masked:
                    st = causal(st, k0, blk)
                m_new = jnp.maximum(m, jnp.max(st, axis=0, keepdims=True))
                p = jnp.exp2(st - m_new).astype(BF16)
                out.append((m_new, jnp.exp2(m - m_new) * acc + _dot(v_ones(hh, k0, blk), p)))
            return tuple(out)

        init_x = tuple((jnp.full((1, blk), NEG_BIG, F32), jnp.zeros((HD_ATT + ONES_ROWS, blk), F32))
                       for _ in range(2))
        carry_x = lax.fori_loop(0, qi, lambda j, c: exact_tile(j, c, False), init_x)
        done_x = exact_tile(qi, carry_x, True)
        o_x = jnp.concatenate([acc[0:HD_ATT] / acc[HD_ATT:HD_ATT + 1] for _, acc in done_x], axis=0)
        o_ref[0] = o_x.T.astype(o_ref.dtype)


def _fox_prompt(qx, kx, vtb, c2t, kn, blk, kn_blk):
    nb, l, _ = qx.shape
    n_pairs = D_ATT // LANES
    kt = blk if blk >= 1024 or l % (2 * blk) else 2 * blk
    assert blk % kn_blk == 0 and l % kt == 0
    kn_per_blk = blk // kn_blk
    pair_rows = lambda a: a.reshape(nb, n_pairs, 2, a.shape[-1])
    return pl.pallas_call(
        functools.partial(_fox_prompt_kernel, blk=blk, kt=kt, n_split=1),
        out_shape=jax.ShapeDtypeStruct((nb, l, D_ATT), BF16),
        grid=(nb, n_pairs, l // blk),
        in_specs=[pl.BlockSpec((1, blk, 2 * LANES), lambda b, p, i: (b, i, p)),
                  pl.BlockSpec((1, l, 2 * LANES), lambda b, p, i: (b, 0, p)),
                  pl.BlockSpec((1, LANES, l), lambda b, p, i: (b, p, 0)),
                  pl.BlockSpec((1, 1, 2, blk), lambda b, p, i: (b, p, 0, i)),
                  pl.BlockSpec((1, 1, 2, l), lambda b, p, i: (b, p, 0, 0)),
                  pl.BlockSpec((1, 1, 2, LANES), lambda b, p, i: (b, p, 0, (i + 1) * kn_per_blk - 1))],
        out_specs=pl.BlockSpec((1, blk, LANES), lambda b, p, i: (b, i, p)),
        compiler_params=_cparams(3),
        name="fox_prompt",
    )(qx, kx, vtb, pair_rows(c2t), pair_rows(c2t), pair_rows(kn))


def _ssd_gate_norm(y, z, g):
    return _rms(y * _silu(z), g)


def _ssd_chunk(xb, z, dt, dtt, prev, state_old, cw, cb, alx, alc, dsx, g_norm, tril, triu):
    q = xb.shape[0]
    row8 = lax.broadcasted_iota(jnp.int32, (8, CONV_CH), 0)
    acc = cb + xb * cw[CONV_WIDTH - 1:CONV_WIDTH, :]
    for back in range(1, CONV_WIDTH):
        sh = pltpu.roll(xb, back, 0)
        head = jnp.where(row8 < back, pltpu.roll(prev, back, 0), sh[0:8])
        shifted = jnp.concatenate([head, sh[8:]], axis=0)
        acc = acc + shifted * cw[CONV_WIDTH - 1 - back:CONV_WIDTH - back, :]
    act = _silu(acc)
    xs = act[:, :D_SSM]
    bmat = act[:, D_SSM:D_SSM + N_BC_GROUPS * N_STATE]
    cmat = act[:, D_SSM + N_BC_GROUPS * N_STATE:]

    a_x = -jnp.exp(alx) * dt
    hi, mid, lo = _split3(a_x)
    parts = jnp.concatenate([hi, mid, lo], axis=1).astype(BF16)
    r = _dot(tril, parts)
    acs_x = r[:, 0:D_SSM] + r[:, D_SSM:2 * D_SSM] + r[:, 2 * D_SSM:3 * D_SSM]
    a_t = -jnp.exp(alc) * dtt
    acs_t = _cumsum_lanes(a_t, triu)

    xdt = xs * dt
    a_last = acs_x[q - 1:q, :]
    xdt_end = (xdt * jnp.exp(a_last - acs_x)).astype(BF16)
    e_acs = jnp.exp(acs_x)
    xdt_b = xdt.astype(BF16)
    state_b = state_old.astype(BF16)

    row = lax.broadcasted_iota(jnp.int32, (q, q), 0)
    col = lax.broadcasted_iota(jnp.int32, (q, q), 1)
    causal = row >= col
    first = lax.broadcasted_iota(jnp.int32, (q, LANES), 1) < P_SSM

    y_parts, state_new = [], []
    for g in range(N_BC_GROUPS):
        cg = cmat[:, g * N_STATE:(g + 1) * N_STATE].astype(BF16)
        bg = bmat[:, g * N_STATE:(g + 1) * N_STATE].astype(BF16)
        cbm = _dot_nt(cg, bg)
        gl = slice(g * GROUP_W, (g + 1) * GROUP_W)
        y_off = _dot(cg, state_b[:, gl]) * e_acs[:, gl]
        for pr in range(GROUP_W // LANES):
            lo_lane = g * GROUP_W + pr * LANES
            pair = []
            for hh in range(2):
                h = lo_lane // P_SSM + hh
                seg = acs_x[:, h * P_SSM:h * P_SSM + 1] - acs_t[h:h + 1, :]
                decay = jnp.exp(jnp.where(causal, seg, -jnp.inf))
                gm = (cbm * decay).astype(BF16)
                pair.append(_dot(gm, xdt_b[:, lo_lane:lo_lane + LANES]))
            y_parts.append(jnp.where(first, pair[0], pair[1]) + y_off[:, pr * LANES:(pr + 1) * LANES])
        state_new.append(jnp.exp(a_last[:, gl]) * state_old[:, gl] + _dot_tn(bg, xdt_end[:, gl]))
    y = jnp.concatenate(y_parts, axis=1) + dsx * xs
    return (_ssd_gate_norm(y, z, g_norm).astype(BF16), xb[q - 8:q], jnp.concatenate(state_new, axis=1))


def _ssd_prompt_kernel(xbc_ref, z_ref, dtx_ref, dtt_ref, cw_ref, cb_ref, alx_ref, alc_ref,
                       dsx_ref, g_ref, tril_ref, triu_ref,
                       y_ref, st_ref, xp_ref, state_ref):
    c = pl.program_id(1)

    @pl.when(c == 0)
    def _():
        xp_ref[...] = jnp.zeros_like(xp_ref)
        state_ref[...] = jnp.zeros_like(state_ref)

    y, tail, state = _ssd_chunk(xbc_ref[0], z_ref[0], dtx_ref[0], dtt_ref[0], xp_ref[...], state_ref[...],
                                cw_ref[...], cb_ref[...], alx_ref[...], alc_ref[...], dsx_ref[...],
                                g_ref[...], tril_ref[...], triu_ref[...])
    y_ref[0] = y
    xp_ref[...] = tail
    state_ref[...] = state

    @pl.when(c == pl.num_programs(1) - 1)
    def _():
        st_ref[0] = state_ref[...].T


def _ssd_prompt(xbc, z, dtx, dtt, conv_w, conv_b, a_log_x, a_log_c, d_skip_x, norm_g):
    nb, l, _ = xbc.shape
    q = SSD_CHUNK
    tril = jnp.tril(jnp.ones((q, q), F32)).astype(BF16)
    triu = jnp.triu(jnp.ones((q, q), F32)).astype(BF16)
    tok = lambda w: pl.BlockSpec((1, q, w), lambda b, c: (b, c, 0))
    const = lambda shape: pl.BlockSpec(shape, lambda b, c: (0,) * len(shape))
    return pl.pallas_call(
        _ssd_prompt_kernel,
        out_shape=(jax.ShapeDtypeStruct((nb, l, D_SSM), BF16),
                   jax.ShapeDtypeStruct((nb, D_SSM, N_STATE), F32)),
        grid=(nb, l // q),
        in_specs=[tok(CONV_CH), tok(D_SSM), tok(D_SSM),
                  pl.BlockSpec((1, H_SSM, q), lambda b, c: (b, 0, c)),
                  const((CONV_WIDTH, CONV_CH)), const((1, CONV_CH)), const((1, D_SSM)),
                  const((H_SSM, 1)), const((1, D_SSM)), const((1, D_SSM)),
                  const((q, q)), const((q, q))],
        out_specs=(tok(D_SSM), pl.BlockSpec((1, D_SSM, N_STATE), lambda b, c: (b, 0, 0))),
        scratch_shapes=[pltpu.VMEM((8, CONV_CH), F32), pltpu.VMEM((N_STATE, D_SSM), F32)],
        compiler_params=_cparams(2),
        name="ssd_prompt",
    )(xbc, z, dtx, dtt, conv_w, conv_b, a_log_x, a_log_c, d_skip_x, norm_g, tril, triu)


def _mix_out_kernel(x_ref, att_ref, ssm_ref, wo_ref, g_ref, wq_ref, x1_ref, qc_ref):
    x1 = x_ref[...] + _dot(att_ref[...], wo_ref[0:D_ATT, :]) + _dot(ssm_ref[...], wo_ref[D_ATT:, :])
    x1_ref[...] = x1
    h = _rms(x1, g_ref[...]).astype(BF16)
    qc_ref[...] = (_dot(h, wq_ref[...]) * X_SCALE).astype(BF16)


def _mix_out(x, att, ssm, w_out, g_cross, w_cq, bt):
    t = x.shape[0]
    tok = lambda w: pl.BlockSpec((bt, w), lambda i: (i, 0))
    const = lambda shape: pl.BlockSpec(shape, lambda i: (0,) * len(shape))
    return pl.pallas_call(
        _mix_out_kernel,
        out_shape=(jax.ShapeDtypeStruct((t, D_MODEL), F32), jax.ShapeDtypeStruct((t, D_MODEL), BF16)),
        grid=(t // bt,),
        in_specs=[tok(D_MODEL), tok(D_ATT), tok(D_SSM), const((D_MODEL, D_MODEL)),
                  const((1, D_MODEL)), const((D_MODEL, D_MODEL))],
        out_specs=(tok(D_MODEL), tok(D_MODEL)),
        compiler_params=_cparams(1),
        name="mix_out",
    )(x, att, ssm, w_out, g_cross, w_cq)


def _ffn_kernel(x1_ref, o_ref, wco_ref, g_ref, wg_ref, wu_ref, wd_ref, gf_ref, y_ref):
    x2 = x1_ref[...] + _dot(o_ref[...], wco_ref[...])
    h = _rms(x2, g_ref[...]).astype(BF16)
    u = (_silu(_dot(h, wg_ref[...])) * _dot(h, wu_ref[...])).astype(BF16)
    x3 = x2 + _dot(u, wd_ref[...])
    y_ref[...] = _rms(x3, gf_ref[...])


def _ffn(x1, o, w_co, g_ffn, w_gate, w_up, w_down, g_final, bt):
    t = x1.shape[0]
    d_ff = w_gate.shape[1]
    tok = lambda w: pl.BlockSpec((bt, w), lambda i: (i, 0))
    const = lambda shape: pl.BlockSpec(shape, lambda i: (0,) * len(shape),
                                       pipeline_mode=pl.Buffered(1))
    return pl.pallas_call(
        _ffn_kernel,
        out_shape=jax.ShapeDtypeStruct((t, D_MODEL), F32),
        grid=(t // bt,),
        in_specs=[tok(D_MODEL), tok(D_MODEL), const((D_MODEL, D_MODEL)), const((1, D_MODEL)),
                  const((D_MODEL, d_ff)), const((D_MODEL, d_ff)), const((d_ff, D_MODEL)),
                  const((1, D_MODEL))],
        out_specs=tok(D_MODEL),
        compiler_params=_cparams(1),
        name="cross_out_ffn",
    )(x1, o, w_co, g_ffn, w_gate, w_up, w_down, g_final)


def _memory_kv_kernel(mem_ref, g_ref, wk_ref, wv_ref, k32_ref, v32_ref, kb_ref, vb_ref):
    mn = _rms(mem_ref[0], g_ref[...]).astype(BF16)
    k = _dot(mn, wk_ref[...])
    v = _dot(mn, wv_ref[...])
    k32_ref[0] = k
    v32_ref[0] = v
    kb_ref[0] = k.astype(BF16)
    vb_ref[0] = v.astype(BF16)


def _memory_kv(mem, g, w_ck, w_cv):
    nb, m, _ = mem.shape
    blk = pl.BlockSpec((1, m, D_MODEL), lambda b: (b, 0, 0))
    const = lambda shape: pl.BlockSpec(shape, lambda b: (0,) * len(shape))
    f = jax.ShapeDtypeStruct((nb, m, D_MODEL), F32)
    h = jax.ShapeDtypeStruct((nb, m, D_MODEL), BF16)
    return pl.pallas_call(
        _memory_kv_kernel,
        out_shape=(f, f, h, h),
        grid=(nb,),
        in_specs=[blk, const((1, D_MODEL)), const((D_MODEL, D_MODEL)), const((D_MODEL, D_MODEL))],
        out_specs=(blk, blk, blk, blk),
        compiler_params=_cparams(1),
        name="memory_kv",
    )(mem, g, w_ck, w_cv)


def _softmax_rows(s):
    m = jnp.max(s, axis=-1, keepdims=True)
    p = jnp.exp(s - m)
    return p / jnp.sum(p, axis=-1, keepdims=True)


def _cross_prompt_kernel(q_ref, k_ref, v_ref, o_ref):
    for h in range(H_X):
        hl = slice(h * HD_X, (h + 1) * HD_X)
        p = _softmax_rows(_dot_nt(q_ref[0, :, hl], k_ref[0, :, hl]))
        o_ref[0, :, hl] = _dot(p.astype(BF16), v_ref[0, :, hl]).astype(o_ref.dtype)


def _cross_prompt(qc, mk, mv, bt):
    nb, l, _ = qc.shape
    m = mk.shape[1]
    tok = pl.BlockSpec((1, bt, D_MODEL), lambda b, i: (b, i, 0))
    mem = pl.BlockSpec((1, m, D_MODEL), lambda b, i: (b, 0, 0))
    return pl.pallas_call(
        _cross_prompt_kernel,
        out_shape=jax.ShapeDtypeStruct((nb, l, D_MODEL), BF16),
        grid=(nb, l // bt),
        in_specs=[tok, mem, mem],
        out_specs=tok,
        compiler_params=_cparams(2),
        name="cross_prompt",
    )(qc, mk, mv)


_X_SUB = 2 * H_X


def _cross_sample_kernel(q_ref, k_ref, v_ref, o_ref, *, lq):
    nsb, n_mem = k_ref.shape[0], k_ref.shape[1]
    n = n_mem * _X_SUB
    rows = lq * _X_SUB
    n_tiles = n // LANES
    cls = lax.broadcasted_iota(jnp.int32, (rows, LANES), 1) % _X_SUB
    own = cls == lax.broadcasted_iota(jnp.int32, (rows, LANES), 0) % _X_SUB
    valid = cls < H_X

    class_steps = (8, 16, 32, 64)

    def per_token(x):
        x3 = x.reshape(lq, _X_SUB, LANES)
        return jnp.broadcast_to(jnp.sum(x3, axis=1, keepdims=True), x3.shape).reshape(rows, LANES)

    seqs = range(nsb)
    s5 = [_dot_nt(q_ref[i], k_ref[i].reshape(n, LANES).astype(BF16)) for i in seqs]
    s_t = [[per_token(jnp.where(own, s5[i][:, j * LANES:(j + 1) * LANES], 0.0)) for j in range(n_tiles)]
           for i in seqs]
    s_t = [[u + pltpu.roll(u, LANES - H_X, 1) for u in s_t[i]] for i in seqs]
    m = [functools.reduce(jnp.maximum, s_t[i]) for i in seqs]
    for sh in class_steps:
        m = [jnp.maximum(x, pltpu.roll(x, sh, 1)) for x in m]
    p_t = [[jnp.where(valid, jnp.exp(s - m[i]), 0.0) for s in s_t[i]] for i in seqs]
    l = [functools.reduce(jnp.add, p_t[i]) for i in seqs]
    for sh in class_steps:
        l = [x + pltpu.roll(x, sh, 1) for x in l]
    inv = [1.0 / jnp.where(valid, l[i], 1.0) for i in seqs]
    pn = [[p * inv[i] for p in p_t[i]] for i in seqs]
    p5 = [jnp.concatenate([jnp.where(own, x + pltpu.roll(x, H_X, 1), 0.0).astype(BF16) for x in pn[i]], axis=1)
          for i in seqs]
    for i in seqs:
        v5 = v_ref[i].reshape(n, LANES).astype(BF16)
        o_ref[i] = _dot(p5[i], v5).astype(o_ref.dtype)


def _cross_sample(q5, mem_k, mem_v, lq, n_seq_blk):
    ns, rows, _ = q5.shape
    m = mem_k.shape[1]
    tok = pl.BlockSpec((n_seq_blk, rows, LANES), lambda i: (i, 0, 0))
    mem = pl.BlockSpec((n_seq_blk, m, _X_SUB, LANES), lambda i: (i, 0, 0, 0))
    return pl.pallas_call(
        functools.partial(_cross_sample_kernel, lq=lq),
        out_shape=jax.ShapeDtypeStruct((ns, rows, LANES), BF16),
        grid=(ns // n_seq_blk,),
        in_specs=[tok, mem, mem],
        out_specs=tok,
        compiler_params=_cparams(1),
        name="cross_sample",
    )(q5, mem_k, mem_v)


def _fox_sample_kernel(pt_ref, q_ref, kn_ref, vn_ref, lfn_ref, tril_ref, *rest, n_pages, lq, nsb):
    del pt_ref
    per_seq = 3 * n_pages
    k_refs = [rest[u * per_seq:u * per_seq + n_pages] for u in range(nsb)]
    v_refs = [rest[u * per_seq + n_pages:u * per_seq + 2 * n_pages] for u in range(nsb)]
    lf_refs = [rest[u * per_seq + 2 * n_pages:(u + 1) * per_seq] for u in range(nsb)]
    o_ref, kt_ref, vt_ref = rest[nsb * per_seq:nsb * per_seq + 3]
    seqs = range(nsb)

    for u in seqs:
        for j in range(n_pages):
            kt_ref[u, :, :, j * PAGE_SIZE:(j + 1) * PAGE_SIZE] = k_refs[u][j][0].astype(BF16)
            vt_ref[u, :, :, j * PAGE_SIZE:(j + 1) * PAGE_SIZE] = v_refs[u][j][0].astype(BF16)

    tril = tril_ref[...]
    lf_all = [jnp.concatenate([r[0] for r in lf_refs[u]], axis=0) * LOG2E for u in seqs]
    splits = [_split3(x) for x in lf_all]
    incl = [_dot(hi.astype(BF16), tril) + _dot(mid.astype(BF16), tril) + _dot(lo.astype(BF16), tril)
            for hi, mid, lo in splits]
    bias = []
    for u in seqs:
        after = jnp.zeros((H_ATT, 1), F32)
        pieces = [None] * n_pages
        for j in reversed(range(n_pages)):
            sl = slice(j * H_ATT, (j + 1) * H_ATT)
            pieces[j] = incl[u][sl] - lf_all[u][sl] + after
            after = after + incl[u][sl][:, 0:1]
        bias.append(jnp.concatenate(pieces, axis=1))

    lane = lax.broadcasted_iota(jnp.int32, (H_ATT, lq), 1)
    cn = []
    for u in seqs:
        lfn = lfn_ref[u] * LOG2E
        c = jnp.zeros((H_ATT, lq), F32)
        for t in range(lq):
            c = c + jnp.where(lane >= t, lfn[:, t:t + 1], 0.0)
        cn.append(c)
    causal = (lax.broadcasted_iota(jnp.int32, (lq, lq), 1)
              <= lax.broadcasted_iota(jnp.int32, (lq, lq), 0))

    units = [(u, h) for u in seqs for h in range(H_ATT)]
    scores = [(_dot(q_ref[u, h], kt_ref[u, h]), _dot_nt(q_ref[u, h], kn_ref[u, h])) for u, h in units]
    probs = []
    for (u, h), (s, s_new) in zip(units, scores):
        s = s + bias[u][h:h + 1, :]
        s_new = jnp.where(causal, s_new - cn[u][h:h + 1, :], NEG_BIG)
        m = jnp.maximum(jnp.max(s, axis=-1, keepdims=True), jnp.max(s_new, axis=-1, keepdims=True))
        p = jnp.exp2(s - m)
        p_new = jnp.exp2(s_new - m)
        l = jnp.sum(p, axis=-1, keepdims=True) + jnp.sum(p_new, axis=-1, keepdims=True)
        probs.append((p.astype(BF16), p_new.astype(BF16), l))
    for (u, h), (p, p_new, l) in zip(units, probs):
        o = _dot_nt(p, vt_ref[u, h]) + _dot(p_new, vn_ref[u, h])
        o_ref[u, h] = (o / l).astype(o_ref.dtype)


def _fox_sample(qh, kh, vh, lfn, cache_kt, cache_vt, cache_lft, page_table, nsb):
    ns, _, lq, _ = qh.shape
    n_pages = page_table.shape[1]
    past = n_pages * PAGE_SIZE
    tril = jnp.tril(jnp.ones((PAGE_SIZE, PAGE_SIZE), F32)).astype(BF16)
    tok = pl.BlockSpec((nsb, H_ATT, lq, HD_ATT), lambda i, pt: (i, 0, 0, 0))

    def page(u, j, shape):
        return pl.BlockSpec((1,) + shape,
                            lambda i, pt: (pt[(i * nsb + u) * n_pages + j],) + (0,) * len(shape))

    in_specs = [tok, tok, tok, pl.BlockSpec((nsb, H_ATT, lq), lambda i, pt: (i, 0, 0)),
                pl.BlockSpec((PAGE_SIZE, PAGE_SIZE), lambda i, pt: (0, 0))]
    pages = []
    for u in range(nsb):
        in_specs += [page(u, j, (H_ATT, HD_ATT, PAGE_SIZE)) for j in range(n_pages)]
        in_specs += [page(u, j, (H_ATT, HD_ATT, PAGE_SIZE)) for j in range(n_pages)]
        in_specs += [page(u, j, (H_ATT, PAGE_SIZE)) for j in range(n_pages)]
        pages += [cache_kt] * n_pages + [cache_vt] * n_pages + [cache_lft] * n_pages
    grid_spec = pltpu.PrefetchScalarGridSpec(
        num_scalar_prefetch=1,
        grid=(ns // nsb,),
        in_specs=in_specs,
        out_specs=tok,
        scratch_shapes=[pltpu.VMEM((nsb, H_ATT, HD_ATT, past), BF16),
                        pltpu.VMEM((nsb, H_ATT, HD_ATT, past), BF16)],
    )
    return pl.pallas_call(
        functools.partial(_fox_sample_kernel, n_pages=n_pages, lq=lq, nsb=nsb),
        out_shape=jax.ShapeDtypeStruct((ns, H_ATT, lq, HD_ATT), BF16),
        grid_spec=grid_spec,
        compiler_params=_cparams(1),
        name="fox_sample",
    )(page_table.reshape(-1), qh, kh, vh, lfn, tril, *pages)


def _ssd_sample_kernel(xbc_ref, z_ref, dtx_ref, cs_ref, st_ref, cw_ref, cb_ref, alx_ref, dsx_ref,
                       g_ref, y_ref, sto_ref, xp_ref, rows_ref, brow_ref, *, lq):
    nsb = xbc_ref.shape[0]
    assert nsb * (lq + 1) <= LANES
    kw = CONV_WIDTH - 1
    xp_ref[:, 0:kw, :] = cs_ref[...]
    xp_ref[:, kw:kw + lq, :] = xbc_ref[...]
    acc = cb_ref[...]
    for tap in range(CONV_WIDTH):
        acc = acc + xp_ref[:, tap:tap + lq, :] * cw_ref[tap:tap + 1, :]
    act = _silu(acc)
    xs = act[:, :, :D_SSM]
    bmat = act[:, :, D_SSM:D_SSM + N_BC_GROUPS * N_STATE]
    cmat = act[:, :, D_SSM + N_BC_GROUPS * N_STATE:]
    dt = dtx_ref[...]
    a = -jnp.exp(alx_ref[...]) * dt
    xdt = xs * dt
    acs = [a[:, 0:1, :]]
    for t in range(1, lq):
        acs.append(acs[-1] + a[:, t:t + 1, :])
    a_last = acs[lq - 1]

    ccat = jnp.concatenate([cmat[:, :, g * N_STATE:(g + 1) * N_STATE] for g in range(N_BC_GROUPS)],
                           axis=1).astype(BF16)
    state = st_ref[...]
    r = jnp.einsum("sgn,shn->sgh", ccat, state.astype(BF16), preferred_element_type=F32)
    group0 = lax.broadcasted_iota(jnp.int32, (1, 1, D_SSM), 2) < GROUP_W
    y_off = jnp.where(group0, r[:, 0:lq, :], r[:, lq:2 * lq, :])

    ys = []
    for t in range(lq):
        y_t = jnp.exp(acs[t]) * y_off[:, t:t + 1, :] + dsx_ref[...] * xs[:, t:t + 1, :]
        for s in range(t + 1):
            cb = jnp.concatenate(
                [jnp.broadcast_to(
                    jnp.sum(cmat[:, t:t + 1, g * N_STATE:(g + 1) * N_STATE]
                            * bmat[:, s:s + 1, g * N_STATE:(g + 1) * N_STATE], axis=-1, keepdims=True),
                    (nsb, 1, GROUP_W)) for g in range(N_BC_GROUPS)], axis=2)
            y_t = y_t + cb * jnp.exp(acs[t] - acs[s]) * xdt[:, s:s + 1, :]
        ys.append(y_t)
    y = jnp.concatenate(ys, axis=1)
    y_ref[...] = _ssd_gate_norm(y, z_ref[...], g_ref[...]).astype(y_ref.dtype)

    upd = jnp.concatenate([xdt[:, s:s + 1, :] * jnp.exp(a_last - acs[s]) for s in range(lq)], axis=1)
    e_last = jnp.exp(a_last)
    rows_ref[...] = jnp.zeros_like(rows_ref)
    brow_ref[...] = jnp.zeros_like(brow_ref)
    for i in range(nsb):
        rows_ref[i * lq:(i + 1) * lq, :] = upd[i]
        rows_ref[nsb * lq + i:nsb * lq + i + 1, :] = e_last[i]
        brow_ref[i * lq:(i + 1) * lq, :] = bmat[i]
    cols = rows_ref[...].T
    upd_cols = cols.astype(BF16)
    b_all = brow_ref[...]
    row_seq = lax.broadcasted_iota(jnp.int32, (LANES, N_BC_GROUPS * N_STATE), 0) // lq
    top = lax.broadcasted_iota(jnp.int32, (D_SSM, N_STATE), 0) < GROUP_W
    for i in range(nsb):
        b_rows = jnp.where(row_seq == i, b_all, 0.0).astype(BF16)
        m = _dot(upd_cols, b_rows)
        add = jnp.where(top, m[:, 0:N_STATE], m[:, N_STATE:2 * N_STATE])
        decay = jnp.broadcast_to(cols[:, nsb * lq + i:nsb * lq + i + 1], (D_SSM, N_STATE))
        sto_ref[i] = decay * state[i] + add


def _ssd_sample(xbc, z, dtx, conv_state, state, conv_w, conv_b, a_log_x, d_skip_x, norm_g, nsb):
    ns, lq, _ = xbc.shape
    seq = lambda r, w: pl.BlockSpec((nsb, r, w), lambda i: (i, 0, 0))
    const = lambda shape: pl.BlockSpec(shape, lambda i: (0,) * len(shape))
    return pl.pallas_call(
        functools.partial(_ssd_sample_kernel, lq=lq),
        out_shape=(jax.ShapeDtypeStruct((ns, lq, D_SSM), BF16),
                   jax.ShapeDtypeStruct((ns, D_SSM, N_STATE), F32)),
        grid=(ns // nsb,),
        in_specs=[seq(lq, CONV_CH), seq(lq, D_SSM), seq(lq, D_SSM), seq(CONV_WIDTH - 1, CONV_CH),
                  seq(D_SSM, N_STATE), const((CONV_WIDTH, CONV_CH)), const((1, CONV_CH)),
                  const((1, D_SSM)), const((1, D_SSM)), const((1, D_SSM))],
        out_specs=(seq(lq, D_SSM), seq(D_SSM, N_STATE)),
        scratch_shapes=[pltpu.VMEM((nsb, 8, CONV_CH), F32), pltpu.VMEM((LANES, D_SSM), F32),
                        pltpu.VMEM((LANES, N_BC_GROUPS * N_STATE), F32)],
        compiler_params=_cparams(1),
        name="ssd_sample",
    )(xbc, z, dtx, conv_state, state, conv_w, conv_b, a_log_x, d_skip_x, norm_g)


def _pick(pref, n):
    return pref if n % pref == 0 else n


def kernel(x_prompt, x_sample, mem_prompt, cache_k, cache_v, cache_logf, page_table, cache_mem_k, cache_mem_v, state_conv, state_ssm, norm_mix_g, w_in, b_forget, conv_w, conv_b, dt_bias, a_log, d_skip, ssm_norm_g, w_out, norm_cross_g, norm_mem_g, w_cq, w_ck, w_cv, w_co, norm_ffn_g, w_gate, w_up, w_down, final_norm_g):
    assert w_in.shape[0] == 1, "one layer"
    nb, l, _ = x_prompt.shape
    ns, lq, _ = x_sample.shape
    n_phys = cache_k.shape[1]
    row = lambda v: v.reshape(1, -1).astype(F32)
    colv = lambda v: v.reshape(-1, 1).astype(F32)
    per_ch = lambda v: jnp.repeat(v.astype(F32), P_SSM).reshape(1, D_SSM)

    w = w_in[0]
    cuts = [D_ATT, 2 * D_ATT, 3 * D_ATT, 3 * D_ATT + H_ATT, 3 * D_ATT + H_ATT + D_SSM,
            3 * D_ATT + H_ATT + D_SSM + CONV_CH]
    w_q, w_k, w_v, w_f, w_z, w_xbc, w_dt = jnp.split(w, cuts, axis=1)
    lane_pad = lambda a: jnp.pad(a, ((0, 0), (0, LANES - a.shape[1])))
    w_main = jnp.concatenate([w_q, w_k, w_z, w_xbc, jnp.repeat(w_dt, P_SSM, axis=1), lane_pad(w_f)],
                             axis=1).astype(BF16)
    w_small_t = jnp.concatenate([w_f, w_dt], axis=1).T.astype(BF16)
    in_params = (row(norm_mix_g[0]), w_main, w_v.T.astype(BF16), w_small_t, colv(b_forget[0]),
                 colv(dt_bias[0]), per_ch(dt_bias[0]), lane_pad(row(b_forget[0])))
    ssd_params = (conv_w[0], row(conv_b[0]), per_ch(a_log[0]))
    ssd_tail = (per_ch(d_skip[0]), row(ssm_norm_g[0]))
    w_out_b, w_cq_b, w_co_b = w_out[0].astype(BF16), w_cq[0].astype(BF16), w_co[0].astype(BF16)
    w_gate_b, w_up_b, w_down_b = w_gate[0].astype(BF16), w_up[0].astype(BF16), w_down[0].astype(BF16)
    g_cross, g_ffn, g_final = row(norm_cross_g[0]), row(norm_ffn_g[0]), row(final_norm_g)

    def tail(x, att, ssm, cross, bt):
        x1, qc = _mix_out(x, att, ssm, w_out_b, g_cross, w_cq_b, _pick(2 * bt, x.shape[0]))
        o = cross(qc)
        return _ffn(x1, o, w_co_b, g_ffn, w_gate_b, w_up_b, w_down_b, g_final, bt)

    bt = _pick(512, l)
    assert bt % SSD_CHUNK == 0
    pp = _in_projection(x_prompt, *in_params, bt, ssd=(*ssd_params, colv(a_log[0]), *ssd_tail))
    k32, v32t, logft, ssm, st_p = pp["k32"], pp["v32t"], pp["logft"], pp["yssm"], pp["state"]
    att = _fox_prompt(pp["qx"], pp["kx"], pp["vtb"], pp["c2t"], pp["kn"], _pick(1024, l), bt)
    mk32, mv32, mkb, mvb = _memory_kv(mem_prompt, row(norm_mem_g[0]), w_ck[0].astype(BF16),
                                      w_cv[0].astype(BF16))
    cross_p = lambda qc: _cross_prompt(qc.reshape(nb, l, D_MODEL), mkb, mvb, bt).reshape(nb * l, D_MODEL)
    y_prompt = tail(x_prompt.reshape(nb * l, D_MODEL), att.reshape(nb * l, D_ATT),
                    ssm.reshape(nb * l, D_SSM), cross_p, bt).reshape(nb, l, D_MODEL)

    ts = ns * lq
    bts = _pick(512, ts)
    ps = _in_projection(x_sample.reshape(1, ts, D_MODEL), *in_params, bts)
    qx_s, k32_s, vtb_s, v32t_s = ps["qx"], ps["k32"], ps["vtb"], ps["v32t"]
    z_s, xbc_s, dtx_s, logft_s = ps["z"], ps["xbc"], ps["dtx"], ps["logft"]
    seq3 = lambda a: a.reshape(ns, lq, a.shape[-1])
    lfn = logft_s.reshape(H_ATT, ns, lq).transpose(1, 0, 2)
    head_major = lambda a: a.reshape(ns, lq, H_ATT, HD_ATT).transpose(0, 2, 1, 3)
    from_t = lambda a: a.reshape(H_ATT, HD_ATT, ns, lq)
    qb_s = qx_s.reshape(ts, D_ATT // LANES, 2, LANES)[:, :, 0, :]
    att_s = _fox_sample(head_major(qb_s), head_major(k32_s.astype(BF16)),
                        from_t(vtb_s).transpose(2, 0, 3, 1), lfn,
                        cache_k[0].transpose(0, 2, 3, 1), cache_v[0].transpose(0, 2, 3, 1),
                        cache_logf[0].transpose(0, 2, 1), page_table, _pick(2, ns))
    att_s = att_s.transpose(0, 2, 1, 3)
    ssm_s, st_s = _ssd_sample(seq3(xbc_s), seq3(z_s), seq3(dtx_s), state_conv[0],
                              state_ssm[0].reshape(ns, D_SSM, N_STATE), *ssd_params, *ssd_tail,
                              _pick(8, ns))
    n_mem = mem_prompt.shape[1]
    stored = lambda a: a.reshape(ns, -1, H_X, 2, LANES).transpose(0, 1, 3, 2, 4).reshape(ns, -1, _X_SUB, LANES)
    cross_s = lambda qc: _cross_sample(
        stored(qc).reshape(ns, lq * _X_SUB, LANES), stored(cache_mem_k[0]), stored(cache_mem_v[0]), lq,
        _pick(4, ns)).reshape(ns, lq, 2, H_X, LANES).transpose(0, 1, 3, 2, 4).reshape(ts, D_MODEL)
    y_sample = tail(x_sample.reshape(ts, D_MODEL), att_s.reshape(ts, D_ATT), ssm_s.reshape(ts, D_SSM),
                    cross_s, bts).reshape(ns, lq, D_MODEL)

    heads = lambda a, n: a.reshape(1, n, -1, H_ATT, HD_ATT)
    kw = CONV_WIDTH - 1
    return (y_prompt, y_sample,
            heads(k32, nb), v32t.reshape(nb, H_ATT, HD_ATT, l).transpose(0, 3, 1, 2)[None],
            logft.transpose(0, 2, 1)[None],
            mk32.reshape(1, nb, n_mem, H_X, HD_X), mv32.reshape(1, nb, n_mem, H_X, HD_X),
            pp["tail"][:, 8 - kw:, :][None], st_p.reshape(1, nb, H_SSM, P_SSM, N_STATE),
            heads(k32_s, ns), from_t(v32t_s).transpose(2, 3, 0, 1)[None],
            logft_s.reshape(H_ATT, ns, lq).transpose(1, 2, 0)[None],
            seq3(xbc_s)[:, lq - kw:, :][None], st_s.reshape(1, ns, H_SSM, P_SSM, N_STATE))
```

```python
import functools
import math

import numpy as np
import jax
import jax.numpy as jnp
from jax import lax
from jax.experimental import pallas as pl
from jax.experimental.pallas import tpu as pltpu

F32 = jnp.float32
BF16 = jnp.bfloat16

D_MODEL = 1024
D_ATT = 512
H_ATT = 8
HD_ATT = 64
D_SSM = 512
H_SSM = 8
P_SSM = 64
N_STATE = 128
N_BC_GROUPS = 2
GROUP_W = D_SSM // N_BC_GROUPS
CONV_WIDTH = 4
CONV_CH = D_SSM + 2 * N_BC_GROUPS * N_STATE
H_X = 4
HD_X = 256
EPS = 1e-6
ATT_SCALE = HD_ATT ** -0.5
X_SCALE = HD_X ** -0.5
SSD_CHUNK = 128
PAGE_SIZE = 128
NEG_BIG = -1e30
LOG2E = math.log2(math.e)

BOUND_SCALE = 1.03
BOUND_MARGIN = 2.0
SLACK_LIMIT = 80.0
UNDERFLOW_LOG2 = 154.0

LANES = 128
ONES_ROWS = 16
VMEM_LIMIT = 56 * 1024 * 1024

_Q0, _K0, _Z0, _XBC0, _DT0, _F0, _MAIN_COLS = 0, 512, 1024, 1536, 2560, 3072, 3200


def _cparams(n_axes):
    return pltpu.CompilerParams(dimension_semantics=("arbitrary",) * n_axes,
                                vmem_limit_bytes=VMEM_LIMIT)


def _rms(x, g):
    ms = jnp.mean(x * x, axis=-1, keepdims=True)
    return x * lax.rsqrt(ms + EPS) * g


def _softplus(x):
    return jnp.maximum(x, 0.0) + jnp.log1p(jnp.exp(-jnp.abs(x)))


def _log_sigmoid(x):
    return jnp.minimum(x, 0.0) - jnp.log1p(jnp.exp(-jnp.abs(x)))


def _silu(x):
    return x * (1.0 / (1.0 + jnp.exp(-x)))


def _split3(x):
    hi = x.astype(BF16).astype(F32)
    r1 = x - hi
    mid = r1.astype(BF16).astype(F32)
    lo = (r1 - mid).astype(BF16).astype(F32)
    return hi, mid, lo


def _dot(a, b):
    return jnp.dot(a, b, preferred_element_type=F32)


def _dot_nt(a, b):
    return lax.dot_general(a, b, (((1,), (1,)), ((), ())), preferred_element_type=F32)


def _dot_tn(a, b):
    return lax.dot_general(a, b, (((0,), (0,)), ((), ())), preferred_element_type=F32)


def _cumsum_lanes(x, tri_upper):
    hi, mid, lo = _split3(x)
    parts = jnp.concatenate([hi, mid, lo], axis=0).astype(BF16)
    r = _dot(parts, tri_upper)
    return r[0:8] + r[8:16] + r[16:24]


def _inproj_kernel(x_ref, g_ref, wm_ref, wvt_ref, wst_ref, bf_ref, dtb_ref, dtbx_ref, bfx_ref,
                   tril_ref, sel_ref, ones_ref, hsel_ref, *rest, with_ssd):
    if with_ssd:
        (cw_ref, cb_ref, alx_ref, alc_ref, dsx_ref, gssm_ref, trilq_ref, triuq_ref,
         qx_ref, kx_ref, k32_ref, vtb_ref, v32t_ref, logft_ref, c2t_ref, kn_ref, yssm_ref, st_ref, tail_ref,
         carry_ref, carry_t_ref, carry_kn_ref, xp_ref, state_ref) = rest
    else:
        (qx_ref, kx_ref, k32_ref, vtb_ref, v32t_ref, z_ref, xbc_ref, dtx_ref, logft_ref, dtt_ref,
         c2t_ref, kn_ref, carry_ref, carry_t_ref, carry_kn_ref) = rest

    @pl.when(pl.program_id(1) == 0)
    def _():
        carry_ref[...] = jnp.zeros_like(carry_ref)
        carry_t_ref[...] = jnp.zeros_like(carry_t_ref)
        carry_kn_ref[...] = jnp.zeros_like(carry_kn_ref)
        if with_ssd:
            xp_ref[...] = jnp.zeros_like(xp_ref)
            state_ref[...] = jnp.zeros_like(state_ref)

    h = _rms(x_ref[0], g_ref[...]).astype(BF16)
    bt = h.shape[0]
    proj = lambda lo, hi: _dot_nt(h, wm_ref[lo:hi, :])
    xbc = proj(_XBC0, _DT0)
    z = proj(_Z0, _XBC0)
    dtx = _softplus(proj(_DT0, _F0) + dtbx_ref[...])
    small = _dot_nt(wst_ref[...], h)
    dtt = _softplus(small[8:16] + dtb_ref[...])
    if with_ssd:
        ssd_consts = (cw_ref[...], cb_ref[...], alx_ref[...], alc_ref[...], dsx_ref[...], gssm_ref[...],
                      trilq_ref[...], triuq_ref[...])
        ssd_state = (xp_ref[...], state_ref[...])
        n_chunks = bt // SSD_CHUNK

        def ssd(c, carry):
            rows = slice(c * SSD_CHUNK, (c + 1) * SSD_CHUNK)
            y, tail, state = _ssd_chunk(xbc[rows], z[rows], dtx[rows], dtt[:, rows], *carry, *ssd_consts)
            yssm_ref[0, rows, :] = y
            return tail, state
    else:
        z_ref[0], xbc_ref[0], dtx_ref[0], dtt_ref[0] = z, xbc, dtx, dtt
        n_chunks = 0
        ssd = None
    ssd_slots = [[c for c in range(n_chunks) if c * 4 // n_chunks == i] for i in range(4)]

    def ssd_here(slot, carry):
        for c in ssd_slots[slot]:
            carry = ssd(c, carry)
        return carry

    if with_ssd:
        ssd_state = ssd_here(0, ssd_state)
    f_rows = proj(_F0, _MAIN_COLS)
    q = (proj(_Q0, _K0) * (ATT_SCALE * LOG2E)).astype(BF16)

    lf2 = _log_sigmoid(f_rows + bfx_ref[...]) * LOG2E
    parts = jnp.concatenate(_split3(lf2), axis=1).astype(BF16)
    r = _dot(tril_ref[...], parts)
    if with_ssd:
        ssd_state = ssd_here(1, ssd_state)
    k = proj(_K0, _Z0)
    k32_ref[0] = k
    kb = k.astype(BF16)

    logft = _log_sigmoid(small[0:8] + bf_ref[...])
    logft_ref[0] = logft
    hi, mid, lo = _split3(logft * LOG2E)
    r_t = _dot_nt(jnp.concatenate([hi, mid, lo], axis=0).astype(BF16), tril_ref[...])
    if with_ssd:
        ssd_state = ssd_here(2, ssd_state)
    vt = _dot_nt(wvt_ref[...], h)
    v32t_ref[0] = vt
    vtb_ref[0] = vt.astype(BF16)

    c2 = r[:, 0:LANES] + r[:, LANES:2 * LANES] + r[:, 2 * LANES:3 * LANES] + carry_ref[0:1, :]
    carry_ref[...] = jnp.broadcast_to(c2[bt - 1:bt, :], carry_ref.shape)
    csplit = jnp.concatenate(_split3(c2), axis=1).astype(BF16)
    ext = (_dot(csplit, sel_ref[...]) + ones_ref[...]).astype(BF16)
    if with_ssd:
        tail, state = ssd_here(3, ssd_state)
        xp_ref[...] = tail
        state_ref[...] = state
        tail_ref[0] = tail

        @pl.when(pl.program_id(1) == pl.num_programs(1) - 1)
        def _():
            st_ref[0] = state.T

    c2t = r_t[0:8] + r_t[8:16] + r_t[16:24] + carry_t_ref[:, 0:1]
    c2t_ref[0] = c2t
    carry_t_ref[...] = jnp.broadcast_to(c2t[:, c2t.shape[1] - 1:], carry_t_ref.shape)
    kf = kb.astype(F32)
    ksq_t = _dot_nt(hsel_ref[...], (kf * kf).astype(BF16))
    kn = jnp.maximum(carry_kn_ref[...], jnp.max(ksq_t, axis=1, keepdims=True))
    carry_kn_ref[...] = kn
    kn_ref[0] = kn
    for p in range(D_ATT // LANES):
        pl_ = slice(p * LANES, (p + 1) * LANES)
        kx_ref[0, :, 2 * p * LANES:(2 * p + 1) * LANES] = kb[:, pl_]
        kx_ref[0, :, (2 * p + 1) * LANES:(2 * p + 2) * LANES] = ext[:, pl_]
        qx_ref[0, :, 2 * p * LANES:(2 * p + 1) * LANES] = q[:, pl_]
        qx_ref[0, :, (2 * p + 1) * LANES:(2 * p + 2) * LANES] = ext[:, D_ATT + p * LANES:D_ATT + (p + 1) * LANES]


def _attention_extras():
    sel = np.zeros((3 * LANES, 2 * D_ATT), np.float32)
    ones = np.zeros((1, 2 * D_ATT), np.float32)
    for p in range(D_ATT // LANES):
        ones[0, p * LANES + 6:p * LANES + 12] = 1.0
        for hh in range(2):
            for s in range(3):
                sel[s * LANES + 2 * p + hh, p * LANES + 3 * hh + s] = 1.0
                sel[s * LANES + 2 * p + hh, D_ATT + p * LANES + 6 + 3 * hh + s] = 1.0
    return jnp.asarray(sel, BF16), jnp.asarray(ones, F32)


def _in_projection(x, g, w_main, w_vt, w_small_t, b_forget, dt_bias, dt_bias_x, b_forget_x, bt, ssd=None):
    nb, l, _ = x.shape
    tril = jnp.tril(jnp.ones((bt, bt), F32)).astype(BF16)
    sel, ones = _attention_extras()
    tok = lambda w: pl.BlockSpec((1, bt, w), lambda b, j: (b, j, 0))
    tok_t = lambda r: pl.BlockSpec((1, r, bt), lambda b, j: (b, 0, j))
    const = lambda shape: pl.BlockSpec(shape, lambda b, j: (0,) * len(shape))
    per_batch = lambda r, w: pl.BlockSpec((1, r, w), lambda b, j: (b, 0, 0))
    f32 = lambda *s: jax.ShapeDtypeStruct(s, F32)
    bf16 = lambda *s: jax.ShapeDtypeStruct(s, BF16)
    outs = {
        "qx": (bf16(nb, l, 2 * D_ATT), tok(2 * D_ATT)), "kx": (bf16(nb, l, 2 * D_ATT), tok(2 * D_ATT)),
        "k32": (f32(nb, l, D_ATT), tok(D_ATT)),
        "vtb": (bf16(nb, D_ATT, l), tok_t(D_ATT)), "v32t": (f32(nb, D_ATT, l), tok_t(D_ATT)),
        "z": (f32(nb, l, D_SSM), tok(D_SSM)), "xbc": (f32(nb, l, CONV_CH), tok(CONV_CH)),
        "dtx": (f32(nb, l, D_SSM), tok(D_SSM)),
        "logft": (f32(nb, H_ATT, l), tok_t(H_ATT)), "dtt": (f32(nb, H_SSM, l), tok_t(H_SSM)),
        "c2t": (f32(nb, H_ATT, l), tok_t(H_ATT)),
        "kn": (f32(nb, H_ATT, (l // bt) * LANES), pl.BlockSpec((1, H_ATT, LANES), lambda b, j: (b, 0, j))),
        "yssm": (bf16(nb, l, D_SSM), tok(D_SSM)), "state": (f32(nb, D_SSM, N_STATE), per_batch(D_SSM, N_STATE)),
        "tail": (f32(nb, 8, CONV_CH), per_batch(8, CONV_CH)),
    }
    if ssd is None:
        names = ("qx", "kx", "k32", "vtb", "v32t", "z", "xbc", "dtx", "logft", "dtt", "c2t", "kn")
        extra_in, extra_specs, extra_scratch = (), [], []
    else:
        names = ("qx", "kx", "k32", "vtb", "v32t", "logft", "c2t", "kn", "yssm", "state", "tail")
        q = SSD_CHUNK
        extra_in = tuple(ssd) + (jnp.tril(jnp.ones((q, q), F32)).astype(BF16),
                                 jnp.triu(jnp.ones((q, q), F32)).astype(BF16))
        extra_specs = [const((CONV_WIDTH, CONV_CH)), const((1, CONV_CH)), const((1, D_SSM)), const((H_SSM, 1)),
                       const((1, D_SSM)), const((1, D_SSM)), const((q, q)), const((q, q))]
        extra_scratch = [pltpu.VMEM((8, CONV_CH), F32), pltpu.VMEM((N_STATE, D_SSM), F32)]
    head_sel = jnp.asarray(np.repeat(np.eye(H_ATT, dtype=np.float32), HD_ATT, axis=1), BF16)
    res = pl.pallas_call(
        functools.partial(_inproj_kernel, with_ssd=ssd is not None),
        out_shape=tuple(outs[n][0] for n in names),
        grid=(nb, l // bt),
        in_specs=[tok(D_MODEL), const((1, D_MODEL)), const((_MAIN_COLS, D_MODEL)),
                  const((D_ATT, D_MODEL)), const((2 * H_ATT, D_MODEL)), const((H_ATT, 1)),
                  const((H_SSM, 1)), const((1, D_SSM)), const((1, LANES)), const((bt, bt)),
                  const((3 * LANES, 2 * D_ATT)), const((1, 2 * D_ATT)), const((H_ATT, D_ATT))] + extra_specs,
        out_specs=tuple(outs[n][1] for n in names),
        scratch_shapes=[pltpu.VMEM((8, LANES), F32), pltpu.VMEM((H_ATT, LANES), F32),
                        pltpu.VMEM((H_ATT, LANES), F32)] + extra_scratch,
        compiler_params=_cparams(2),
        name="in_projection_ssd" if ssd is not None else "in_projection",
    )(x, g, w_main, w_vt, w_small_t, b_forget, dt_bias, dt_bias_x, b_forget_x, tril, sel, ones, head_sel,
      *extra_in)
    return dict(zip(names, res))


def _fox_prompt_kernel(qx_ref, kx_ref, vt_ref, c2t_ref, c2all_ref, kn_ref, o_ref, *, blk, kt, n_split):
    qi = pl.program_id(2)
    q = qx_ref[0, :, 0:LANES]
    cq = jnp.broadcast_to(qx_ref[0, 0:1, LANES:2 * LANES], (blk, LANES))
    lane = lax.broadcasted_iota(jnp.int32, (blk, LANES), 1)
    first = lane < HD_ATT
    zero = jnp.zeros_like(q)
    minus = jnp.full_like(q, -1.0)
    q_heads, bounds = [], []
    for hh in range(2):
        ext = jnp.where((lane >= 3 * hh) & (lane < 3 * hh + 3), minus,
                        jnp.where((lane >= 6 + 3 * hh) & (lane < 9 + 3 * hh), cq, zero))
        own = jnp.where(first, q, zero) if hh == 0 else jnp.where(first, zero, q)
        q_heads.append(jnp.concatenate([own, ext], axis=1))
        qsq = _dot_nt(jnp.ones((8, LANES), BF16), own * own)[0:1]
        c2q = c2t_ref[0, 0, hh:hh + 1, :]
        bounds.append(BOUND_SCALE * jnp.sqrt(qsq * kn_ref[0, 0, hh:hh + 1, 0:1])
                      + (c2q[:, 0:1] - c2q) + BOUND_MARGIN)

    def v_ones(hh, k0, n):
        vt = vt_ref[0, hh * HD_ATT:(hh + 1) * HD_ATT, pl.ds(k0, n)]
        return jnp.concatenate([vt, jnp.ones((ONES_ROWS, n), BF16)], axis=0)

    def causal(st, k0, n, q_off=0, q_n=blk):
        key = lax.broadcasted_iota(jnp.int32, (n, q_n), 0) + (k0 - qi * blk - q_off)
        qry = lax.broadcasted_iota(jnp.int32, (n, q_n), 1)
        return jnp.where(key <= qry, st, NEG_BIG)

    qw = blk // n_split
    chains = [(hh, c) for c in range(n_split) for hh in range(2)]
    q_parts = [q_heads[hh][c * qw:(c + 1) * qw, :] for hh, c in chains]
    b_parts = [bounds[hh][:, c * qw:(c + 1) * qw] for hh, c in chains]

    def fast_tile(s, carry, masked):
        k0 = pl.multiple_of(s * kt, kt)
        kj = kx_ref[0, pl.ds(k0, kt), :]
        st_next = _dot_nt(kj, q_parts[0])
        out = []
        for i, ((hh, c), (g, acc)) in enumerate(zip(chains, carry)):
            st = st_next
            if i + 1 < len(chains):
                st_next = _dot_nt(kj, q_parts[i + 1])
            if masked:
                st = causal(st, k0, kt, c * qw, qw)
            g = jnp.maximum(g, jnp.max(st.reshape(kt // 8, 8, qw), axis=0))
            p = jnp.exp2(st - b_parts[i]).astype(BF16)
            out.append((g, acc + _dot(v_ones(hh, k0, kt), p)))
        return tuple(out)

    n_tiles = (qi * blk + blk + kt - 1) // kt
    n_dead = functools.reduce(jnp.minimum, [
        jnp.sum((c2t_ref[0, 0, hh:hh + 1, 0:1] - c2all_ref[0, 0, hh:hh + 1, :] < -UNDERFLOW_LOG2).astype(F32))
        for hh in range(2)])
    first_tile = jnp.minimum(n_dead.astype(jnp.int32) // kt, n_tiles - 1)
    init = tuple((jnp.full((8, qw), NEG_BIG, F32), jnp.zeros((HD_ATT + ONES_ROWS, qw), F32))
                 for _ in chains)
    carry = lax.fori_loop(first_tile, n_tiles - 1, lambda s, c: fast_tile(s, c, False), init)

    def diagonal_tile(carry):
        half = blk // 2
        k0 = pl.multiple_of(qi * blk, blk)
        kj = kx_ref[0, pl.ds(k0, blk), :]
        parts = [(hh, q_off, n_keys) for hh in range(2) for q_off, n_keys in ((0, half), (half, blk))]
        sts = [_dot_nt(kj[0:n_keys], q_heads[hh][q_off:q_off + half, :]) for hh, q_off, n_keys in parts]
        new = {}
        for (hh, q_off, n_keys), st in zip(parts, sts):
            g, acc = carry[hh]
            st = causal(st, k0, n_keys, q_off, half)
            g_part = jnp.maximum(g[:, q_off:q_off + half], jnp.max(st.reshape(n_keys // 8, 8, half), axis=0))
            p = jnp.exp2(st - bounds[hh][:, q_off:q_off + half]).astype(BF16)
            new[hh, q_off] = (g_part, acc[:, q_off:q_off + half] + _dot(v_ones(hh, k0, n_keys), p))
        return tuple(tuple(jnp.concatenate([new[hh, 0][i], new[hh, half][i]], axis=1) for i in range(2))
                     for hh in range(2))

    done = diagonal_tile(carry) if (kt == blk and n_split == 1 and blk % 16 == 0) else fast_tile(n_tiles - 1, carry, True)
    by_head = [[done[chains.index((hh, c))] for c in range(n_split)] for hh in range(2)]
    o_t = jnp.concatenate(
        [jnp.concatenate([acc[0:HD_ATT] / acc[HD_ATT:HD_ATT + 1] for _, acc in by_head[hh]], axis=1)
         for hh in range(2)], axis=0)
    o_ref[0] = o_t.T.astype(o_ref.dtype)
    slack = functools.reduce(jnp.maximum, [jnp.max(b_parts[i] - jnp.max(done[i][0], axis=0, keepdims=True))
                                           for i in range(len(chains))])

    @pl.when(slack > SLACK_LIMIT)
    def _():
        def exact_tile(j, carry, masked):
            k0 = pl.multiple_of(j * blk, blk)
            kj = kx_ref[0, pl.ds(k0, blk), :]
            out = []
            for hh in range(2):
                m, acc = carry[hh]
                st = _dot_nt(kj, q_heads[hh])
                if masked:
                    st = causal(st, k0, blk)
                m_new = jnp.maximum(m, jnp.max(st, axis=0, keepdims=True))
                p = jnp.exp2(st - m_new).astype(BF16)
                out.append((m_new, jnp.exp2(m - m_new) * acc + _dot(v_ones(hh, k0, blk), p)))
            return tuple(out)

        init_x = tuple((jnp.full((1, blk), NEG_BIG, F32), jnp.zeros((HD_ATT + ONES_ROWS, blk), F32))
                       for _ in range(2))
        carry_x = lax.fori_loop(0, qi, lambda j, c: exact_tile(j, c, False), init_x)
        done_x = exact_tile(qi, carry_x, True)
        o_x = jnp.concatenate([acc[0:HD_ATT] / acc[HD_ATT:HD_ATT + 1] for _, acc in done_x], axis=0)
        o_ref[0] = o_x.T.astype(o_ref.dtype)


def _fox_prompt(qx, kx, vtb, c2t, kn, blk, kn_blk):
    nb, l, _ = qx.shape
    n_pairs = D_ATT // LANES
    kt = blk if blk >= 1024 or l % (2 * blk) else 2 * blk
    assert blk % kn_blk == 0 and l % kt == 0
    kn_per_blk = blk // kn_blk
    pair_rows = lambda a: a.reshape(nb, n_pairs, 2, a.shape[-1])
    return pl.pallas_call(
        functools.partial(_fox_prompt_kernel, blk=blk, kt=kt, n_split=1),
        out_shape=jax.ShapeDtypeStruct((nb, l, D_ATT), BF16),
        grid=(nb, n_pairs, l // blk),
        in_specs=[pl.BlockSpec((1, blk, 2 * LANES), lambda b, p, i: (b, i, p)),
                  pl.BlockSpec((1, l, 2 * LANES), lambda b, p, i: (b, 0, p)),
                  pl.BlockSpec((1, LANES, l), lambda b, p, i: (b, p, 0)),
                  pl.BlockSpec((1, 1, 2, blk), lambda b, p, i: (b, p, 0, i)),
                  pl.BlockSpec((1, 1, 2, l), lambda b, p, i: (b, p, 0, 0)),
                  pl.BlockSpec((1, 1, 2, LANES), lambda b, p, i: (b, p, 0, (i + 1) * kn_per_blk - 1))],
        out_specs=pl.BlockSpec((1, blk, LANES), lambda b, p, i: (b, i, p)),
        compiler_params=_cparams(3),
        name="fox_prompt",
    )(qx, kx, vtb, pair_rows(c2t), pair_rows(c2t), pair_rows(kn))


def _ssd_gate_norm(y, z, g):
    return _rms(y * _silu(z), g)


def _ssd_chunk(xb, z, dt, dtt, prev, state_old, cw, cb, alx, alc, dsx, g_norm, tril, triu):
    q = xb.shape[0]
    row8 = lax.broadcasted_iota(jnp.int32, (8, CONV_CH), 0)
    acc = cb + xb * cw[CONV_WIDTH - 1:CONV_WIDTH, :]
    for back in range(1, CONV_WIDTH):
        sh = pltpu.roll(xb, back, 0)
        head = jnp.where(row8 < back, pltpu.roll(prev, back, 0), sh[0:8])
        shifted = jnp.concatenate([head, sh[8:]], axis=0)
        acc = acc + shifted * cw[CONV_WIDTH - 1 - back:CONV_WIDTH - back, :]
    act = _silu(acc)
    xs = act[:, :D_SSM]
    bmat = act[:, D_SSM:D_SSM + N_BC_GROUPS * N_STATE]
    cmat = act[:, D_SSM + N_BC_GROUPS * N_STATE:]

    a_x = -jnp.exp(alx) * dt
    hi, mid, lo = _split3(a_x)
    parts = jnp.concatenate([hi, mid, lo], axis=1).astype(BF16)
    r = _dot(tril, parts)
    acs_x = r[:, 0:D_SSM] + r[:, D_SSM:2 * D_SSM] + r[:, 2 * D_SSM:3 * D_SSM]
    a_t = -jnp.exp(alc) * dtt
    acs_t = _cumsum_lanes(a_t, triu)

    xdt = xs * dt
    a_last = acs_x[q - 1:q, :]
    xdt_end = (xdt * jnp.exp(a_last - acs_x)).astype(BF16)
    e_acs = jnp.exp(acs_x)
    xdt_b = xdt.astype(BF16)
    state_b = state_old.astype(BF16)

    row = lax.broadcasted_iota(jnp.int32, (q, q), 0)
    col = lax.broadcasted_iota(jnp.int32, (q, q), 1)
    causal = row >= col
    first = lax.broadcasted_iota(jnp.int32, (q, LANES), 1) < P_SSM

    y_parts, state_new = [], []
    for g in range(N_BC_GROUPS):
        cg = cmat[:, g * N_STATE:(g + 1) * N_STATE].astype(BF16)
        bg = bmat[:, g * N_STATE:(g + 1) * N_STATE].astype(BF16)
        cbm = _dot_nt(cg, bg)
        gl = slice(g * GROUP_W, (g + 1) * GROUP_W)
        y_off = _dot(cg, state_b[:, gl]) * e_acs[:, gl]
        for pr in range(GROUP_W // LANES):
            lo_lane = g * GROUP_W + pr * LANES
            pair = []
            for hh in range(2):
                h = lo_lane // P_SSM + hh
                seg = acs_x[:, h * P_SSM:h * P_SSM + 1] - acs_t[h:h + 1, :]
                decay = jnp.exp(jnp.where(causal, seg, -jnp.inf))
                gm = (cbm * decay).astype(BF16)
                pair.append(_dot(gm, xdt_b[:, lo_lane:lo_lane + LANES]))
            y_parts.append(jnp.where(first, pair[0], pair[1]) + y_off[:, pr * LANES:(pr + 1) * LANES])
        state_new.append(jnp.exp(a_last[:, gl]) * state_old[:, gl] + _dot_tn(bg, xdt_end[:, gl]))
    y = jnp.concatenate(y_parts, axis=1) + dsx * xs
    return (_ssd_gate_norm(y, z, g_norm).astype(BF16), xb[q - 8:q], jnp.concatenate(state_new, axis=1))


def _ssd_prompt_kernel(xbc_ref, z_ref, dtx_ref, dtt_ref, cw_ref, cb_ref, alx_ref, alc_ref,
                       dsx_ref, g_ref, tril_ref, triu_ref,
                       y_ref, st_ref, xp_ref, state_ref):
    c = pl.program_id(1)

    @pl.when(c == 0)
    def _():
        xp_ref[...] = jnp.zeros_like(xp_ref)
        state_ref[...] = jnp.zeros_like(state_ref)

    y, tail, state = _ssd_chunk(xbc_ref[0], z_ref[0], dtx_ref[0], dtt_ref[0], xp_ref[...], state_ref[...],
                                cw_ref[...], cb_ref[...], alx_ref[...], alc_ref[...], dsx_ref[...],
                                g_ref[...], tril_ref[...], triu_ref[...])
    y_ref[0] = y
    xp_ref[...] = tail
    state_ref[...] = state

    @pl.when(c == pl.num_programs(1) - 1)
    def _():
        st_ref[0] = state_ref[...].T


def _ssd_prompt(xbc, z, dtx, dtt, conv_w, conv_b, a_log_x, a_log_c, d_skip_x, norm_g):
    nb, l, _ = xbc.shape
    q = SSD_CHUNK
    tril = jnp.tril(jnp.ones((q, q), F32)).astype(BF16)
    triu = jnp.triu(jnp.ones((q, q), F32)).astype(BF16)
    tok = lambda w: pl.BlockSpec((1, q, w), lambda b, c: (b, c, 0))
    const = lambda shape: pl.BlockSpec(shape, lambda b, c: (0,) * len(shape))
    return pl.pallas_call(
        _ssd_prompt_kernel,
        out_shape=(jax.ShapeDtypeStruct((nb, l, D_SSM), BF16),
                   jax.ShapeDtypeStruct((nb, D_SSM, N_STATE), F32)),
        grid=(nb, l // q),
        in_specs=[tok(CONV_CH), tok(D_SSM), tok(D_SSM),
                  pl.BlockSpec((1, H_SSM, q), lambda b, c: (b, 0, c)),
                  const((CONV_WIDTH, CONV_CH)), const((1, CONV_CH)), const((1, D_SSM)),
                  const((H_SSM, 1)), const((1, D_SSM)), const((1, D_SSM)),
                  const((q, q)), const((q, q))],
        out_specs=(tok(D_SSM), pl.BlockSpec((1, D_SSM, N_STATE), lambda b, c: (b, 0, 0))),
        scratch_shapes=[pltpu.VMEM((8, CONV_CH), F32), pltpu.VMEM((N_STATE, D_SSM), F32)],
        compiler_params=_cparams(2),
        name="ssd_prompt",
    )(xbc, z, dtx, dtt, conv_w, conv_b, a_log_x, a_log_c, d_skip_x, norm_g, tril, triu)


def _mix_out_kernel(x_ref, att_ref, ssm_ref, wo_ref, g_ref, wq_ref, x1_ref, qc_ref):
    x1 = x_ref[...] + _dot(att_ref[...], wo_ref[0:D_ATT, :]) + _dot(ssm_ref[...], wo_ref[D_ATT:, :])
    x1_ref[...] = x1
    h = _rms(x1, g_ref[...]).astype(BF16)
    qc_ref[...] = (_dot(h, wq_ref[...]) * X_SCALE).astype(BF16)


def _mix_out(x, att, ssm, w_out, g_cross, w_cq, bt):
    t = x.shape[0]
    tok = lambda w: pl.BlockSpec((bt, w), lambda i: (i, 0))
    const = lambda shape: pl.BlockSpec(shape, lambda i: (0,) * len(shape))
    return pl.pallas_call(
        _mix_out_kernel,
        out_shape=(jax.ShapeDtypeStruct((t, D_MODEL), F32), jax.ShapeDtypeStruct((t, D_MODEL), BF16)),
        grid=(t // bt,),
        in_specs=[tok(D_MODEL), tok(D_ATT), tok(D_SSM), const((D_MODEL, D_MODEL)),
                  const((1, D_MODEL)), const((D_MODEL, D_MODEL))],
        out_specs=(tok(D_MODEL), tok(D_MODEL)),
        compiler_params=_cparams(1),
        name="mix_out",
    )(x, att, ssm, w_out, g_cross, w_cq)


def _ffn_kernel(x1_ref, o_ref, wco_ref, g_ref, wg_ref, wu_ref, wd_ref, gf_ref, y_ref):
    x2 = x1_ref[...] + _dot(o_ref[...], wco_ref[...])
    h = _rms(x2, g_ref[...]).astype(BF16)
    u = (_silu(_dot(h, wg_ref[...])) * _dot(h, wu_ref[...])).astype(BF16)
    x3 = x2 + _dot(u, wd_ref[...])
    y_ref[...] = _rms(x3, gf_ref[...])


def _ffn(x1, o, w_co, g_ffn, w_gate, w_up, w_down, g_final, bt):
    t = x1.shape[0]
    d_ff = w_gate.shape[1]
    tok = lambda w: pl.BlockSpec((bt, w), lambda i: (i, 0))
    const = lambda shape: pl.BlockSpec(shape, lambda i: (0,) * len(shape),
                                       pipeline_mode=pl.Buffered(1))
    return pl.pallas_call(
        _ffn_kernel,
        out_shape=jax.ShapeDtypeStruct((t, D_MODEL), F32),
        grid=(t // bt,),
        in_specs=[tok(D_MODEL), tok(D_MODEL), const((D_MODEL, D_MODEL)), const((1, D_MODEL)),
                  const((D_MODEL, d_ff)), const((D_MODEL, d_ff)), const((d_ff, D_MODEL)),
                  const((1, D_MODEL))],
        out_specs=tok(D_MODEL),
        compiler_params=_cparams(1),
        name="cross_out_ffn",
    )(x1, o, w_co, g_ffn, w_gate, w_up, w_down, g_final)


def _prompt_tail_kernel(x_ref, att_ref, ssm_ref, mk_ref, mv_ref, wo_ref, gc_ref, wq_ref, wco_ref, gf_ref,
                        wg_ref, wu_ref, wd_ref, gfin_ref, y_ref):
    x1 = x_ref[0] + _dot(att_ref[0], wo_ref[0:D_ATT, :]) + _dot(ssm_ref[0], wo_ref[D_ATT:, :])
    qc = (_dot(_rms(x1, gc_ref[...]).astype(BF16), wq_ref[...]) * X_SCALE).astype(BF16)
    heads = []
    for h in range(H_X):
        hl = slice(h * HD_X, (h + 1) * HD_X)
        p = _softmax_rows(_dot_nt(qc[:, hl], mk_ref[0, :, hl]))
        heads.append(_dot(p.astype(BF16), mv_ref[0, :, hl]).astype(BF16))
    x2 = x1 + _dot(jnp.concatenate(heads, axis=1), wco_ref[...])
    h2 = _rms(x2, gf_ref[...]).astype(BF16)
    u = (_silu(_dot(h2, wg_ref[...])) * _dot(h2, wu_ref[...])).astype(BF16)
    y_ref[0] = _rms(x2 + _dot(u, wd_ref[...]), gfin_ref[...])


def _prompt_tail(x, att, ssm, mk, mv, w_out, g_cross, w_cq, w_co, g_ffn, w_gate, w_up, w_down, g_final, bt):
    nb, l, _ = x.shape
    m = mk.shape[1]
    d_ff = w_gate.shape[1]
    tok = lambda w: pl.BlockSpec((1, bt, w), lambda b, i: (b, i, 0))
    mem = pl.BlockSpec((1, m, D_MODEL), lambda b, i: (b, 0, 0))
    const = lambda shape: pl.BlockSpec(shape, lambda b, i: (0,) * len(shape), pipeline_mode=pl.Buffered(1))
    return pl.pallas_call(
        _prompt_tail_kernel,
        out_shape=jax.ShapeDtypeStruct((nb, l, D_MODEL), F32),
        grid=(nb, l // bt),
        in_specs=[tok(D_MODEL), tok(D_ATT), tok(D_SSM), mem, mem, const((D_MODEL, D_MODEL)),
                  const((1, D_MODEL)), const((D_MODEL, D_MODEL)), const((D_MODEL, D_MODEL)), const((1, D_MODEL)),
                  const((D_MODEL, d_ff)), const((D_MODEL, d_ff)), const((d_ff, D_MODEL)), const((1, D_MODEL))],
        out_specs=tok(D_MODEL),
        compiler_params=_cparams(2),
        name="prompt_tail",
    )(x, att, ssm, mk, mv, w_out, g_cross, w_cq, w_co, g_ffn, w_gate, w_up, w_down, g_final)


def _memory_kv_kernel(mem_ref, g_ref, wk_ref, wv_ref, k32_ref, v32_ref, kb_ref, vb_ref):
    mn = _rms(mem_ref[0], g_ref[...]).astype(BF16)
    k = _dot(mn, wk_ref[...])
    v = _dot(mn, wv_ref[...])
    k32_ref[0] = k
    v32_ref[0] = v
    kb_ref[0] = k.astype(BF16)
    vb_ref[0] = v.astype(BF16)


def _memory_kv(mem, g, w_ck, w_cv):
    nb, m, _ = mem.shape
    blk = pl.BlockSpec((1, m, D_MODEL), lambda b: (b, 0, 0))
    const = lambda shape: pl.BlockSpec(shape, lambda b: (0,) * len(shape))
    f = jax.ShapeDtypeStruct((nb, m, D_MODEL), F32)
    h = jax.ShapeDtypeStruct((nb, m, D_MODEL), BF16)
    return pl.pallas_call(
        _memory_kv_kernel,
        out_shape=(f, f, h, h),
        grid=(nb,),
        in_specs=[blk, const((1, D_MODEL)), const((D_MODEL, D_MODEL)), const((D_MODEL, D_MODEL))],
        out_specs=(blk, blk, blk, blk),
        compiler_params=_cparams(1),
        name="memory_kv",
    )(mem, g, w_ck, w_cv)


def _softmax_rows(s):
    m = jnp.max(s, axis=-1, keepdims=True)
    p = jnp.exp(s - m)
    return p / jnp.sum(p, axis=-1, keepdims=True)


def _cross_prompt_kernel(q_ref, k_ref, v_ref, o_ref):
    for h in range(H_X):
        hl = slice(h * HD_X, (h + 1) * HD_X)
        p = _softmax_rows(_dot_nt(q_ref[0, :, hl], k_ref[0, :, hl]))
        o_ref[0, :, hl] = _dot(p.astype(BF16), v_ref[0, :, hl]).astype(o_ref.dtype)


def _cross_prompt(qc, mk, mv, bt):
    nb, l, _ = qc.shape
    m = mk.shape[1]
    tok = pl.BlockSpec((1, bt, D_MODEL), lambda b, i: (b, i, 0))
    mem = pl.BlockSpec((1, m, D_MODEL), lambda b, i: (b, 0, 0))
    return pl.pallas_call(
        _cross_prompt_kernel,
        out_shape=jax.ShapeDtypeStruct((nb, l, D_MODEL), BF16),
        grid=(nb, l // bt),
        in_specs=[tok, mem, mem],
        out_specs=tok,
        compiler_params=_cparams(2),
        name="cross_prompt",
    )(qc, mk, mv)


_X_SUB = 2 * H_X


def _cross_sample_kernel(q_ref, k_ref, v_ref, o_ref, *, lq):
    nsb, n_mem = k_ref.shape[0], k_ref.shape[1]
    n = n_mem * _X_SUB
    rows = lq * _X_SUB
    n_tiles = n // LANES
    cls = lax.broadcasted_iota(jnp.int32, (rows, LANES), 1) % _X_SUB
    own = cls == lax.broadcasted_iota(jnp.int32, (rows, LANES), 0) % _X_SUB
    valid = cls < H_X

    class_steps = (8, 16, 32, 64)

    def per_token(x):
        x3 = x.reshape(lq, _X_SUB, LANES)
        return jnp.broadcast_to(jnp.sum(x3, axis=1, keepdims=True), x3.shape).reshape(rows, LANES)

    seqs = range(nsb)
    s5 = [_dot_nt(q_ref[i], k_ref[i].reshape(n, LANES).astype(BF16)) for i in seqs]
    s_t = [[per_token(jnp.where(own, s5[i][:, j * LANES:(j + 1) * LANES], 0.0)) for j in range(n_tiles)]
           for i in seqs]
    s_t = [[u + pltpu.roll(u, LANES - H_X, 1) for u in s_t[i]] for i in seqs]
    m = [functools.reduce(jnp.maximum, s_t[i]) for i in seqs]
    for sh in class_steps:
        m = [jnp.maximum(x, pltpu.roll(x, sh, 1)) for x in m]
    p_t = [[jnp.where(valid, jnp.exp(s - m[i]), 0.0) for s in s_t[i]] for i in seqs]
    l = [functools.reduce(jnp.add, p_t[i]) for i in seqs]
    for sh in class_steps:
        l = [x + pltpu.roll(x, sh, 1) for x in l]
    inv = [1.0 / jnp.where(valid, l[i], 1.0) for i in seqs]
    pn = [[p * inv[i] for p in p_t[i]] for i in seqs]
    p5 = [jnp.concatenate([jnp.where(own, x + pltpu.roll(x, H_X, 1), 0.0).astype(BF16) for x in pn[i]], axis=1)
          for i in seqs]
    for i in seqs:
        v5 = v_ref[i].reshape(n, LANES).astype(BF16)
        o_ref[i] = _dot(p5[i], v5).astype(o_ref.dtype)


def _cross_sample(q5, mem_k, mem_v, lq, n_seq_blk):
    ns, rows, _ = q5.shape
    m = mem_k.shape[1]
    tok = pl.BlockSpec((n_seq_blk, rows, LANES), lambda i: (i, 0, 0))
    mem = pl.BlockSpec((n_seq_blk, m, _X_SUB, LANES), lambda i: (i, 0, 0, 0))
    return pl.pallas_call(
        functools.partial(_cross_sample_kernel, lq=lq),
        out_shape=jax.ShapeDtypeStruct((ns, rows, LANES), BF16),
        grid=(ns // n_seq_blk,),
        in_specs=[tok, mem, mem],
        out_specs=tok,
        compiler_params=_cparams(1),
        name="cross_sample",
    )(q5, mem_k, mem_v)


def _fox_sample_kernel(pt_ref, q_ref, kn_ref, vn_ref, lfn_ref, tril_ref, *rest, n_pages, lq, nsb):
    del pt_ref
    per_seq = 3 * n_pages
    k_refs = [rest[u * per_seq:u * per_seq + n_pages] for u in range(nsb)]
    v_refs = [rest[u * per_seq + n_pages:u * per_seq + 2 * n_pages] for u in range(nsb)]
    lf_refs = [rest[u * per_seq + 2 * n_pages:(u + 1) * per_seq] for u in range(nsb)]
    o_ref, kt_ref, vt_ref = rest[nsb * per_seq:nsb * per_seq + 3]
    seqs = range(nsb)

    for u in seqs:
        for j in range(n_pages):
            kt_ref[u, :, :, j * PAGE_SIZE:(j + 1) * PAGE_SIZE] = k_refs[u][j][0].astype(BF16)
            vt_ref[u, :, :, j * PAGE_SIZE:(j + 1) * PAGE_SIZE] = v_refs[u][j][0].astype(BF16)

    tril = tril_ref[...]
    lf_all = [jnp.concatenate([r[0] for r in lf_refs[u]], axis=0) * LOG2E for u in seqs]
    splits = [_split3(x) for x in lf_all]
    incl = [_dot(hi.astype(BF16), tril) + _dot(mid.astype(BF16), tril) + _dot(lo.astype(BF16), tril)
            for hi, mid, lo in splits]
    bias = []
    for u in seqs:
        after = jnp.zeros((H_ATT, 1), F32)
        pieces = [None] * n_pages
        for j in reversed(range(n_pages)):
            sl = slice(j * H_ATT, (j + 1) * H_ATT)
            pieces[j] = incl[u][sl] - lf_all[u][sl] + after
            after = after + incl[u][sl][:, 0:1]
        bias.append(jnp.concatenate(pieces, axis=1))

    lane = lax.broadcasted_iota(jnp.int32, (H_ATT, lq), 1)
    cn = []
    for u in seqs:
        lfn = lfn_ref[u] * LOG2E
        c = jnp.zeros((H_ATT, lq), F32)
        for t in range(lq):
            c = c + jnp.where(lane >= t, lfn[:, t:t + 1], 0.0)
        cn.append(c)
    causal = (lax.broadcasted_iota(jnp.int32, (lq, lq), 1)
              <= lax.broadcasted_iota(jnp.int32, (lq, lq), 0))

    units = [(u, h) for u in seqs for h in range(H_ATT)]
    scores = [(_dot(q_ref[u, h], kt_ref[u, h]), _dot_nt(q_ref[u, h], kn_ref[u, h])) for u, h in units]
    probs = []
    for (u, h), (s, s_new) in zip(units, scores):
        s = s + bias[u][h:h + 1, :]
        s_new = jnp.where(causal, s_new - cn[u][h:h + 1, :], NEG_BIG)
        m = jnp.maximum(jnp.max(s, axis=-1, keepdims=True), jnp.max(s_new, axis=-1, keepdims=True))
        p = jnp.exp2(s - m)
        p_new = jnp.exp2(s_new - m)
        l = jnp.sum(p, axis=-1, keepdims=True) + jnp.sum(p_new, axis=-1, keepdims=True)
        probs.append((p.astype(BF16), p_new.astype(BF16), l))
    for (u, h), (p, p_new, l) in zip(units, probs):
        o = _dot_nt(p, vt_ref[u, h]) + _dot(p_new, vn_ref[u, h])
        o_ref[u, h] = (o / l).astype(o_ref.dtype)


def _fox_sample(qh, kh, vh, lfn, cache_kt, cache_vt, cache_lft, page_table, nsb):
    ns, _, lq, _ = qh.shape
    n_pages = page_table.shape[1]
    past = n_pages * PAGE_SIZE
    tril = jnp.tril(jnp.ones((PAGE_SIZE, PAGE_SIZE), F32)).astype(BF16)
    tok = pl.BlockSpec((nsb, H_ATT, lq, HD_ATT), lambda i, pt: (i, 0, 0, 0))

    def page(u, j, shape):
        return pl.BlockSpec((1,) + shape,
                            lambda i, pt: (pt[(i * nsb + u) * n_pages + j],) + (0,) * len(shape))

    in_specs = [tok, tok, tok, pl.BlockSpec((nsb, H_ATT, lq), lambda i, pt: (i, 0, 0)),
                pl.BlockSpec((PAGE_SIZE, PAGE_SIZE), lambda i, pt: (0, 0))]
    pages = []
    for u in range(nsb):
        in_specs += [page(u, j, (H_ATT, HD_ATT, PAGE_SIZE)) for j in range(n_pages)]
        in_specs += [page(u, j, (H_ATT, HD_ATT, PAGE_SIZE)) for j in range(n_pages)]
        in_specs += [page(u, j, (H_ATT, PAGE_SIZE)) for j in range(n_pages)]
        pages += [cache_kt] * n_pages + [cache_vt] * n_pages + [cache_lft] * n_pages
    grid_spec = pltpu.PrefetchScalarGridSpec(
        num_scalar_prefetch=1,
        grid=(ns // nsb,),
        in_specs=in_specs,
        out_specs=tok,
        scratch_shapes=[pltpu.VMEM((nsb, H_ATT, HD_ATT, past), BF16),
                        pltpu.VMEM((nsb, H_ATT, HD_ATT, past), BF16)],
    )
    return pl.pallas_call(
        functools.partial(_fox_sample_kernel, n_pages=n_pages, lq=lq, nsb=nsb),
        out_shape=jax.ShapeDtypeStruct((ns, H_ATT, lq, HD_ATT), BF16),
        grid_spec=grid_spec,
        compiler_params=_cparams(1),
        name="fox_sample",
    )(page_table.reshape(-1), qh, kh, vh, lfn, tril, *pages)


def _ssd_sample_kernel(xbc_ref, z_ref, dtx_ref, cs_ref, st_ref, cw_ref, cb_ref, alx_ref, dsx_ref,
                       g_ref, y_ref, sto_ref, xp_ref, rows_ref, brow_ref, *, lq):
    nsb = xbc_ref.shape[0]
    assert nsb * (lq + 1) <= LANES
    kw = CONV_WIDTH - 1
    xp_ref[:, 0:kw, :] = cs_ref[...]
    xp_ref[:, kw:kw + lq, :] = xbc_ref[...]
    acc = cb_ref[...]
    for tap in range(CONV_WIDTH):
        acc = acc + xp_ref[:, tap:tap + lq, :] * cw_ref[tap:tap + 1, :]
    act = _silu(acc)
    xs = act[:, :, :D_SSM]
    bmat = act[:, :, D_SSM:D_SSM + N_BC_GROUPS * N_STATE]
    cmat = act[:, :, D_SSM + N_BC_GROUPS * N_STATE:]
    dt = dtx_ref[...]
    a = -jnp.exp(alx_ref[...]) * dt
    xdt = xs * dt
    acs = [a[:, 0:1, :]]
    for t in range(1, lq):
        acs.append(acs[-1] + a[:, t:t + 1, :])
    a_last = acs[lq - 1]

    ccat = jnp.concatenate([cmat[:, :, g * N_STATE:(g + 1) * N_STATE] for g in range(N_BC_GROUPS)],
                           axis=1).astype(BF16)
    state = st_ref[...]
    r = jnp.einsum("sgn,shn->sgh", ccat, state.astype(BF16), preferred_element_type=F32)
    group0 = lax.broadcasted_iota(jnp.int32, (1, 1, D_SSM), 2) < GROUP_W
    y_off = jnp.where(group0, r[:, 0:lq, :], r[:, lq:2 * lq, :])

    ys = []
    for t in range(lq):
        y_t = jnp.exp(acs[t]) * y_off[:, t:t + 1, :] + dsx_ref[...] * xs[:, t:t + 1, :]
        for s in range(t + 1):
            cb = jnp.concatenate(
                [jnp.broadcast_to(
                    jnp.sum(cmat[:, t:t + 1, g * N_STATE:(g + 1) * N_STATE]
                            * bmat[:, s:s + 1, g * N_STATE:(g + 1) * N_STATE], axis=-1, keepdims=True),
                    (nsb, 1, GROUP_W)) for g in range(N_BC_GROUPS)], axis=2)
            y_t = y_t + cb * jnp.exp(acs[t] - acs[s]) * xdt[:, s:s + 1, :]
        ys.append(y_t)
    y = jnp.concatenate(ys, axis=1)
    y_ref[...] = _ssd_gate_norm(y, z_ref[...], g_ref[...]).astype(y_ref.dtype)

    upd = jnp.concatenate([xdt[:, s:s + 1, :] * jnp.exp(a_last - acs[s]) for s in range(lq)], axis=1)
    e_last = jnp.exp(a_last)
    rows_ref[...] = jnp.zeros_like(rows_ref)
    brow_ref[...] = jnp.zeros_like(brow_ref)
    for i in range(nsb):
        rows_ref[i * lq:(i + 1) * lq, :] = upd[i]
        rows_ref[nsb * lq + i:nsb * lq + i + 1, :] = e_last[i]
        brow_ref[i * lq:(i + 1) * lq, :] = bmat[i]
    cols = rows_ref[...].T
    upd_cols = cols.astype(BF16)
    b_all = brow_ref[...]
    row_seq = lax.broadcasted_iota(jnp.int32, (LANES, N_BC_GROUPS * N_STATE), 0) // lq
    top = lax.broadcasted_iota(jnp.int32, (D_SSM, N_STATE), 0) < GROUP_W
    for i in range(nsb):
        b_rows = jnp.where(row_seq == i, b_all, 0.0).astype(BF16)
        m = _dot(upd_cols, b_rows)
        add = jnp.where(top, m[:, 0:N_STATE], m[:, N_STATE:2 * N_STATE])
        decay = jnp.broadcast_to(cols[:, nsb * lq + i:nsb * lq + i + 1], (D_SSM, N_STATE))
        sto_ref[i] = decay * state[i] + add


def _ssd_sample(xbc, z, dtx, conv_state, state, conv_w, conv_b, a_log_x, d_skip_x, norm_g, nsb):
    ns, lq, _ = xbc.shape
    seq = lambda r, w: pl.BlockSpec((nsb, r, w), lambda i: (i, 0, 0))
    const = lambda shape: pl.BlockSpec(shape, lambda i: (0,) * len(shape))
    return pl.pallas_call(
        functools.partial(_ssd_sample_kernel, lq=lq),
        out_shape=(jax.ShapeDtypeStruct((ns, lq, D_SSM), BF16),
                   jax.ShapeDtypeStruct((ns, D_SSM, N_STATE), F32)),
        grid=(ns // nsb,),
        in_specs=[seq(lq, CONV_CH), seq(lq, D_SSM), seq(lq, D_SSM), seq(CONV_WIDTH - 1, CONV_CH),
                  seq(D_SSM, N_STATE), const((CONV_WIDTH, CONV_CH)), const((1, CONV_CH)),
                  const((1, D_SSM)), const((1, D_SSM)), const((1, D_SSM))],
        out_specs=(seq(lq, D_SSM), seq(D_SSM, N_STATE)),
        scratch_shapes=[pltpu.VMEM((nsb, 8, CONV_CH), F32), pltpu.VMEM((LANES, D_SSM), F32),
                        pltpu.VMEM((LANES, N_BC_GROUPS * N_STATE), F32)],
        compiler_params=_cparams(1),
        name="ssd_sample",
    )(xbc, z, dtx, conv_state, state, conv_w, conv_b, a_log_x, d_skip_x, norm_g)


def _pick(pref, n):
    return pref if n % pref == 0 else n


def kernel(x_prompt, x_sample, mem_prompt, cache_k, cache_v, cache_logf, page_table, cache_mem_k, cache_mem_v, state_conv, state_ssm, norm_mix_g, w_in, b_forget, conv_w, conv_b, dt_bias, a_log, d_skip, ssm_norm_g, w_out, norm_cross_g, norm_mem_g, w_cq, w_ck, w_cv, w_co, norm_ffn_g, w_gate, w_up, w_down, final_norm_g):
    assert w_in.shape[0] == 1, "one layer"
    nb, l, _ = x_prompt.shape
    ns, lq, _ = x_sample.shape
    n_phys = cache_k.shape[1]
    row = lambda v: v.reshape(1, -1).astype(F32)
    colv = lambda v: v.reshape(-1, 1).astype(F32)
    per_ch = lambda v: jnp.repeat(v.astype(F32), P_SSM).reshape(1, D_SSM)

    w_t = w_in[0].T
    cuts = [D_ATT, 2 * D_ATT, 3 * D_ATT, 3 * D_ATT + H_ATT, 3 * D_ATT + H_ATT + D_SSM,
            3 * D_ATT + H_ATT + D_SSM + CONV_CH]
    w_q, w_k, w_v, w_f, w_z, w_xbc, w_dt = jnp.split(w_t, cuts, axis=0)
    lane_pad = lambda a: jnp.pad(a, ((0, 0), (0, LANES - a.shape[1])))
    w_main_t = jnp.concatenate([w_q, w_k, w_z, w_xbc, jnp.repeat(w_dt, P_SSM, axis=0),
                                jnp.pad(w_f, ((0, LANES - H_ATT), (0, 0)))], axis=0).astype(BF16)
    w_small_t = jnp.concatenate([w_f, w_dt], axis=0).astype(BF16)
    in_params = (row(norm_mix_g[0]), w_main_t, w_v.astype(BF16), w_small_t, colv(b_forget[0]),
                 colv(dt_bias[0]), per_ch(dt_bias[0]), lane_pad(row(b_forget[0])))
    ssd_params = (conv_w[0], row(conv_b[0]), per_ch(a_log[0]))
    ssd_tail = (per_ch(d_skip[0]), row(ssm_norm_g[0]))
    w_out_b, w_cq_b, w_co_b = w_out[0].astype(BF16), w_cq[0].astype(BF16), w_co[0].astype(BF16)
    w_gate_b, w_up_b, w_down_b = w_gate[0].astype(BF16), w_up[0].astype(BF16), w_down[0].astype(BF16)
    g_cross, g_ffn, g_final = row(norm_cross_g[0]), row(norm_ffn_g[0]), row(final_norm_g)

    def tail(x, att, ssm, cross, bt):
        x1, qc = _mix_out(x, att, ssm, w_out_b, g_cross, w_cq_b, _pick(2 * bt, x.shape[0]))
        o = cross(qc)
        return _ffn(x1, o, w_co_b, g_ffn, w_gate_b, w_up_b, w_down_b, g_final, bt)

    bt = _pick(512, l)
    assert bt % SSD_CHUNK == 0
    pp = _in_projection(x_prompt, *in_params, bt, ssd=(*ssd_params, colv(a_log[0]), *ssd_tail))
    k32, v32t, logft, ssm, st_p = pp["k32"], pp["v32t"], pp["logft"], pp["yssm"], pp["state"]
    att = _fox_prompt(pp["qx"], pp["kx"], pp["vtb"], pp["c2t"], pp["kn"], _pick(1024, l), bt)
    mk32, mv32, mkb, mvb = _memory_kv(mem_prompt, row(norm_mem_g[0]), w_ck[0].astype(BF16),
                                      w_cv[0].astype(BF16))
    y_prompt = _prompt_tail(x_prompt, att, ssm, mkb, mvb, w_out_b, g_cross, w_cq_b, w_co_b, g_ffn,
                            w_gate_b, w_up_b, w_down_b, g_final, bt)

    ts = ns * lq
    bts = _pick(512, ts)
    ps = _in_projection(x_sample.reshape(1, ts, D_MODEL), *in_params, bts)
    qx_s, k32_s, vtb_s, v32t_s = ps["qx"], ps["k32"], ps["vtb"], ps["v32t"]
    z_s, xbc_s, dtx_s, logft_s = ps["z"], ps["xbc"], ps["dtx"], ps["logft"]
    seq3 = lambda a: a.reshape(ns, lq, a.shape[-1])
    lfn = logft_s.reshape(H_ATT, ns, lq).transpose(1, 0, 2)
    head_major = lambda a: a.reshape(ns, lq, H_ATT, HD_ATT).transpose(0, 2, 1, 3)
    from_t = lambda a: a.reshape(H_ATT, HD_ATT, ns, lq)
    qb_s = qx_s.reshape(ts, D_ATT // LANES, 2, LANES)[:, :, 0, :]
    att_s = _fox_sample(head_major(qb_s), head_major(k32_s.astype(BF16)),
                        from_t(vtb_s).transpose(2, 0, 3, 1), lfn,
                        cache_k[0].transpose(0, 2, 3, 1), cache_v[0].transpose(0, 2, 3, 1),
                        cache_logf[0].transpose(0, 2, 1), page_table, _pick(2, ns))
    att_s = att_s.transpose(0, 2, 1, 3)
    ssm_s, st_s = _ssd_sample(seq3(xbc_s), seq3(z_s), seq3(dtx_s), state_conv[0],
                              state_ssm[0].reshape(ns, D_SSM, N_STATE), *ssd_params, *ssd_tail,
                              _pick(8, ns))
    n_mem = mem_prompt.shape[1]
    stored = lambda a: a.reshape(ns, -1, H_X, 2, LANES).transpose(0, 1, 3, 2, 4).reshape(ns, -1, _X_SUB, LANES)
    cross_s = lambda qc: _cross_sample(
        stored(qc).reshape(ns, lq * _X_SUB, LANES), stored(cache_mem_k[0]), stored(cache_mem_v[0]), lq,
        _pick(4, ns)).reshape(ns, lq, 2, H_X, LANES).transpose(0, 1, 3, 2, 4).reshape(ts, D_MODEL)
    y_sample = tail(x_sample.reshape(ts, D_MODEL), att_s.reshape(ts, D_ATT), ssm_s.reshape(ts, D_SSM),
                    cross_s, bts).reshape(ns, lq, D_MODEL)

    heads = lambda a, n: a.reshape(1, n, -1, H_ATT, HD_ATT)
    kw = CONV_WIDTH - 1
    return (y_prompt, y_sample,
            heads(k32, nb), v32t.reshape(nb, H_ATT, HD_ATT, l).transpose(0, 3, 1, 2)[None],
            logft.transpose(0, 2, 1)[None],
            mk32.reshape(1, nb, n_mem, H_X, HD_X), mv32.reshape(1, nb, n_mem, H_X, HD_X),
            pp["tail"][:, 8 - kw:, :][None], st_p.reshape(1, nb, H_SSM, P_SSM, N_STATE),
            heads(k32_s, ns), from_t(v32t_s).transpose(2, 3, 0, 1)[None],
            logft_s.reshape(H_ATT, ns, lq).transpose(1, 2, 0)[None],
            seq3(xbc_s)[:, lq - kw:, :][None], st_s.reshape(1, ns, H_SSM, P_SSM, N_STATE))
```

```python
import functools
import math

import numpy as np
import jax
import jax.numpy as jnp
from jax import lax
from jax.experimental import pallas as pl
from jax.experimental.pallas import tpu as pltpu

F32 = jnp.float32
BF16 = jnp.bfloat16

D_MODEL = 1024
D_ATT = 512
H_ATT = 8
HD_ATT = 64
D_SSM = 512
H_SSM = 8
P_SSM = 64
N_STATE = 128
N_BC_GROUPS = 2
GROUP_W = D_SSM // N_BC_GROUPS
CONV_WIDTH = 4
CONV_CH = D_SSM + 2 * N_BC_GROUPS * N_STATE
H_X = 4
HD_X = 256
EPS = 1e-6
ATT_SCALE = HD_ATT ** -0.5
X_SCALE = HD_X ** -0.5
SSD_CHUNK = 128
PAGE_SIZE = 128
NEG_BIG = -1e30
LOG2E = math.log2(math.e)

BOUND_SCALE = 1.03
BOUND_MARGIN = 2.0
SLACK_LIMIT = 80.0
UNDERFLOW_LOG2 = 154.0

LANES = 128
ONES_ROWS = 16
VMEM_LIMIT = 56 * 1024 * 1024

_Q0, _K0, _Z0, _XBC0, _DT0, _F0, _MAIN_COLS = 0, 512, 1024, 1536, 2560, 3072, 3200


def _cparams(n_axes):
    return pltpu.CompilerParams(dimension_semantics=("arbitrary",) * n_axes,
                                vmem_limit_bytes=VMEM_LIMIT)


def _rms(x, g):
    ms = jnp.mean(x * x, axis=-1, keepdims=True)
    return x * lax.rsqrt(ms + EPS) * g


def _softplus(x):
    return jnp.maximum(x, 0.0) + jnp.log1p(jnp.exp(-jnp.abs(x)))


def _log_sigmoid(x):
    return jnp.minimum(x, 0.0) - jnp.log1p(jnp.exp(-jnp.abs(x)))


def _silu(x):
    return x * (1.0 / (1.0 + jnp.exp(-x)))


def _split3(x):
    hi = x.astype(BF16).astype(F32)
    r1 = x - hi
    mid = r1.astype(BF16).astype(F32)
    lo = (r1 - mid).astype(BF16).astype(F32)
    return hi, mid, lo


def _dot(a, b):
    return jnp.dot(a, b, preferred_element_type=F32)


def _dot_nt(a, b):
    return lax.dot_general(a, b, (((1,), (1,)), ((), ())), preferred_element_type=F32)


def _dot_tn(a, b):
    return lax.dot_general(a, b, (((0,), (0,)), ((), ())), preferred_element_type=F32)


def _cumsum_lanes(x, tri_upper):
    hi, mid, lo = _split3(x)
    parts = jnp.concatenate([hi, mid, lo], axis=0).astype(BF16)
    r = _dot(parts, tri_upper)
    return r[0:8] + r[8:16] + r[16:24]


def _inproj_kernel(x_ref, g_ref, wm_ref, wvt_ref, wst_ref, bf_ref, dtb_ref, dtbx_ref, bfx_ref,
                   tril_ref, sel_ref, ones_ref, hsel_ref, *rest, with_ssd):
    if with_ssd:
        (cw_ref, cb_ref, alx_ref, alc_ref, dsx_ref, gssm_ref, trilq_ref, triuq_ref,
         qx_ref, kx_ref, k32_ref, vtb_ref, v32t_ref, logft_ref, c2t_ref, kn_ref, yssm_ref, st_ref, tail_ref,
         carry_ref, carry_t_ref, carry_kn_ref, xp_ref, state_ref) = rest
    else:
        (qx_ref, kx_ref, k32_ref, vtb_ref, v32t_ref, z_ref, xbc_ref, dtx_ref, logft_ref, dtt_ref,
         c2t_ref, kn_ref, carry_ref, carry_t_ref, carry_kn_ref) = rest

    @pl.when(pl.program_id(1) == 0)
    def _():
        carry_ref[...] = jnp.zeros_like(carry_ref)
        carry_t_ref[...] = jnp.zeros_like(carry_t_ref)
        carry_kn_ref[...] = jnp.zeros_like(carry_kn_ref)
        if with_ssd:
            xp_ref[...] = jnp.zeros_like(xp_ref)
            state_ref[...] = jnp.zeros_like(state_ref)

    h = _rms(x_ref[0], g_ref[...]).astype(BF16)
    bt = h.shape[0]
    proj = lambda lo, hi: _dot_nt(h, wm_ref[lo:hi, :])
    xbc = proj(_XBC0, _DT0)
    z = proj(_Z0, _XBC0)
    dtx = _softplus(proj(_DT0, _F0) + dtbx_ref[...])
    small = _dot_nt(wst_ref[...], h)
    dtt = _softplus(small[8:16] + dtb_ref[...])
    if with_ssd:
        ssd_consts = (cw_ref[...], cb_ref[...], alx_ref[...], alc_ref[...], dsx_ref[...], gssm_ref[...],
                      trilq_ref[...], triuq_ref[...])
        ssd_state = (xp_ref[...], state_ref[...])
        n_chunks = bt // SSD_CHUNK

        def ssd(c, carry):
            rows = slice(c * SSD_CHUNK, (c + 1) * SSD_CHUNK)
            y, tail, state = _ssd_chunk(xbc[rows], z[rows], dtx[rows], dtt[:, rows], *carry, *ssd_consts)
            yssm_ref[0, rows, :] = y
            return tail, state
    else:
        z_ref[0], xbc_ref[0], dtx_ref[0], dtt_ref[0] = z, xbc, dtx, dtt
        n_chunks = 0
        ssd = None
    ssd_slots = [[c for c in range(n_chunks) if c * 4 // n_chunks == i] for i in range(4)]

    def ssd_here(slot, carry):
        for c in ssd_slots[slot]:
            carry = ssd(c, carry)
        return carry

    q = (proj(_Q0, _K0) * (ATT_SCALE * LOG2E)).astype(BF16)
    k = proj(_K0, _Z0)
    k32_ref[0] = k
    kb = k.astype(BF16)
    if with_ssd:
        ssd_state = ssd_here(0, ssd_state)
    f_rows = proj(_F0, _MAIN_COLS)

    lf2 = _log_sigmoid(f_rows + bfx_ref[...]) * LOG2E
    parts = jnp.concatenate(_split3(lf2), axis=1).astype(BF16)
    r = _dot(tril_ref[...], parts)
    if with_ssd:
        ssd_state = ssd_here(1, ssd_state)

    logft = _log_sigmoid(small[0:8] + bf_ref[...])
    logft_ref[0] = logft
    hi, mid, lo = _split3(logft * LOG2E)
    r_t = _dot_nt(jnp.concatenate([hi, mid, lo], axis=0).astype(BF16), tril_ref[...])
    if with_ssd:
        ssd_state = ssd_here(2, ssd_state)
    vt = _dot_nt(wvt_ref[...], h)
    v32t_ref[0] = vt
    vtb_ref[0] = vt.astype(BF16)

    c2 = r[:, 0:LANES] + r[:, LANES:2 * LANES] + r[:, 2 * LANES:3 * LANES] + carry_ref[0:1, :]
    carry_ref[...] = jnp.broadcast_to(c2[bt - 1:bt, :], carry_ref.shape)
    csplit = jnp.concatenate(_split3(c2), axis=1).astype(BF16)
    ext = (_dot(csplit, sel_ref[...]) + ones_ref[...]).astype(BF16)
    if with_ssd:
        tail, state = ssd_here(3, ssd_state)
        xp_ref[...] = tail
        state_ref[...] = state
        tail_ref[0] = tail

        @pl.when(pl.program_id(1) == pl.num_programs(1) - 1)
        def _():
            st_ref[0] = state.T

    c2t = r_t[0:8] + r_t[8:16] + r_t[16:24] + carry_t_ref[:, 0:1]
    c2t_ref[0] = c2t
    carry_t_ref[...] = jnp.broadcast_to(c2t[:, c2t.shape[1] - 1:], carry_t_ref.shape)
    kf = kb.astype(F32)
    ksq_t = _dot_nt(hsel_ref[...], (kf * kf).astype(BF16))
    kn = jnp.maximum(carry_kn_ref[...], jnp.max(ksq_t, axis=1, keepdims=True))
    carry_kn_ref[...] = kn
    kn_ref[0] = kn
    for p in range(D_ATT // LANES):
        pl_ = slice(p * LANES, (p + 1) * LANES)
        kx_ref[0, :, 2 * p * LANES:(2 * p + 1) * LANES] = kb[:, pl_]
        kx_ref[0, :, (2 * p + 1) * LANES:(2 * p + 2) * LANES] = ext[:, pl_]
        qx_ref[0, :, 2 * p * LANES:(2 * p + 1) * LANES] = q[:, pl_]
        qx_ref[0, :, (2 * p + 1) * LANES:(2 * p + 2) * LANES] = ext[:, D_ATT + p * LANES:D_ATT + (p + 1) * LANES]


def _attention_extras():
    sel = np.zeros((3 * LANES, 2 * D_ATT), np.float32)
    ones = np.zeros((1, 2 * D_ATT), np.float32)
    for p in range(D_ATT // LANES):
        ones[0, p * LANES + 6:p * LANES + 12] = 1.0
        for hh in range(2):
            for s in range(3):
                sel[s * LANES + 2 * p + hh, p * LANES + 3 * hh + s] = 1.0
                sel[s * LANES + 2 * p + hh, D_ATT + p * LANES + 6 + 3 * hh + s] = 1.0
    return jnp.asarray(sel, BF16), jnp.asarray(ones, F32)


def _in_projection(x, g, w_main, w_vt, w_small_t, b_forget, dt_bias, dt_bias_x, b_forget_x, bt, ssd=None):
    nb, l, _ = x.shape
    tril = jnp.tril(jnp.ones((bt, bt), F32)).astype(BF16)
    sel, ones = _attention_extras()
    tok = lambda w: pl.BlockSpec((1, bt, w), lambda b, j: (b, j, 0))
    tok_t = lambda r: pl.BlockSpec((1, r, bt), lambda b, j: (b, 0, j))
    const = lambda shape: pl.BlockSpec(shape, lambda b, j: (0,) * len(shape))
    per_batch = lambda r, w: pl.BlockSpec((1, r, w), lambda b, j: (b, 0, 0))
    f32 = lambda *s: jax.ShapeDtypeStruct(s, F32)
    bf16 = lambda *s: jax.ShapeDtypeStruct(s, BF16)
    outs = {
        "qx": (bf16(nb, l, 2 * D_ATT), tok(2 * D_ATT)), "kx": (bf16(nb, l, 2 * D_ATT), tok(2 * D_ATT)),
        "k32": (f32(nb, l, D_ATT), tok(D_ATT)),
        "vtb": (bf16(nb, D_ATT, l), tok_t(D_ATT)), "v32t": (f32(nb, D_ATT, l), tok_t(D_ATT)),
        "z": (f32(nb, l, D_SSM), tok(D_SSM)), "xbc": (f32(nb, l, CONV_CH), tok(CONV_CH)),
        "dtx": (f32(nb, l, D_SSM), tok(D_SSM)),
        "logft": (f32(nb, H_ATT, l), tok_t(H_ATT)), "dtt": (f32(nb, H_SSM, l), tok_t(H_SSM)),
        "c2t": (f32(nb, H_ATT, l), tok_t(H_ATT)),
        "kn": (f32(nb, H_ATT, (l // bt) * LANES), pl.BlockSpec((1, H_ATT, LANES), lambda b, j: (b, 0, j))),
        "yssm": (bf16(nb, l, D_SSM), tok(D_SSM)), "state": (f32(nb, D_SSM, N_STATE), per_batch(D_SSM, N_STATE)),
        "tail": (f32(nb, 8, CONV_CH), per_batch(8, CONV_CH)),
    }
    if ssd is None:
        names = ("qx", "kx", "k32", "vtb", "v32t", "z", "xbc", "dtx", "logft", "dtt", "c2t", "kn")
        extra_in, extra_specs, extra_scratch = (), [], []
    else:
        names = ("qx", "kx", "k32", "vtb", "v32t", "logft", "c2t", "kn", "yssm", "state", "tail")
        q = SSD_CHUNK
        extra_in = tuple(ssd) + (jnp.tril(jnp.ones((q, q), F32)).astype(BF16),
                                 jnp.triu(jnp.ones((q, q), F32)).astype(BF16))
        extra_specs = [const((CONV_WIDTH, CONV_CH)), const((1, CONV_CH)), const((1, D_SSM)), const((H_SSM, 1)),
                       const((1, D_SSM)), const((1, D_SSM)), const((q, q)), const((q, q))]
        extra_scratch = [pltpu.VMEM((8, CONV_CH), F32), pltpu.VMEM((N_STATE, D_SSM), F32)]
    head_sel = jnp.asarray(np.repeat(np.eye(H_ATT, dtype=np.float32), HD_ATT, axis=1), BF16)
    res = pl.pallas_call(
        functools.partial(_inproj_kernel, with_ssd=ssd is not None),
        out_shape=tuple(outs[n][0] for n in names),
        grid=(nb, l // bt),
        in_specs=[tok(D_MODEL), const((1, D_MODEL)), const((_MAIN_COLS, D_MODEL)),
                  const((D_ATT, D_MODEL)), const((2 * H_ATT, D_MODEL)), const((H_ATT, 1)),
                  const((H_SSM, 1)), const((1, D_SSM)), const((1, LANES)), const((bt, bt)),
                  const((3 * LANES, 2 * D_ATT)), const((1, 2 * D_ATT)), const((H_ATT, D_ATT))] + extra_specs,
        out_specs=tuple(outs[n][1] for n in names),
        scratch_shapes=[pltpu.VMEM((8, LANES), F32), pltpu.VMEM((H_ATT, LANES), F32),
                        pltpu.VMEM((H_ATT, LANES), F32)] + extra_scratch,
        compiler_params=_cparams(2),
        name="in_projection_ssd" if ssd is not None else "in_projection",
    )(x, g, w_main, w_vt, w_small_t, b_forget, dt_bias, dt_bias_x, b_forget_x, tril, sel, ones, head_sel,
      *extra_in)
    return dict(zip(names, res))


def _fox_prompt_kernel(qx_ref, kx_ref, vt_ref, c2t_ref, c2all_ref, kn_ref, o_ref, *, blk, kt, n_split):
    qi = pl.program_id(2)
    q = qx_ref[0, :, 0:LANES]
    cq = jnp.broadcast_to(qx_ref[0, 0:1, LANES:2 * LANES], (blk, LANES))
    lane = lax.broadcasted_iota(jnp.int32, (blk, LANES), 1)
    first = lane < HD_ATT
    zero = jnp.zeros_like(q)
    minus = jnp.full_like(q, -1.0)
    q_heads, bounds = [], []
    for hh in range(2):
        ext = jnp.where((lane >= 3 * hh) & (lane < 3 * hh + 3), minus,
                        jnp.where((lane >= 6 + 3 * hh) & (lane < 9 + 3 * hh), cq, zero))
        own = jnp.where(first, q, zero) if hh == 0 else jnp.where(first, zero, q)
        q_heads.append(jnp.concatenate([own, ext], axis=1))
        qsq = _dot_nt(jnp.ones((8, LANES), BF16), own * own)[0:1]
        c2q = c2t_ref[0, 0, hh:hh + 1, :]
        bounds.append(BOUND_SCALE * jnp.sqrt(qsq * kn_ref[0, 0, hh:hh + 1, 0:1])
                      + (c2q[:, 0:1] - c2q) + BOUND_MARGIN)

    def v_ones(hh, k0, n):
        vt = vt_ref[0, hh * HD_ATT:(hh + 1) * HD_ATT, pl.ds(k0, n)]
        return jnp.concatenate([vt, jnp.ones((ONES_ROWS, n), BF16)], axis=0)

    def causal(st, k0, n, q_off=0, q_n=blk):
        key = lax.broadcasted_iota(jnp.int32, (n, q_n), 0) + (k0 - qi * blk - q_off)
        qry = lax.broadcasted_iota(jnp.int32, (n, q_n), 1)
        return jnp.where(key <= qry, st, NEG_BIG)

    qw = blk // n_split
    chains = [(hh, c) for c in range(n_split) for hh in range(2)]
    q_parts = [q_heads[hh][c * qw:(c + 1) * qw, :] for hh, c in chains]
    b_parts = [bounds[hh][:, c * qw:(c + 1) * qw] for hh, c in chains]

    def fast_tile(s, carry, masked):
        k0 = pl.multiple_of(s * kt, kt)
        kj = kx_ref[0, pl.ds(k0, kt), :]
        st_next = _dot_nt(kj, q_parts[0])
        out = []
        for i, ((hh, c), (g, acc)) in enumerate(zip(chains, carry)):
            st = st_next
            if i + 1 < len(chains):
                st_next = _dot_nt(kj, q_parts[i + 1])
            if masked:
                st = causal(st, k0, kt, c * qw, qw)
            g = jnp.maximum(g, jnp.max(st.reshape(kt // 8, 8, qw), axis=0))
            p = jnp.exp2(st - b_parts[i]).astype(BF16)
            out.append((g, acc + _dot(v_ones(hh, k0, kt), p)))
        return tuple(out)

    n_tiles = (qi * blk + blk + kt - 1) // kt
    n_dead = functools.reduce(jnp.minimum, [
        jnp.sum((c2t_ref[0, 0, hh:hh + 1, 0:1] - c2all_ref[0, 0, hh:hh + 1, :] < -UNDERFLOW_LOG2).astype(F32))
        for hh in range(2)])
    first_tile = jnp.minimum(n_dead.astype(jnp.int32) // kt, n_tiles - 1)
    init = tuple((jnp.full((8, qw), NEG_BIG, F32), jnp.zeros((HD_ATT + ONES_ROWS, qw), F32))
                 for _ in chains)
    carry = lax.fori_loop(first_tile, n_tiles - 1, lambda s, c: fast_tile(s, c, False), init)

    def diagonal_tile(carry):
        half = blk // 2
        k0 = pl.multiple_of(qi * blk, blk)
        kj = kx_ref[0, pl.ds(k0, blk), :]
        parts = [(hh, q_off, n_keys) for hh in range(2) for q_off, n_keys in ((0, half), (half, blk))]
        sts = [_dot_nt(kj[0:n_keys], q_heads[hh][q_off:q_off + half, :]) for hh, q_off, n_keys in parts]
        new = {}
        for (hh, q_off, n_keys), st in zip(parts, sts):
            g, acc = carry[hh]
            st = causal(st, k0, n_keys, q_off, half)
            g_part = jnp.maximum(g[:, q_off:q_off + half], jnp.max(st.reshape(n_keys // 8, 8, half), axis=0))
            p = jnp.exp2(st - bounds[hh][:, q_off:q_off + half]).astype(BF16)
            new[hh, q_off] = (g_part, acc[:, q_off:q_off + half] + _dot(v_ones(hh, k0, n_keys), p))
        return tuple(tuple(jnp.concatenate([new[hh, 0][i], new[hh, half][i]], axis=1) for i in range(2))
                     for hh in range(2))

    done = diagonal_tile(carry) if (kt == blk and n_split == 1 and blk % 16 == 0) else fast_tile(n_tiles - 1, carry, True)
    by_head = [[done[chains.index((hh, c))] for c in range(n_split)] for hh in range(2)]
    o_t = jnp.concatenate(
        [jnp.concatenate([acc[0:HD_ATT] / acc[HD_ATT:HD_ATT + 1] for _, acc in by_head[hh]], axis=1)
         for hh in range(2)], axis=0)
    o_ref[0] = o_t.T.astype(o_ref.dtype)
    slack = functools.reduce(jnp.maximum, [jnp.max(b_parts[i] - jnp.max(done[i][0], axis=0, keepdims=True))
                                           for i in range(len(chains))])

    @pl.when(slack > SLACK_LIMIT)
    def _():
        def exact_tile(j, carry, masked):
            k0 = pl.multiple_of(j * blk, blk)
            kj = kx_ref[0, pl.ds(k0, blk), :]
            out = []
            for hh in range(2):
                m, acc = carry[hh]
                st = _dot_nt(kj, q_heads[hh])
                if masked:
                    st = causal(st, k0, blk)
                m_new = jnp.maximum(m, jnp.max(st, axis=0, keepdims=True))
                p = jnp.exp2(st - m_new).astype(BF16)
                out.append((m_new, jnp.exp2(m - m_new) * acc + _dot(v_ones(hh, k0, blk), p)))
            return tuple(out)

        init_x = tuple((jnp.full((1, blk), NEG_BIG, F32), jnp.zeros((HD_ATT + ONES_ROWS, blk), F32))
                       for _ in range(2))
        carry_x = lax.fori_loop(0, qi, lambda j, c: exact_tile(j, c, False), init_x)
        done_x = exact_tile(qi, carry_x, True)
        o_x = jnp.concatenate([acc[0:HD_ATT] / acc[HD_ATT:HD_ATT + 1] for _, acc in done_x], axis=0)
        o_ref[0] = o_x.T.astype(o_ref.dtype)


def _fox_prompt(qx, kx, vtb, c2t, kn, blk, kn_blk):
    nb, l, _ = qx.shape
    n_pairs = D_ATT // LANES
    kt = blk if blk >= 1024 or l % (2 * blk) else 2 * blk
    assert blk % kn_blk == 0 and l % kt == 0
    kn_per_blk = blk // kn_blk
    pair_rows = lambda a: a.reshape(nb, n_pairs, 2, a.shape[-1])
    return pl.pallas_call(
        functools.partial(_fox_prompt_kernel, blk=blk, kt=kt, n_split=1),
        out_shape=jax.ShapeDtypeStruct((nb, l, D_ATT), BF16),
        grid=(nb, n_pairs, l // blk),
        in_specs=[pl.BlockSpec((1, blk, 2 * LANES), lambda b, p, i: (b, i, p)),
                  pl.BlockSpec((1, l, 2 * LANES), lambda b, p, i: (b, 0, p)),
                  pl.BlockSpec((1, LANES, l), lambda b, p, i: (b, p, 0)),
                  pl.BlockSpec((1, 1, 2, blk), lambda b, p, i: (b, p, 0, i)),
                  pl.BlockSpec((1, 1, 2, l), lambda b, p, i: (b, p, 0, 0)),
                  pl.BlockSpec((1, 1, 2, LANES), lambda b, p, i: (b, p, 0, (i + 1) * kn_per_blk - 1))],
        out_specs=pl.BlockSpec((1, blk, LANES), lambda b, p, i: (b, i, p)),
        compiler_params=_cparams(3),
        name="fox_prompt",
    )(qx, kx, vtb, pair_rows(c2t), pair_rows(c2t), pair_rows(kn))


def _ssd_gate_norm(y, z, g):
    return _rms(y * _silu(z), g)


def _ssd_chunk(xb, z, dt, dtt, prev, state_old, cw, cb, alx, alc, dsx, g_norm, tril, triu):
    q = xb.shape[0]
    row8 = lax.broadcasted_iota(jnp.int32, (8, CONV_CH), 0)
    acc = cb + xb * cw[CONV_WIDTH - 1:CONV_WIDTH, :]
    for back in range(1, CONV_WIDTH):
        sh = pltpu.roll(xb, back, 0)
        head = jnp.where(row8 < back, pltpu.roll(prev, back, 0), sh[0:8])
        shifted = jnp.concatenate([head, sh[8:]], axis=0)
        acc = acc + shifted * cw[CONV_WIDTH - 1 - back:CONV_WIDTH - back, :]
    act = _silu(acc)
    xs = act[:, :D_SSM]
    bmat = act[:, D_SSM:D_SSM + N_BC_GROUPS * N_STATE]
    cmat = act[:, D_SSM + N_BC_GROUPS * N_STATE:]

    a_x = -jnp.exp(alx) * dt
    hi, mid, lo = _split3(a_x)
    parts = jnp.concatenate([hi, mid, lo], axis=1).astype(BF16)
    r = _dot(tril, parts)
    acs_x = r[:, 0:D_SSM] + r[:, D_SSM:2 * D_SSM] + r[:, 2 * D_SSM:3 * D_SSM]
    a_t = -jnp.exp(alc) * dtt
    acs_t = _cumsum_lanes(a_t, triu)

    xdt = xs * dt
    a_last = acs_x[q - 1:q, :]
    xdt_end = (xdt * jnp.exp(a_last - acs_x)).astype(BF16)
    e_acs = jnp.exp(acs_x)
    xdt_b = xdt.astype(BF16)
    state_b = state_old.astype(BF16)

    row = lax.broadcasted_iota(jnp.int32, (q, q), 0)
    col = lax.broadcasted_iota(jnp.int32, (q, q), 1)
    causal = row >= col
    first = lax.broadcasted_iota(jnp.int32, (q, LANES), 1) < P_SSM

    y_parts, state_new = [], []
    for g in range(N_BC_GROUPS):
        cg = cmat[:, g * N_STATE:(g + 1) * N_STATE].astype(BF16)
        bg = bmat[:, g * N_STATE:(g + 1) * N_STATE].astype(BF16)
        cbm = _dot_nt(cg, bg)
        gl = slice(g * GROUP_W, (g + 1) * GROUP_W)
        y_off = _dot(cg, state_b[:, gl]) * e_acs[:, gl]
        for pr in range(GROUP_W // LANES):
            lo_lane = g * GROUP_W + pr * LANES
            pair = []
            for hh in range(2):
                h = lo_lane // P_SSM + hh
                seg = acs_x[:, h * P_SSM:h * P_SSM + 1] - acs_t[h:h + 1, :]
                decay = jnp.exp(jnp.where(causal, seg, -jnp.inf))
                gm = (cbm * decay).astype(BF16)
                pair.append(_dot(gm, xdt_b[:, lo_lane:lo_lane + LANES]))
            y_parts.append(jnp.where(first, pair[0], pair[1]) + y_off[:, pr * LANES:(pr + 1) * LANES])
        state_new.append(jnp.exp(a_last[:, gl]) * state_old[:, gl] + _dot_tn(bg, xdt_end[:, gl]))
    y = jnp.concatenate(y_parts, axis=1) + dsx * xs
    return (_ssd_gate_norm(y, z, g_norm).astype(BF16), xb[q - 8:q], jnp.concatenate(state_new, axis=1))


def _mix_out_kernel(x_ref, att_ref, ssm_ref, wo_ref, g_ref, wq_ref, x1_ref, qc_ref):
    x1 = x_ref[...] + _dot(att_ref[...], wo_ref[0:D_ATT, :]) + _dot(ssm_ref[...], wo_ref[D_ATT:, :])
    x1_ref[...] = x1
    h = _rms(x1, g_ref[...]).astype(BF16)
    qc_ref[...] = (_dot(h, wq_ref[...]) * X_SCALE).astype(BF16)


def _mix_out(x, att, ssm, w_out, g_cross, w_cq, bt):
    t = x.shape[0]
    tok = lambda w: pl.BlockSpec((bt, w), lambda i: (i, 0))
    const = lambda shape: pl.BlockSpec(shape, lambda i: (0,) * len(shape))
    return pl.pallas_call(
        _mix_out_kernel,
        out_shape=(jax.ShapeDtypeStruct((t, D_MODEL), F32), jax.ShapeDtypeStruct((t, D_MODEL), BF16)),
        grid=(t // bt,),
        in_specs=[tok(D_MODEL), tok(D_ATT), tok(D_SSM), const((D_MODEL, D_MODEL)),
                  const((1, D_MODEL)), const((D_MODEL, D_MODEL))],
        out_specs=(tok(D_MODEL), tok(D_MODEL)),
        compiler_params=_cparams(1),
        name="mix_out",
    )(x, att, ssm, w_out, g_cross, w_cq)


def _ffn_kernel(x1_ref, o_ref, wco_ref, g_ref, wg_ref, wu_ref, wd_ref, gf_ref, y_ref):
    x2 = x1_ref[...] + _dot(o_ref[...], wco_ref[...])
    h = _rms(x2, g_ref[...]).astype(BF16)
    u = (_silu(_dot(h, wg_ref[...])) * _dot(h, wu_ref[...])).astype(BF16)
    x3 = x2 + _dot(u, wd_ref[...])
    y_ref[...] = _rms(x3, gf_ref[...])


def _ffn(x1, o, w_co, g_ffn, w_gate, w_up, w_down, g_final, bt):
    t = x1.shape[0]
    d_ff = w_gate.shape[1]
    tok = lambda w: pl.BlockSpec((bt, w), lambda i: (i, 0))
    const = lambda shape: pl.BlockSpec(shape, lambda i: (0,) * len(shape),
                                       pipeline_mode=pl.Buffered(1))
    return pl.pallas_call(
        _ffn_kernel,
        out_shape=jax.ShapeDtypeStruct((t, D_MODEL), F32),
        grid=(t // bt,),
        in_specs=[tok(D_MODEL), tok(D_MODEL), const((D_MODEL, D_MODEL)), const((1, D_MODEL)),
                  const((D_MODEL, d_ff)), const((D_MODEL, d_ff)), const((d_ff, D_MODEL)),
                  const((1, D_MODEL))],
        out_specs=tok(D_MODEL),
        compiler_params=_cparams(1),
        name="cross_out_ffn",
    )(x1, o, w_co, g_ffn, w_gate, w_up, w_down, g_final)


def _prompt_tail_kernel(x_ref, att_ref, ssm_ref, mk_ref, mv_ref, wo_ref, gc_ref, wq_ref, wco_ref, gf_ref,
                        wg_ref, wu_ref, wd_ref, gfin_ref, y_ref):
    x1 = x_ref[0] + _dot(att_ref[0], wo_ref[0:D_ATT, :]) + _dot(ssm_ref[0], wo_ref[D_ATT:, :])
    qc = (_dot(_rms(x1, gc_ref[...]).astype(BF16), wq_ref[...]) * X_SCALE).astype(BF16)
    heads = []
    for h in range(H_X):
        hl = slice(h * HD_X, (h + 1) * HD_X)
        p = _softmax_rows(_dot_nt(qc[:, hl], mk_ref[0, :, hl]))
        heads.append(_dot(p.astype(BF16), mv_ref[0, :, hl]).astype(BF16))
    x2 = x1 + _dot(jnp.concatenate(heads, axis=1), wco_ref[...])
    h2 = _rms(x2, gf_ref[...]).astype(BF16)
    u = (_silu(_dot(h2, wg_ref[...])) * _dot(h2, wu_ref[...])).astype(BF16)
    y_ref[0] = _rms(x2 + _dot(u, wd_ref[...]), gfin_ref[...])


def _prompt_tail(x, att, ssm, mk, mv, w_out, g_cross, w_cq, w_co, g_ffn, w_gate, w_up, w_down, g_final, bt):
    nb, l, _ = x.shape
    m = mk.shape[1]
    d_ff = w_gate.shape[1]
    tok = lambda w: pl.BlockSpec((1, bt, w), lambda b, i: (b, i, 0))
    mem = pl.BlockSpec((1, m, D_MODEL), lambda b, i: (b, 0, 0))
    const = lambda shape: pl.BlockSpec(shape, lambda b, i: (0,) * len(shape), pipeline_mode=pl.Buffered(1))
    return pl.pallas_call(
        _prompt_tail_kernel,
        out_shape=jax.ShapeDtypeStruct((nb, l, D_MODEL), F32),
        grid=(nb, l // bt),
        in_specs=[tok(D_MODEL), tok(D_ATT), tok(D_SSM), mem, mem, const((D_MODEL, D_MODEL)),
                  const((1, D_MODEL)), const((D_MODEL, D_MODEL)), const((D_MODEL, D_MODEL)), const((1, D_MODEL)),
                  const((D_MODEL, d_ff)), const((D_MODEL, d_ff)), const((d_ff, D_MODEL)), const((1, D_MODEL))],
        out_specs=tok(D_MODEL),
        compiler_params=_cparams(2),
        name="prompt_tail",
    )(x, att, ssm, mk, mv, w_out, g_cross, w_cq, w_co, g_ffn, w_gate, w_up, w_down, g_final)


def _memory_kv_kernel(mem_ref, g_ref, wk_ref, wv_ref, k32_ref, v32_ref, kb_ref, vb_ref):
    mn = _rms(mem_ref[0], g_ref[...]).astype(BF16)
    k = _dot(mn, wk_ref[...])
    v = _dot(mn, wv_ref[...])
    k32_ref[0] = k
    v32_ref[0] = v
    kb_ref[0] = k.astype(BF16)
    vb_ref[0] = v.astype(BF16)


def _memory_kv(mem, g, w_ck, w_cv):
    nb, m, _ = mem.shape
    blk = pl.BlockSpec((1, m, D_MODEL), lambda b: (b, 0, 0))
    const = lambda shape: pl.BlockSpec(shape, lambda b: (0,) * len(shape))
    f = jax.ShapeDtypeStruct((nb, m, D_MODEL), F32)
    h = jax.ShapeDtypeStruct((nb, m, D_MODEL), BF16)
    return pl.pallas_call(
        _memory_kv_kernel,
        out_shape=(f, f, h, h),
        grid=(nb,),
        in_specs=[blk, const((1, D_MODEL)), const((D_MODEL, D_MODEL)), const((D_MODEL, D_MODEL))],
        out_specs=(blk, blk, blk, blk),
        compiler_params=_cparams(1),
        name="memory_kv",
    )(mem, g, w_ck, w_cv)


def _softmax_rows(s):
    m = jnp.max(s, axis=-1, keepdims=True)
    p = jnp.exp(s - m)
    return p / jnp.sum(p, axis=-1, keepdims=True)


_X_SUB = 2 * H_X


def _cross_sample_kernel(q_ref, k_ref, v_ref, o_ref, *, lq):
    nsb, n_mem = k_ref.shape[0], k_ref.shape[1]
    n = n_mem * _X_SUB
    rows = lq * _X_SUB
    n_tiles = n // LANES
    cls = lax.broadcasted_iota(jnp.int32, (rows, LANES), 1) % _X_SUB
    own = cls == lax.broadcasted_iota(jnp.int32, (rows, LANES), 0) % _X_SUB
    valid = cls < H_X

    class_steps = (8, 16, 32, 64)

    def per_token(x):
        x3 = x.reshape(lq, _X_SUB, LANES)
        return jnp.broadcast_to(jnp.sum(x3, axis=1, keepdims=True), x3.shape).reshape(rows, LANES)

    seqs = range(nsb)
    s5 = [_dot_nt(q_ref[i], k_ref[i].reshape(n, LANES).astype(BF16)) for i in seqs]
    s_t = [[per_token(jnp.where(own, s5[i][:, j * LANES:(j + 1) * LANES], 0.0)) for j in range(n_tiles)]
           for i in seqs]
    s_t = [[u + pltpu.roll(u, LANES - H_X, 1) for u in s_t[i]] for i in seqs]
    m = [functools.reduce(jnp.maximum, s_t[i]) for i in seqs]
    for sh in class_steps:
        m = [jnp.maximum(x, pltpu.roll(x, sh, 1)) for x in m]
    p_t = [[jnp.where(valid, jnp.exp(s - m[i]), 0.0) for s in s_t[i]] for i in seqs]
    l = [functools.reduce(jnp.add, p_t[i]) for i in seqs]
    for sh in class_steps:
        l = [x + pltpu.roll(x, sh, 1) for x in l]
    inv = [1.0 / jnp.where(valid, l[i], 1.0) for i in seqs]
    pn = [[p * inv[i] for p in p_t[i]] for i in seqs]
    p5 = [jnp.concatenate([jnp.where(own, x + pltpu.roll(x, H_X, 1), 0.0).astype(BF16) for x in pn[i]], axis=1)
          for i in seqs]
    for i in seqs:
        v5 = v_ref[i].reshape(n, LANES).astype(BF16)
        o_ref[i] = _dot(p5[i], v5).astype(o_ref.dtype)


def _cross_sample(q5, mem_k, mem_v, lq, n_seq_blk):
    ns, rows, _ = q5.shape
    m = mem_k.shape[1]
    tok = pl.BlockSpec((n_seq_blk, rows, LANES), lambda i: (i, 0, 0))
    mem = pl.BlockSpec((n_seq_blk, m, _X_SUB, LANES), lambda i: (i, 0, 0, 0))
    return pl.pallas_call(
        functools.partial(_cross_sample_kernel, lq=lq),
        out_shape=jax.ShapeDtypeStruct((ns, rows, LANES), BF16),
        grid=(ns // n_seq_blk,),
        in_specs=[tok, mem, mem],
        out_specs=tok,
        compiler_params=_cparams(1),
        name="cross_sample",
    )(q5, mem_k, mem_v)


def _fox_sample_kernel(pt_ref, q_ref, kn_ref, vn_ref, lfn_ref, tril_ref, *rest, n_pages, lq, nsb):
    del pt_ref
    per_seq = 3 * n_pages
    k_refs = [rest[u * per_seq:u * per_seq + n_pages] for u in range(nsb)]
    v_refs = [rest[u * per_seq + n_pages:u * per_seq + 2 * n_pages] for u in range(nsb)]
    lf_refs = [rest[u * per_seq + 2 * n_pages:(u + 1) * per_seq] for u in range(nsb)]
    o_ref, kt_ref, vt_ref = rest[nsb * per_seq:nsb * per_seq + 3]
    seqs = range(nsb)

    for u in seqs:
        for j in range(n_pages):
            kt_ref[u, :, :, j * PAGE_SIZE:(j + 1) * PAGE_SIZE] = k_refs[u][j][0].astype(BF16)
            vt_ref[u, :, :, j * PAGE_SIZE:(j + 1) * PAGE_SIZE] = v_refs[u][j][0].astype(BF16)

    tril = tril_ref[...]
    lf_all = [jnp.concatenate([r[0] for r in lf_refs[u]], axis=0) * LOG2E for u in seqs]
    splits = [_split3(x) for x in lf_all]
    incl = [_dot(hi.astype(BF16), tril) + _dot(mid.astype(BF16), tril) + _dot(lo.astype(BF16), tril)
            for hi, mid, lo in splits]
    bias = []
    for u in seqs:
        after = jnp.zeros((H_ATT, 1), F32)
        pieces = [None] * n_pages
        for j in reversed(range(n_pages)):
            sl = slice(j * H_ATT, (j + 1) * H_ATT)
            pieces[j] = incl[u][sl] - lf_all[u][sl] + after
            after = after + incl[u][sl][:, 0:1]
        bias.append(jnp.concatenate(pieces, axis=1))

    lane = lax.broadcasted_iota(jnp.int32, (H_ATT, lq), 1)
    cn = []
    for u in seqs:
        lfn = lfn_ref[u] * LOG2E
        c = jnp.zeros((H_ATT, lq), F32)
        for t in range(lq):
            c = c + jnp.where(lane >= t, lfn[:, t:t + 1], 0.0)
        cn.append(c)
    causal = (lax.broadcasted_iota(jnp.int32, (lq, lq), 1)
              <= lax.broadcasted_iota(jnp.int32, (lq, lq), 0))

    units = [(u, h) for u in seqs for h in range(H_ATT)]
    scores = [(_dot(q_ref[u, h], kt_ref[u, h]), _dot_nt(q_ref[u, h], kn_ref[u, h])) for u, h in units]
    probs = []
    for (u, h), (s, s_new) in zip(units, scores):
        s = s + bias[u][h:h + 1, :]
        s_new = jnp.where(causal, s_new - cn[u][h:h + 1, :], NEG_BIG)
        m = jnp.maximum(jnp.max(s, axis=-1, keepdims=True), jnp.max(s_new, axis=-1, keepdims=True))
        p = jnp.exp2(s - m)
        p_new = jnp.exp2(s_new - m)
        l = jnp.sum(p, axis=-1, keepdims=True) + jnp.sum(p_new, axis=-1, keepdims=True)
        probs.append((p.astype(BF16), p_new.astype(BF16), l))
    for (u, h), (p, p_new, l) in zip(units, probs):
        o = _dot_nt(p, vt_ref[u, h]) + _dot(p_new, vn_ref[u, h])
        o_ref[u, h] = (o / l).astype(o_ref.dtype)


def _fox_sample(qh, kh, vh, lfn, cache_kt, cache_vt, cache_lft, page_table, nsb):
    ns, _, lq, _ = qh.shape
    n_pages = page_table.shape[1]
    past = n_pages * PAGE_SIZE
    tril = jnp.tril(jnp.ones((PAGE_SIZE, PAGE_SIZE), F32)).astype(BF16)
    tok = pl.BlockSpec((nsb, H_ATT, lq, HD_ATT), lambda i, pt: (i, 0, 0, 0))

    def page(u, j, shape):
        return pl.BlockSpec((1,) + shape,
                            lambda i, pt: (pt[(i * nsb + u) * n_pages + j],) + (0,) * len(shape))

    in_specs = [tok, tok, tok, pl.BlockSpec((nsb, H_ATT, lq), lambda i, pt: (i, 0, 0)),
                pl.BlockSpec((PAGE_SIZE, PAGE_SIZE), lambda i, pt: (0, 0))]
    pages = []
    for u in range(nsb):
        in_specs += [page(u, j, (H_ATT, HD_ATT, PAGE_SIZE)) for j in range(n_pages)]
        in_specs += [page(u, j, (H_ATT, HD_ATT, PAGE_SIZE)) for j in range(n_pages)]
        in_specs += [page(u, j, (H_ATT, PAGE_SIZE)) for j in range(n_pages)]
        pages += [cache_kt] * n_pages + [cache_vt] * n_pages + [cache_lft] * n_pages
    grid_spec = pltpu.PrefetchScalarGridSpec(
        num_scalar_prefetch=1,
        grid=(ns // nsb,),
        in_specs=in_specs,
        out_specs=tok,
        scratch_shapes=[pltpu.VMEM((nsb, H_ATT, HD_ATT, past), BF16),
                        pltpu.VMEM((nsb, H_ATT, HD_ATT, past), BF16)],
    )
    return pl.pallas_call(
        functools.partial(_fox_sample_kernel, n_pages=n_pages, lq=lq, nsb=nsb),
        out_shape=jax.ShapeDtypeStruct((ns, H_ATT, lq, HD_ATT), BF16),
        grid_spec=grid_spec,
        compiler_params=_cparams(1),
        name="fox_sample",
    )(page_table.reshape(-1), qh, kh, vh, lfn, tril, *pages)


def _ssd_sample_kernel(xbc_ref, z_ref, dtx_ref, cs_ref, st_ref, cw_ref, cb_ref, alx_ref, dsx_ref,
                       g_ref, y_ref, sto_ref, xp_ref, rows_ref, brow_ref, *, lq):
    nsb = xbc_ref.shape[0]
    assert nsb * (lq + 1) <= LANES
    kw = CONV_WIDTH - 1
    xp_ref[:, 0:kw, :] = cs_ref[...]
    xp_ref[:, kw:kw + lq, :] = xbc_ref[...]
    acc = cb_ref[...]
    for tap in range(CONV_WIDTH):
        acc = acc + xp_ref[:, tap:tap + lq, :] * cw_ref[tap:tap + 1, :]
    act = _silu(acc)
    xs = act[:, :, :D_SSM]
    bmat = act[:, :, D_SSM:D_SSM + N_BC_GROUPS * N_STATE]
    cmat = act[:, :, D_SSM + N_BC_GROUPS * N_STATE:]
    dt = dtx_ref[...]
    a = -jnp.exp(alx_ref[...]) * dt
    xdt = xs * dt
    acs = [a[:, 0:1, :]]
    for t in range(1, lq):
        acs.append(acs[-1] + a[:, t:t + 1, :])
    a_last = acs[lq - 1]

    ccat = jnp.concatenate([cmat[:, :, g * N_STATE:(g + 1) * N_STATE] for g in range(N_BC_GROUPS)],
                           axis=1).astype(BF16)
    state = st_ref[...]
    r = jnp.einsum("sgn,shn->sgh", ccat, state.astype(BF16), preferred_element_type=F32)
    group0 = lax.broadcasted_iota(jnp.int32, (1, 1, D_SSM), 2) < GROUP_W
    y_off = jnp.where(group0, r[:, 0:lq, :], r[:, lq:2 * lq, :])

    ys = []
    for t in range(lq):
        y_t = jnp.exp(acs[t]) * y_off[:, t:t + 1, :] + dsx_ref[...] * xs[:, t:t + 1, :]
        for s in range(t + 1):
            cb = jnp.concatenate(
                [jnp.broadcast_to(
                    jnp.sum(cmat[:, t:t + 1, g * N_STATE:(g + 1) * N_STATE]
                            * bmat[:, s:s + 1, g * N_STATE:(g + 1) * N_STATE], axis=-1, keepdims=True),
                    (nsb, 1, GROUP_W)) for g in range(N_BC_GROUPS)], axis=2)
            y_t = y_t + cb * jnp.exp(acs[t] - acs[s]) * xdt[:, s:s + 1, :]
        ys.append(y_t)
    y = jnp.concatenate(ys, axis=1)
    y_ref[...] = _ssd_gate_norm(y, z_ref[...], g_ref[...]).astype(y_ref.dtype)

    upd = jnp.concatenate([xdt[:, s:s + 1, :] * jnp.exp(a_last - acs[s]) for s in range(lq)], axis=1)
    e_last = jnp.exp(a_last)
    rows_ref[...] = jnp.zeros_like(rows_ref)
    brow_ref[...] = jnp.zeros_like(brow_ref)
    for i in range(nsb):
        rows_ref[i * lq:(i + 1) * lq, :] = upd[i]
        rows_ref[nsb * lq + i:nsb * lq + i + 1, :] = e_last[i]
        brow_ref[i * lq:(i + 1) * lq, :] = bmat[i]
    cols = rows_ref[...].T
    upd_cols = cols.astype(BF16)
    b_all = brow_ref[...]
    row_seq = lax.broadcasted_iota(jnp.int32, (LANES, N_BC_GROUPS * N_STATE), 0) // lq
    top = lax.broadcasted_iota(jnp.int32, (D_SSM, N_STATE), 0) < GROUP_W
    for i in range(nsb):
        b_rows = jnp.where(row_seq == i, b_all, 0.0).astype(BF16)
        m = _dot(upd_cols, b_rows)
        add = jnp.where(top, m[:, 0:N_STATE], m[:, N_STATE:2 * N_STATE])
        decay = jnp.broadcast_to(cols[:, nsb * lq + i:nsb * lq + i + 1], (D_SSM, N_STATE))
        sto_ref[i] = decay * state[i] + add


def _ssd_sample(xbc, z, dtx, conv_state, state, conv_w, conv_b, a_log_x, d_skip_x, norm_g, nsb):
    ns, lq, _ = xbc.shape
    seq = lambda r, w: pl.BlockSpec((nsb, r, w), lambda i: (i, 0, 0))
    const = lambda shape: pl.BlockSpec(shape, lambda i: (0,) * len(shape))
    return pl.pallas_call(
        functools.partial(_ssd_sample_kernel, lq=lq),
        out_shape=(jax.ShapeDtypeStruct((ns, lq, D_SSM), BF16),
                   jax.ShapeDtypeStruct((ns, D_SSM, N_STATE), F32)),
        grid=(ns // nsb,),
        in_specs=[seq(lq, CONV_CH), seq(lq, D_SSM), seq(lq, D_SSM), seq(CONV_WIDTH - 1, CONV_CH),
                  seq(D_SSM, N_STATE), const((CONV_WIDTH, CONV_CH)), const((1, CONV_CH)),
                  const((1, D_SSM)), const((1, D_SSM)), const((1, D_SSM))],
        out_specs=(seq(lq, D_SSM), seq(D_SSM, N_STATE)),
        scratch_shapes=[pltpu.VMEM((nsb, 8, CONV_CH), F32), pltpu.VMEM((LANES, D_SSM), F32),
                        pltpu.VMEM((LANES, N_BC_GROUPS * N_STATE), F32)],
        compiler_params=_cparams(1),
        name="ssd_sample",
    )(xbc, z, dtx, conv_state, state, conv_w, conv_b, a_log_x, d_skip_x, norm_g)


def _pick(pref, n):
    return pref if n % pref == 0 else n


def kernel(x_prompt, x_sample, mem_prompt, cache_k, cache_v, cache_logf, page_table, cache_mem_k, cache_mem_v, state_conv, state_ssm, norm_mix_g, w_in, b_forget, conv_w, conv_b, dt_bias, a_log, d_skip, ssm_norm_g, w_out, norm_cross_g, norm_mem_g, w_cq, w_ck, w_cv, w_co, norm_ffn_g, w_gate, w_up, w_down, final_norm_g):
    assert w_in.shape[0] == 1, "one layer"
    nb, l, _ = x_prompt.shape
    ns, lq, _ = x_sample.shape
    row = lambda v: v.reshape(1, -1).astype(F32)
    colv = lambda v: v.reshape(-1, 1).astype(F32)
    per_ch = lambda v: jnp.repeat(v.astype(F32), P_SSM).reshape(1, D_SSM)

    w_t = w_in[0].T
    cuts = [D_ATT, 2 * D_ATT, 3 * D_ATT, 3 * D_ATT + H_ATT, 3 * D_ATT + H_ATT + D_SSM,
            3 * D_ATT + H_ATT + D_SSM + CONV_CH]
    w_q, w_k, w_v, w_f, w_z, w_xbc, w_dt = jnp.split(w_t, cuts, axis=0)
    lane_pad = lambda a: jnp.pad(a, ((0, 0), (0, LANES - a.shape[1])))
    w_main_t = jnp.concatenate([w_q, w_k, w_z, w_xbc, jnp.repeat(w_dt, P_SSM, axis=0),
                                jnp.pad(w_f, ((0, LANES - H_ATT), (0, 0)))], axis=0).astype(BF16)
    w_small_t = jnp.concatenate([w_f, w_dt], axis=0).astype(BF16)
    in_params = (row(norm_mix_g[0]), w_main_t, w_v.astype(BF16), w_small_t, colv(b_forget[0]),
                 colv(dt_bias[0]), per_ch(dt_bias[0]), lane_pad(row(b_forget[0])))
    ssd_params = (conv_w[0], row(conv_b[0]), per_ch(a_log[0]))
    ssd_tail = (per_ch(d_skip[0]), row(ssm_norm_g[0]))
    w_out_b, w_cq_b, w_co_b = w_out[0].astype(BF16), w_cq[0].astype(BF16), w_co[0].astype(BF16)
    w_gate_b, w_up_b, w_down_b = w_gate[0].astype(BF16), w_up[0].astype(BF16), w_down[0].astype(BF16)
    g_cross, g_ffn, g_final = row(norm_cross_g[0]), row(norm_ffn_g[0]), row(final_norm_g)

    def tail(x, att, ssm, cross, bt):
        x1, qc = _mix_out(x, att, ssm, w_out_b, g_cross, w_cq_b, _pick(2 * bt, x.shape[0]))
        o = cross(qc)
        return _ffn(x1, o, w_co_b, g_ffn, w_gate_b, w_up_b, w_down_b, g_final, bt)

    bt = _pick(512, l)
    assert bt % SSD_CHUNK == 0
    pp = _in_projection(x_prompt, *in_params, bt, ssd=(*ssd_params, colv(a_log[0]), *ssd_tail))
    k32, v32t, logft, ssm, st_p = pp["k32"], pp["v32t"], pp["logft"], pp["yssm"], pp["state"]
    att = _fox_prompt(pp["qx"], pp["kx"], pp["vtb"], pp["c2t"], pp["kn"], _pick(1024, l), bt)
    mk32, mv32, mkb, mvb = _memory_kv(mem_prompt, row(norm_mem_g[0]), w_ck[0].astype(BF16),
                                      w_cv[0].astype(BF16))
    y_prompt = _prompt_tail(x_prompt, att, ssm, mkb, mvb, w_out_b, g_cross, w_cq_b, w_co_b, g_ffn,
                            w_gate_b, w_up_b, w_down_b, g_final, bt)

    ts = ns * lq
    bts = _pick(512, ts)
    ps = _in_projection(x_sample.reshape(1, ts, D_MODEL), *in_params, bts)
    qx_s, k32_s, vtb_s, v32t_s = ps["qx"], ps["k32"], ps["vtb"], ps["v32t"]
    z_s, xbc_s, dtx_s, logft_s = ps["z"], ps["xbc"], ps["dtx"], ps["logft"]
    seq3 = lambda a: a.reshape(ns, lq, a.shape[-1])
    lfn = logft_s.reshape(H_ATT, ns, lq).transpose(1, 0, 2)
    head_major = lambda a: a.reshape(ns, lq, H_ATT, HD_ATT).transpose(0, 2, 1, 3)
    from_t = lambda a: a.reshape(H_ATT, HD_ATT, ns, lq)
    qb_s = qx_s.reshape(ts, D_ATT // LANES, 2, LANES)[:, :, 0, :]
    att_s = _fox_sample(head_major(qb_s), head_major(k32_s.astype(BF16)),
                        from_t(vtb_s).transpose(2, 0, 3, 1), lfn,
                        cache_k[0].transpose(0, 2, 3, 1), cache_v[0].transpose(0, 2, 3, 1),
                        cache_logf[0].transpose(0, 2, 1), page_table, _pick(2, ns))
    att_s = att_s.transpose(0, 2, 1, 3)
    ssm_s, st_s = _ssd_sample(seq3(xbc_s), seq3(z_s), seq3(dtx_s), state_conv[0],
                              state_ssm[0].reshape(ns, D_SSM, N_STATE), *ssd_params, *ssd_tail,
                              _pick(8, ns))
    n_mem = mem_prompt.shape[1]
    stored = lambda a: a.reshape(ns, -1, H_X, 2, LANES).transpose(0, 1, 3, 2, 4).reshape(ns, -1, _X_SUB, LANES)
    cross_s = lambda qc: _cross_sample(
        stored(qc).reshape(ns, lq * _X_SUB, LANES), stored(cache_mem_k[0]), stored(cache_mem_v[0]), lq,
        _pick(4, ns)).reshape(ns, lq, 2, H_X, LANES).transpose(0, 1, 3, 2, 4).reshape(ts, D_MODEL)
    y_sample = tail(x_sample.reshape(ts, D_MODEL), att_s.reshape(ts, D_ATT), ssm_s.reshape(ts, D_SSM),
                    cross_s, bts).reshape(ns, lq, D_MODEL)

    heads = lambda a, n: a.reshape(1, n, -1, H_ATT, HD_ATT)
    kw = CONV_WIDTH - 1
    return (y_prompt, y_sample,
            heads(k32, nb), v32t.reshape(nb, H_ATT, HD_ATT, l).transpose(0, 3, 1, 2)[None],
            logft.transpose(0, 2, 1)[None],
            mk32.reshape(1, nb, n_mem, H_X, HD_X), mv32.reshape(1, nb, n_mem, H_X, HD_X),
            pp["tail"][:, 8 - kw:, :][None], st_p.reshape(1, nb, H_SSM, P_SSM, N_STATE),
            heads(k32_s, ns), from_t(v32t_s).transpose(2, 3, 0, 1)[None],
            logft_s.reshape(H_ATT, ns, lq).transpose(1, 2, 0)[None],
            seq3(xbc_s)[:, lq - kw:, :][None], st_s.reshape(1, ns, H_SSM, P_SSM, N_STATE))
```

```python
import functools
import math

import numpy as np
import jax
import jax.numpy as jnp
from jax import lax
from jax.experimental import pallas as pl
from jax.experimental.pallas import tpu as pltpu

F32 = jnp.float32
BF16 = jnp.bfloat16

D_MODEL = 1024
D_ATT = 512
H_ATT = 8
HD_ATT = 64
D_SSM = 512
H_SSM = 8
P_SSM = 64
N_STATE = 128
N_BC_GROUPS = 2
GROUP_W = D_SSM // N_BC_GROUPS
CONV_WIDTH = 4
CONV_CH = D_SSM + 2 * N_BC_GROUPS * N_STATE
H_X = 4
HD_X = 256
EPS = 1e-6
ATT_SCALE = HD_ATT ** -0.5
X_SCALE = HD_X ** -0.5
SSD_CHUNK = 128
PAGE_SIZE = 128
NEG_BIG = -1e30
LOG2E = math.log2(math.e)

BOUND_SCALE = 1.03
BOUND_MARGIN = 2.0
SLACK_LIMIT = 80.0
UNDERFLOW_LOG2 = 154.0

LANES = 128
ONES_ROWS = 16
VMEM_LIMIT = 56 * 1024 * 1024

_Q0, _K0, _Z0, _XBC0, _DT0, _F0, _MAIN_COLS = 0, 512, 1024, 1536, 2560, 3072, 3200


def _cparams(n_axes):
    return pltpu.CompilerParams(dimension_semantics=("arbitrary",) * n_axes,
                                vmem_limit_bytes=VMEM_LIMIT)


def _rms(x, g):
    ms = jnp.mean(x * x, axis=-1, keepdims=True)
    return x * lax.rsqrt(ms + EPS) * g


def _softplus(x):
    return jnp.maximum(x, 0.0) + jnp.log1p(jnp.exp(-jnp.abs(x)))


def _log_sigmoid(x):
    return jnp.minimum(x, 0.0) - jnp.log1p(jnp.exp(-jnp.abs(x)))


def _silu(x):
    return x * (1.0 / (1.0 + jnp.exp(-x)))


def _split3(x):
    hi = x.astype(BF16).astype(F32)
    r1 = x - hi
    mid = r1.astype(BF16).astype(F32)
    lo = (r1 - mid).astype(BF16).astype(F32)
    return hi, mid, lo


def _dot(a, b):
    return jnp.dot(a, b, preferred_element_type=F32)


def _dot_nt(a, b):
    return lax.dot_general(a, b, (((1,), (1,)), ((), ())), preferred_element_type=F32)


def _dot_tn(a, b):
    return lax.dot_general(a, b, (((0,), (0,)), ((), ())), preferred_element_type=F32)


def _cumsum_lanes(x, tri_upper):
    hi, mid, lo = _split3(x)
    parts = jnp.concatenate([hi, mid, lo], axis=0).astype(BF16)
    r = _dot(parts, tri_upper)
    return r[0:8] + r[8:16] + r[16:24]


def _inproj_kernel(x_ref, g_ref, wm_ref, wvt_ref, wst_ref, bf_ref, dtb_ref, dtbx_ref, bfx_ref,
                   tril_ref, sel_ref, ones_ref, hsel_ref, *rest, with_ssd):
    if with_ssd:
        (cw_ref, cb_ref, alx_ref, alc_ref, dsx_ref, gssm_ref, trilq_ref, triuq_ref,
         qx_ref, kx_ref, k32_ref, vtb_ref, v32t_ref, logft_ref, c2t_ref, kn_ref, yssm_ref, st_ref, tail_ref,
         carry_ref, carry_t_ref, carry_kn_ref, xp_ref, state_ref) = rest
    else:
        (qx_ref, kx_ref, k32_ref, vtb_ref, v32t_ref, z_ref, xbc_ref, dtx_ref, logft_ref, dtt_ref,
         c2t_ref, kn_ref, carry_ref, carry_t_ref, carry_kn_ref) = rest

    @pl.when(pl.program_id(1) == 0)
    def _():
        carry_ref[...] = jnp.zeros_like(carry_ref)
        carry_t_ref[...] = jnp.zeros_like(carry_t_ref)
        carry_kn_ref[...] = jnp.zeros_like(carry_kn_ref)
        if with_ssd:
            xp_ref[...] = jnp.zeros_like(xp_ref)
            state_ref[...] = jnp.zeros_like(state_ref)

    h = _rms(x_ref[0], g_ref[...]).astype(BF16)
    bt = h.shape[0]
    proj = lambda lo, hi: _dot_nt(h, wm_ref[lo:hi, :])
    xbc = proj(_XBC0, _DT0)
    z = proj(_Z0, _XBC0)
    dtx = _softplus(proj(_DT0, _F0) + dtbx_ref[...])
    small = _dot_nt(wst_ref[...], h)
    dtt = _softplus(small[8:16] + dtb_ref[...])
    if with_ssd:
        ssd_consts = (cw_ref[...], cb_ref[...], alx_ref[...], alc_ref[...], dsx_ref[...], gssm_ref[...],
                      trilq_ref[...], triuq_ref[...])
        ssd_state = (xp_ref[...], state_ref[...])
        n_chunks = bt // SSD_CHUNK

        def ssd(c, carry):
            rows = slice(c * SSD_CHUNK, (c + 1) * SSD_CHUNK)
            y, tail, state = _ssd_chunk(xbc[rows], z[rows], dtx[rows], dtt[:, rows], *carry, *ssd_consts)
            yssm_ref[0, rows, :] = y
            return tail, state
    else:
        z_ref[0], xbc_ref[0], dtx_ref[0], dtt_ref[0] = z, xbc, dtx, dtt
        n_chunks = 0
        ssd = None
    ssd_slots = [[c for c in range(n_chunks) if c * 4 // n_chunks == i] for i in range(4)]

    def ssd_here(slot, carry):
        for c in ssd_slots[slot]:
            carry = ssd(c, carry)
        return carry

    q = (proj(_Q0, _K0) * (ATT_SCALE * LOG2E)).astype(BF16)
    k = proj(_K0, _Z0)
    k32_ref[0] = k
    kb = k.astype(BF16)
    if with_ssd:
        ssd_state = ssd_here(0, ssd_state)
    f_rows = proj(_F0, _MAIN_COLS)

    lf2 = _log_sigmoid(f_rows + bfx_ref[...]) * LOG2E
    parts = jnp.concatenate(_split3(lf2), axis=1).astype(BF16)
    r = _dot(tril_ref[...], parts)
    if with_ssd:
        ssd_state = ssd_here(1, ssd_state)

    logft = _log_sigmoid(small[0:8] + bf_ref[...])
    logft_ref[0] = logft
    hi, mid, lo = _split3(logft * LOG2E)
    r_t = _dot_nt(jnp.concatenate([hi, mid, lo], axis=0).astype(BF16), tril_ref[...])
    if with_ssd:
        ssd_state = ssd_here(2, ssd_state)
    vt = _dot_nt(wvt_ref[...], h)
    v32t_ref[0] = vt
    vtb_ref[0] = vt.astype(BF16)

    c2 = r[:, 0:LANES] + r[:, LANES:2 * LANES] + r[:, 2 * LANES:3 * LANES] + carry_ref[0:1, :]
    carry_ref[...] = jnp.broadcast_to(c2[bt - 1:bt, :], carry_ref.shape)
    csplit = jnp.concatenate(_split3(c2), axis=1).astype(BF16)
    ext = (_dot(csplit, sel_ref[...]) + ones_ref[...]).astype(BF16)
    if with_ssd:
        tail, state = ssd_here(3, ssd_state)
        xp_ref[...] = tail
        state_ref[...] = state
        tail_ref[0] = tail

        @pl.when(pl.program_id(1) == pl.num_programs(1) - 1)
        def _():
            st_ref[0] = state.T

    c2t = r_t[0:8] + r_t[8:16] + r_t[16:24] + carry_t_ref[:, 0:1]
    c2t_ref[0] = c2t
    carry_t_ref[...] = jnp.broadcast_to(c2t[:, c2t.shape[1] - 1:], carry_t_ref.shape)
    kf = kb.astype(F32)
    ksq_t = _dot_nt(hsel_ref[...], (kf * kf).astype(BF16))
    kn = jnp.maximum(carry_kn_ref[...], jnp.max(ksq_t, axis=1, keepdims=True))
    carry_kn_ref[...] = kn
    kn_ref[0] = kn
    for p in range(D_ATT // LANES):
        pl_ = slice(p * LANES, (p + 1) * LANES)
        kx_ref[0, :, 2 * p * LANES:(2 * p + 1) * LANES] = kb[:, pl_]
        kx_ref[0, :, (2 * p + 1) * LANES:(2 * p + 2) * LANES] = ext[:, pl_]
        qx_ref[0, :, 2 * p * LANES:(2 * p + 1) * LANES] = q[:, pl_]
        qx_ref[0, :, (2 * p + 1) * LANES:(2 * p + 2) * LANES] = ext[:, D_ATT + p * LANES:D_ATT + (p + 1) * LANES]


def _attention_extras():
    sel = np.zeros((3 * LANES, 2 * D_ATT), np.float32)
    ones = np.zeros((1, 2 * D_ATT), np.float32)
    for p in range(D_ATT // LANES):
        ones[0, p * LANES + 6:p * LANES + 12] = 1.0
        for hh in range(2):
            for s in range(3):
                sel[s * LANES + 2 * p + hh, p * LANES + 3 * hh + s] = 1.0
                sel[s * LANES + 2 * p + hh, D_ATT + p * LANES + 6 + 3 * hh + s] = 1.0
    return jnp.asarray(sel, BF16), jnp.asarray(ones, F32)


def _in_projection(x, g, w_main, w_vt, w_small_t, b_forget, dt_bias, dt_bias_x, b_forget_x, bt, ssd=None):
    nb, l, _ = x.shape
    tril = jnp.tril(jnp.ones((bt, bt), F32)).astype(BF16)
    sel, ones = _attention_extras()
    tok = lambda w: pl.BlockSpec((1, bt, w), lambda b, j: (b, j, 0))
    tok_t = lambda r: pl.BlockSpec((1, r, bt), lambda b, j: (b, 0, j))
    const = lambda shape: pl.BlockSpec(shape, lambda b, j: (0,) * len(shape))
    per_batch = lambda r, w: pl.BlockSpec((1, r, w), lambda b, j: (b, 0, 0))
    f32 = lambda *s: jax.ShapeDtypeStruct(s, F32)
    bf16 = lambda *s: jax.ShapeDtypeStruct(s, BF16)
    outs = {
        "qx": (bf16(nb, l, 2 * D_ATT), tok(2 * D_ATT)), "kx": (bf16(nb, l, 2 * D_ATT), tok(2 * D_ATT)),
        "k32": (f32(nb, l, D_ATT), tok(D_ATT)),
        "vtb": (bf16(nb, D_ATT, l), tok_t(D_ATT)), "v32t": (f32(nb, D_ATT, l), tok_t(D_ATT)),
        "z": (f32(nb, l, D_SSM), tok(D_SSM)), "xbc": (f32(nb, l, CONV_CH), tok(CONV_CH)),
        "dtx": (f32(nb, l, D_SSM), tok(D_SSM)),
        "logft": (f32(nb, H_ATT, l), tok_t(H_ATT)), "dtt": (f32(nb, H_SSM, l), tok_t(H_SSM)),
        "c2t": (f32(nb, H_ATT, l), tok_t(H_ATT)),
        "kn": (f32(nb, H_ATT, (l // bt) * LANES), pl.BlockSpec((1, H_ATT, LANES), lambda b, j: (b, 0, j))),
        "yssm": (bf16(nb, l, D_SSM), tok(D_SSM)), "state": (f32(nb, D_SSM, N_STATE), per_batch(D_SSM, N_STATE)),
        "tail": (f32(nb, 8, CONV_CH), per_batch(8, CONV_CH)),
    }
    if ssd is None:
        names = ("qx", "kx", "k32", "vtb", "v32t", "z", "xbc", "dtx", "logft", "dtt", "c2t", "kn")
        extra_in, extra_specs, extra_scratch = (), [], []
    else:
        names = ("qx", "kx", "k32", "vtb", "v32t", "logft", "c2t", "kn", "yssm", "state", "tail")
        q = SSD_CHUNK
        extra_in = tuple(ssd) + (jnp.tril(jnp.ones((q, q), F32)).astype(BF16),
                                 jnp.triu(jnp.ones((q, q), F32)).astype(BF16))
        extra_specs = [const((CONV_WIDTH, CONV_CH)), const((1, CONV_CH)), const((1, D_SSM)), const((H_SSM, 1)),
                       const((1, D_SSM)), const((1, D_SSM)), const((q, q)), const((q, q))]
        extra_scratch = [pltpu.VMEM((8, CONV_CH), F32), pltpu.VMEM((N_STATE, D_SSM), F32)]
    head_sel = jnp.asarray(np.repeat(np.eye(H_ATT, dtype=np.float32), HD_ATT, axis=1), BF16)
    res = pl.pallas_call(
        functools.partial(_inproj_kernel, with_ssd=ssd is not None),
        out_shape=tuple(outs[n][0] for n in names),
        grid=(nb, l // bt),
        in_specs=[tok(D_MODEL), const((1, D_MODEL)), const((_MAIN_COLS, D_MODEL)),
                  const((D_ATT, D_MODEL)), const((2 * H_ATT, D_MODEL)), const((H_ATT, 1)),
                  const((H_SSM, 1)), const((1, D_SSM)), const((1, LANES)), const((bt, bt)),
                  const((3 * LANES, 2 * D_ATT)), const((1, 2 * D_ATT)), const((H_ATT, D_ATT))] + extra_specs,
        out_specs=tuple(outs[n][1] for n in names),
        scratch_shapes=[pltpu.VMEM((8, LANES), F32), pltpu.VMEM((H_ATT, LANES), F32),
                        pltpu.VMEM((H_ATT, LANES), F32)] + extra_scratch,
        compiler_params=_cparams(2),
        name="in_projection_ssd" if ssd is not None else "in_projection",
    )(x, g, w_main, w_vt, w_small_t, b_forget, dt_bias, dt_bias_x, b_forget_x, tril, sel, ones, head_sel,
      *extra_in)
    return dict(zip(names, res))


def _fox_prompt_kernel(qx_ref, kx_ref, vt_ref, c2t_ref, c2all_ref, kn_ref, o_ref, *, blk, kt, n_split):
    qi = pl.program_id(2)
    q = qx_ref[0, :, 0:LANES]
    cq = jnp.broadcast_to(qx_ref[0, 0:1, LANES:2 * LANES], (blk, LANES))
    lane = lax.broadcasted_iota(jnp.int32, (blk, LANES), 1)
    first = lane < HD_ATT
    zero = jnp.zeros_like(q)
    minus = jnp.full_like(q, -1.0)
    q_heads, bounds = [], []
    for hh in range(2):
        ext = jnp.where((lane >= 3 * hh) & (lane < 3 * hh + 3), minus,
                        jnp.where((lane >= 6 + 3 * hh) & (lane < 9 + 3 * hh), cq, zero))
        own = jnp.where(first, q, zero) if hh == 0 else jnp.where(first, zero, q)
        q_heads.append(jnp.concatenate([own, ext], axis=1))
        qsq = _dot_nt(jnp.ones((8, LANES), BF16), own * own)[0:1]
        c2q = c2t_ref[0, 0, hh:hh + 1, :]
        bounds.append(BOUND_SCALE * jnp.sqrt(qsq * kn_ref[0, 0, hh:hh + 1, 0:1])
                      + (c2q[:, 0:1] - c2q) + BOUND_MARGIN)

    def v_ones(hh, k0, n):
        vt = vt_ref[0, hh * HD_ATT:(hh + 1) * HD_ATT, pl.ds(k0, n)]
        return jnp.concatenate([vt, jnp.ones((ONES_ROWS, n), BF16)], axis=0)

    def causal(st, k0, n, q_off=0, q_n=blk):
        key = lax.broadcasted_iota(jnp.int32, (n, q_n), 0) + (k0 - qi * blk - q_off)
        qry = lax.broadcasted_iota(jnp.int32, (n, q_n), 1)
        return jnp.where(key <= qry, st, NEG_BIG)

    qw = blk // n_split
    chains = [(hh, c) for c in range(n_split) for hh in range(2)]
    q_parts = [q_heads[hh][c * qw:(c + 1) * qw, :] for hh, c in chains]
    b_parts = [bounds[hh][:, c * qw:(c + 1) * qw] for hh, c in chains]

    def fast_tile(s, carry, masked):
        k0 = pl.multiple_of(s * kt, kt)
        kj = kx_ref[0, pl.ds(k0, kt), :]
        st_next = _dot_nt(kj, q_parts[0])
        out = []
        for i, ((hh, c), (g, acc)) in enumerate(zip(chains, carry)):
            st = st_next
            if i + 1 < len(chains):
                st_next = _dot_nt(kj, q_parts[i + 1])
            if masked:
                st = causal(st, k0, kt, c * qw, qw)
            g = jnp.maximum(g, jnp.max(st.reshape(kt // 8, 8, qw), axis=0))
            p = jnp.exp2(st - b_parts[i]).astype(BF16)
            out.append((g, acc + _dot(v_ones(hh, k0, kt), p)))
        return tuple(out)

    n_tiles = (qi * blk + blk + kt - 1) // kt
    n_dead = functools.reduce(jnp.minimum, [
        jnp.sum((c2t_ref[0, 0, hh:hh + 1, 0:1] - c2all_ref[0, 0, hh:hh + 1, :] < -UNDERFLOW_LOG2).astype(F32))
        for hh in range(2)])
    first_tile = jnp.minimum(n_dead.astype(jnp.int32) // kt, n_tiles - 1)
    init = tuple((jnp.full((8, qw), NEG_BIG, F32), jnp.zeros((HD_ATT + ONES_ROWS, qw), F32))
                 for _ in chains)
    carry = lax.fori_loop(first_tile, n_tiles - 1, lambda s, c: fast_tile(s, c, False), init)

    def diagonal_tile(carry):
        half = blk // 2
        k0 = pl.multiple_of(qi * blk, blk)
        kj = kx_ref[0, pl.ds(k0, blk), :]
        parts = [(hh, q_off, n_keys) for hh in range(2) for q_off, n_keys in ((0, half), (half, blk))]
        sts = [_dot_nt(kj[0:n_keys], q_heads[hh][q_off:q_off + half, :]) for hh, q_off, n_keys in parts]
        new = {}
        for (hh, q_off, n_keys), st in zip(parts, sts):
            g, acc = carry[hh]
            st = causal(st, k0, n_keys, q_off, half)
            g_part = jnp.maximum(g[:, q_off:q_off + half], jnp.max(st.reshape(n_keys // 8, 8, half), axis=0))
            p = jnp.exp2(st - bounds[hh][:, q_off:q_off + half]).astype(BF16)
            new[hh, q_off] = (g_part, acc[:, q_off:q_off + half] + _dot(v_ones(hh, k0, n_keys), p))
        return tuple(tuple(jnp.concatenate([new[hh, 0][i], new[hh, half][i]], axis=1) for i in range(2))
                     for hh in range(2))

    done = diagonal_tile(carry) if (kt == blk and n_split == 1 and blk % 16 == 0) else fast_tile(n_tiles - 1, carry, True)
    by_head = [[done[chains.index((hh, c))] for c in range(n_split)] for hh in range(2)]
    o_t = jnp.concatenate(
        [jnp.concatenate([acc[0:HD_ATT] / acc[HD_ATT:HD_ATT + 1] for _, acc in by_head[hh]], axis=1)
         for hh in range(2)], axis=0)
    o_ref[0] = o_t.T.astype(o_ref.dtype)
    slack = functools.reduce(jnp.maximum, [jnp.max(b_parts[i] - jnp.max(done[i][0], axis=0, keepdims=True))
                                           for i in range(len(chains))])

    @pl.when(slack > SLACK_LIMIT)
    def _():
        def exact_tile(j, carry, masked):
            k0 = pl.multiple_of(j * blk, blk)
            kj = kx_ref[0, pl.ds(k0, blk), :]
            out = []
            for hh in range(2):
                m, acc = carry[hh]
                st = _dot_nt(kj, q_heads[hh])
                if masked:
                    st = causal(st, k0, blk)
                m_new = jnp.maximum(m, jnp.max(st, axis=0, keepdims=True))
                p = jnp.exp2(st - m_new).astype(BF16)
                out.append((m_new, jnp.exp2(m - m_new) * acc + _dot(v_ones(hh, k0, blk), p)))
            return tuple(out)

        init_x = tuple((jnp.full((1, blk), NEG_BIG, F32), jnp.zeros((HD_ATT + ONES_ROWS, blk), F32))
                       for _ in range(2))
        carry_x = lax.fori_loop(0, qi, lambda j, c: exact_tile(j, c, False), init_x)
        done_x = exact_tile(qi, carry_x, True)
        o_x = jnp.concatenate([acc[0:HD_ATT] / acc[HD_ATT:HD_ATT + 1] for _, acc in done_x], axis=0)
        o_ref[0] = o_x.T.astype(o_ref.dtype)


def _fox_prompt(qx, kx, vtb, c2t, kn, blk, kn_blk):
    nb, l, _ = qx.shape
    n_pairs = D_ATT // LANES
    kt = blk if blk >= 1024 or l % (2 * blk) else 2 * blk
    assert blk % kn_blk == 0 and l % kt == 0
    kn_per_blk = blk // kn_blk
    pair_rows = lambda a: a.reshape(nb, n_pairs, 2, a.shape[-1])
    return pl.pallas_call(
        functools.partial(_fox_prompt_kernel, blk=blk, kt=kt, n_split=1),
        out_shape=jax.ShapeDtypeStruct((nb, l, D_ATT), BF16),
        grid=(nb, n_pairs, l // blk),
        in_specs=[pl.BlockSpec((1, blk, 2 * LANES), lambda b, p, i: (b, i, p)),
                  pl.BlockSpec((1, l, 2 * LANES), lambda b, p, i: (b, 0, p)),
                  pl.BlockSpec((1, LANES, l), lambda b, p, i: (b, p, 0)),
                  pl.BlockSpec((1, 1, 2, blk), lambda b, p, i: (b, p, 0, i)),
                  pl.BlockSpec((1, 1, 2, l), lambda b, p, i: (b, p, 0, 0)),
                  pl.BlockSpec((1, 1, 2, LANES), lambda b, p, i: (b, p, 0, (i + 1) * kn_per_blk - 1))],
        out_specs=pl.BlockSpec((1, blk, LANES), lambda b, p, i: (b, i, p)),
        compiler_params=_cparams(3),
        name="fox_prompt",
    )(qx, kx, vtb, pair_rows(c2t), pair_rows(c2t), pair_rows(kn))


def _ssd_gate_norm(y, z, g):
    return _rms(y * _silu(z), g)


def _ssd_chunk(xb, z, dt, dtt, prev, state_old, cw, cb, alx, alc, dsx, g_norm, tril, triu):
    q = xb.shape[0]
    row8 = lax.broadcasted_iota(jnp.int32, (8, CONV_CH), 0)
    acc = cb + xb * cw[CONV_WIDTH - 1:CONV_WIDTH, :]
    for back in range(1, CONV_WIDTH):
        sh = pltpu.roll(xb, back, 0)
        head = jnp.where(row8 < back, pltpu.roll(prev, back, 0), sh[0:8])
        shifted = jnp.concatenate([head, sh[8:]], axis=0)
        acc = acc + shifted * cw[CONV_WIDTH - 1 - back:CONV_WIDTH - back, :]
    act = _silu(acc)
    xs = act[:, :D_SSM]
    bmat = act[:, D_SSM:D_SSM + N_BC_GROUPS * N_STATE]
    cmat = act[:, D_SSM + N_BC_GROUPS * N_STATE:]

    a_x = -jnp.exp(alx) * dt
    hi, mid, lo = _split3(a_x)
    parts = jnp.concatenate([hi, mid, lo], axis=1).astype(BF16)
    r = _dot(tril, parts)
    acs_x = r[:, 0:D_SSM] + r[:, D_SSM:2 * D_SSM] + r[:, 2 * D_SSM:3 * D_SSM]
    a_t = -jnp.exp(alc) * dtt
    acs_t = _cumsum_lanes(a_t, triu)

    xdt = xs * dt
    a_last = acs_x[q - 1:q, :]
    xdt_end = (xdt * jnp.exp(a_last - acs_x)).astype(BF16)
    e_acs = jnp.exp(acs_x)
    xdt_b = xdt.astype(BF16)
    state_b = state_old.astype(BF16)

    row = lax.broadcasted_iota(jnp.int32, (q, q), 0)
    col = lax.broadcasted_iota(jnp.int32, (q, q), 1)
    causal = row >= col
    first = lax.broadcasted_iota(jnp.int32, (q, LANES), 1) < P_SSM

    y_parts, state_new = [], []
    for g in range(N_BC_GROUPS):
        cg = cmat[:, g * N_STATE:(g + 1) * N_STATE].astype(BF16)
        bg = bmat[:, g * N_STATE:(g + 1) * N_STATE].astype(BF16)
        cbm = _dot_nt(cg, bg)
        gl = slice(g * GROUP_W, (g + 1) * GROUP_W)
        y_off = _dot(cg, state_b[:, gl]) * e_acs[:, gl]
        for pr in range(GROUP_W // LANES):
            lo_lane = g * GROUP_W + pr * LANES
            pair = []
            for hh in range(2):
                h = lo_lane // P_SSM + hh
                seg = acs_x[:, h * P_SSM:h * P_SSM + 1] - acs_t[h:h + 1, :]
                decay = jnp.exp(jnp.where(causal, seg, -jnp.inf))
                gm = (cbm * decay).astype(BF16)
                pair.append(_dot(gm, xdt_b[:, lo_lane:lo_lane + LANES]))
            y_parts.append(jnp.where(first, pair[0], pair[1]) + y_off[:, pr * LANES:(pr + 1) * LANES])
        state_new.append(jnp.exp(a_last[:, gl]) * state_old[:, gl] + _dot_tn(bg, xdt_end[:, gl]))
    y = jnp.concatenate(y_parts, axis=1) + dsx * xs
    return (_ssd_gate_norm(y, z, g_norm).astype(BF16), xb[q - 8:q], jnp.concatenate(state_new, axis=1))


def _mix_out_kernel(x_ref, att_ref, ssm_ref, wo_ref, g_ref, wq_ref, x1_ref, qc_ref):
    x1 = x_ref[...] + _dot(att_ref[...], wo_ref[0:D_ATT, :]) + _dot(ssm_ref[...], wo_ref[D_ATT:, :])
    x1_ref[...] = x1
    h = _rms(x1, g_ref[...]).astype(BF16)
    qc_ref[...] = (_dot(h, wq_ref[...]) * X_SCALE).astype(BF16)


def _mix_out(x, att, ssm, w_out, g_cross, w_cq, bt):
    t = x.shape[0]
    tok = lambda w: pl.BlockSpec((bt, w), lambda i: (i, 0))
    const = lambda shape: pl.BlockSpec(shape, lambda i: (0,) * len(shape))
    return pl.pallas_call(
        _mix_out_kernel,
        out_shape=(jax.ShapeDtypeStruct((t, D_MODEL), F32), jax.ShapeDtypeStruct((t, D_MODEL), BF16)),
        grid=(t // bt,),
        in_specs=[tok(D_MODEL), tok(D_ATT), tok(D_SSM), const((D_MODEL, D_MODEL)),
                  const((1, D_MODEL)), const((D_MODEL, D_MODEL))],
        out_specs=(tok(D_MODEL), tok(D_MODEL)),
        compiler_params=_cparams(1),
        name="mix_out",
    )(x, att, ssm, w_out, g_cross, w_cq)


def _ffn_kernel(x1_ref, o_ref, wco_ref, g_ref, wg_ref, wu_ref, wd_ref, gf_ref, y_ref):
    x2 = x1_ref[...] + _dot(o_ref[...], wco_ref[...])
    h = _rms(x2, g_ref[...]).astype(BF16)
    u = (_silu(_dot(h, wg_ref[...])) * _dot(h, wu_ref[...])).astype(BF16)
    x3 = x2 + _dot(u, wd_ref[...])
    y_ref[...] = _rms(x3, gf_ref[...])


def _ffn(x1, o, w_co, g_ffn, w_gate, w_up, w_down, g_final, bt):
    t = x1.shape[0]
    d_ff = w_gate.shape[1]
    tok = lambda w: pl.BlockSpec((bt, w), lambda i: (i, 0))
    const = lambda shape: pl.BlockSpec(shape, lambda i: (0,) * len(shape),
                                       pipeline_mode=pl.Buffered(1))
    return pl.pallas_call(
        _ffn_kernel,
        out_shape=jax.ShapeDtypeStruct((t, D_MODEL), F32),
        grid=(t // bt,),
        in_specs=[tok(D_MODEL), tok(D_MODEL), const((D_MODEL, D_MODEL)), const((1, D_MODEL)),
                  const((D_MODEL, d_ff)), const((D_MODEL, d_ff)), const((d_ff, D_MODEL)),
                  const((1, D_MODEL))],
        out_specs=tok(D_MODEL),
        compiler_params=_cparams(1),
        name="cross_out_ffn",
    )(x1, o, w_co, g_ffn, w_gate, w_up, w_down, g_final)


def _prompt_tail_kernel(x_ref, att_ref, ssm_ref, mk_ref, mv_ref, wo_ref, gc_ref, wq_ref, wco_ref, gf_ref,
                        wg_ref, wu_ref, wd_ref, gfin_ref, y_ref):
    x1 = x_ref[0] + _dot(att_ref[0], wo_ref[0:D_ATT, :]) + _dot(ssm_ref[0], wo_ref[D_ATT:, :])
    qc = (_dot(_rms(x1, gc_ref[...]).astype(BF16), wq_ref[...]) * X_SCALE).astype(BF16)
    heads = []
    for h in range(H_X):
        hl = slice(h * HD_X, (h + 1) * HD_X)
        p = _softmax_rows(_dot_nt(qc[:, hl], mk_ref[0, :, hl]))
        heads.append(_dot(p.astype(BF16), mv_ref[0, :, hl]).astype(BF16))
    x2 = x1 + _dot(jnp.concatenate(heads, axis=1), wco_ref[...])
    h2 = _rms(x2, gf_ref[...]).astype(BF16)
    u = (_silu(_dot(h2, wg_ref[...])) * _dot(h2, wu_ref[...])).astype(BF16)
    y_ref[0] = _rms(x2 + _dot(u, wd_ref[...]), gfin_ref[...])


def _prompt_tail(x, att, ssm, mk, mv, w_out, g_cross, w_cq, w_co, g_ffn, w_gate, w_up, w_down, g_final, bt):
    nb, l, _ = x.shape
    m = mk.shape[1]
    d_ff = w_gate.shape[1]
    tok = lambda w: pl.BlockSpec((1, bt, w), lambda b, i: (b, i, 0))
    mem = pl.BlockSpec((1, m, D_MODEL), lambda b, i: (b, 0, 0))
    const = lambda shape: pl.BlockSpec(shape, lambda b, i: (0,) * len(shape), pipeline_mode=pl.Buffered(1))
    return pl.pallas_call(
        _prompt_tail_kernel,
        out_shape=jax.ShapeDtypeStruct((nb, l, D_MODEL), F32),
        grid=(nb, l // bt),
        in_specs=[tok(D_MODEL), tok(D_ATT), tok(D_SSM), mem, mem, const((D_MODEL, D_MODEL)),
                  const((1, D_MODEL)), const((D_MODEL, D_MODEL)), const((D_MODEL, D_MODEL)), const((1, D_MODEL)),
                  const((D_MODEL, d_ff)), const((D_MODEL, d_ff)), const((d_ff, D_MODEL)), const((1, D_MODEL))],
        out_specs=tok(D_MODEL),
        compiler_params=_cparams(2),
        name="prompt_tail",
    )(x, att, ssm, mk, mv, w_out, g_cross, w_cq, w_co, g_ffn, w_gate, w_up, w_down, g_final)


def _memory_kv_kernel(mem_ref, g_ref, wk_ref, wv_ref, k32_ref, v32_ref, kb_ref, vb_ref):
    mn = _rms(mem_ref[0], g_ref[...]).astype(BF16)
    k = _dot(mn, wk_ref[...])
    v = _dot(mn, wv_ref[...])
    k32_ref[0] = k
    v32_ref[0] = v
    kb_ref[0] = k.astype(BF16)
    vb_ref[0] = v.astype(BF16)


def _memory_kv(mem, g, w_ck, w_cv):
    nb, m, _ = mem.shape
    blk = pl.BlockSpec((1, m, D_MODEL), lambda b: (b, 0, 0))
    const = lambda shape: pl.BlockSpec(shape, lambda b: (0,) * len(shape))
    f = jax.ShapeDtypeStruct((nb, m, D_MODEL), F32)
    h = jax.ShapeDtypeStruct((nb, m, D_MODEL), BF16)
    return pl.pallas_call(
        _memory_kv_kernel,
        out_shape=(f, f, h, h),
        grid=(nb,),
        in_specs=[blk, const((1, D_MODEL)), const((D_MODEL, D_MODEL)), const((D_MODEL, D_MODEL))],
        out_specs=(blk, blk, blk, blk),
        compiler_params=_cparams(1),
        name="memory_kv",
    )(mem, g, w_ck, w_cv)


def _softmax_rows(s):
    m = jnp.max(s, axis=-1, keepdims=True)
    p = jnp.exp(s - m)
    return p / jnp.sum(p, axis=-1, keepdims=True)


_X_SUB = 2 * H_X


def _cross_sample_kernel(q_ref, k_ref, v_ref, o_ref, *, lq):
    nsb, n_mem = k_ref.shape[0], k_ref.shape[1]
    n = n_mem * _X_SUB
    rows = lq * _X_SUB
    n_tiles = n // LANES
    cls = lax.broadcasted_iota(jnp.int32, (rows, LANES), 1) % _X_SUB
    own = cls == lax.broadcasted_iota(jnp.int32, (rows, LANES), 0) % _X_SUB
    valid = cls < H_X

    class_steps = (8, 16, 32, 64)

    def per_token(x):
        x3 = x.reshape(lq, _X_SUB, LANES)
        return jnp.broadcast_to(jnp.sum(x3, axis=1, keepdims=True), x3.shape).reshape(rows, LANES)

    seqs = range(nsb)
    s5 = [_dot_nt(q_ref[i], k_ref[i].reshape(n, LANES).astype(BF16)) for i in seqs]
    s_t = [[per_token(jnp.where(own, s5[i][:, j * LANES:(j + 1) * LANES], 0.0)) for j in range(n_tiles)]
           for i in seqs]
    s_t = [[u + pltpu.roll(u, LANES - H_X, 1) for u in s_t[i]] for i in seqs]
    m = [functools.reduce(jnp.maximum, s_t[i]) for i in seqs]
    for sh in class_steps:
        m = [jnp.maximum(x, pltpu.roll(x, sh, 1)) for x in m]
    p_t = [[jnp.where(valid, jnp.exp(s - m[i]), 0.0) for s in s_t[i]] for i in seqs]
    l = [functools.reduce(jnp.add, p_t[i]) for i in seqs]
    for sh in class_steps:
        l = [x + pltpu.roll(x, sh, 1) for x in l]
    inv = [1.0 / jnp.where(valid, l[i], 1.0) for i in seqs]
    pn = [[p * inv[i] for p in p_t[i]] for i in seqs]
    p5 = [jnp.concatenate([jnp.where(own, x + pltpu.roll(x, H_X, 1), 0.0).astype(BF16) for x in pn[i]], axis=1)
          for i in seqs]
    for i in seqs:
        v5 = v_ref[i].reshape(n, LANES).astype(BF16)
        o_ref[i] = _dot(p5[i], v5).astype(o_ref.dtype)


def _cross_sample(q5, mem_k, mem_v, lq, n_seq_blk):
    ns, rows, _ = q5.shape
    m = mem_k.shape[1]
    tok = pl.BlockSpec((n_seq_blk, rows, LANES), lambda i: (i, 0, 0))
    mem = pl.BlockSpec((n_seq_blk, m, _X_SUB, LANES), lambda i: (i, 0, 0, 0))
    return pl.pallas_call(
        functools.partial(_cross_sample_kernel, lq=lq),
        out_shape=jax.ShapeDtypeStruct((ns, rows, LANES), BF16),
        grid=(ns // n_seq_blk,),
        in_specs=[tok, mem, mem],
        out_specs=tok,
        compiler_params=_cparams(1),
        name="cross_sample",
    )(q5, mem_k, mem_v)


def _fox_sample_kernel(pt_ref, q_ref, kn_ref, vn_ref, lfn_ref, tril_ref, lft_ref, *rest, n_pages, lq, nsb):
    per_seq = 2 * n_pages
    k_refs = [rest[u * per_seq:u * per_seq + n_pages] for u in range(nsb)]
    v_refs = [rest[u * per_seq + n_pages:(u + 1) * per_seq] for u in range(nsb)]
    o_ref, kt_ref, vt_ref = rest[nsb * per_seq:nsb * per_seq + 3]
    seqs = range(nsb)
    first = pl.program_id(0) * nsb * n_pages
    lf_pages = [[lft_ref[pt_ref[first + u * n_pages + j]] for j in range(n_pages)] for u in seqs]

    for u in seqs:
        for j in range(n_pages):
            kt_ref[u, :, :, j * PAGE_SIZE:(j + 1) * PAGE_SIZE] = k_refs[u][j][0].astype(BF16)
            vt_ref[u, :, :, j * PAGE_SIZE:(j + 1) * PAGE_SIZE] = v_refs[u][j][0].astype(BF16)

    tril = tril_ref[...]
    lf_all = [jnp.concatenate(lf_pages[u], axis=0) * LOG2E for u in seqs]
    splits = [_split3(x) for x in lf_all]
    incl = [_dot(hi.astype(BF16), tril) + _dot(mid.astype(BF16), tril) + _dot(lo.astype(BF16), tril)
            for hi, mid, lo in splits]
    bias = []
    for u in seqs:
        after = jnp.zeros((H_ATT, 1), F32)
        pieces = [None] * n_pages
        for j in reversed(range(n_pages)):
            sl = slice(j * H_ATT, (j + 1) * H_ATT)
            pieces[j] = incl[u][sl] - lf_all[u][sl] + after
            after = after + incl[u][sl][:, 0:1]
        bias.append(jnp.concatenate(pieces, axis=1))

    lane = lax.broadcasted_iota(jnp.int32, (H_ATT, lq), 1)
    cn = []
    for u in seqs:
        lfn = lfn_ref[u] * LOG2E
        c = jnp.zeros((H_ATT, lq), F32)
        for t in range(lq):
            c = c + jnp.where(lane >= t, lfn[:, t:t + 1], 0.0)
        cn.append(c)
    causal = (lax.broadcasted_iota(jnp.int32, (lq, lq), 1)
              <= lax.broadcasted_iota(jnp.int32, (lq, lq), 0))

    units = [(u, h) for u in seqs for h in range(H_ATT)]
    scores = [(_dot(q_ref[u, h], kt_ref[u, h]), _dot_nt(q_ref[u, h], kn_ref[u, h])) for u, h in units]
    probs = []
    for (u, h), (s, s_new) in zip(units, scores):
        s = s + bias[u][h:h + 1, :]
        s_new = jnp.where(causal, s_new - cn[u][h:h + 1, :], NEG_BIG)
        m = jnp.maximum(jnp.max(s, axis=-1, keepdims=True), jnp.max(s_new, axis=-1, keepdims=True))
        p = jnp.exp2(s - m)
        p_new = jnp.exp2(s_new - m)
        l = jnp.sum(p, axis=-1, keepdims=True) + jnp.sum(p_new, axis=-1, keepdims=True)
        probs.append((p.astype(BF16), p_new.astype(BF16), l))
    for (u, h), (p, p_new, l) in zip(units, probs):
        o = _dot_nt(p, vt_ref[u, h]) + _dot(p_new, vn_ref[u, h])
        o_ref[u, h] = (o / l).astype(o_ref.dtype)


def _fox_sample(qh, kh, vh, lfn, cache_kt, cache_vt, cache_lft, page_table, nsb):
    ns, _, lq, _ = qh.shape
    n_pages = page_table.shape[1]
    past = n_pages * PAGE_SIZE
    tril = jnp.tril(jnp.ones((PAGE_SIZE, PAGE_SIZE), F32)).astype(BF16)
    tok = pl.BlockSpec((nsb, H_ATT, lq, HD_ATT), lambda i, pt: (i, 0, 0, 0))

    def page(u, j, shape):
        return pl.BlockSpec((1,) + shape,
                            lambda i, pt: (pt[(i * nsb + u) * n_pages + j],) + (0,) * len(shape))

    in_specs = [tok, tok, tok, pl.BlockSpec((nsb, H_ATT, lq), lambda i, pt: (i, 0, 0)),
                pl.BlockSpec((PAGE_SIZE, PAGE_SIZE), lambda i, pt: (0, 0)),
                pl.BlockSpec(cache_lft.shape, lambda i, pt: (0, 0, 0), pipeline_mode=pl.Buffered(1))]
    pages = []
    for u in range(nsb):
        in_specs += [page(u, j, (H_ATT, HD_ATT, PAGE_SIZE)) for j in range(n_pages)]
        in_specs += [page(u, j, (H_ATT, HD_ATT, PAGE_SIZE)) for j in range(n_pages)]
        pages += [cache_kt] * n_pages + [cache_vt] * n_pages
    grid_spec = pltpu.PrefetchScalarGridSpec(
        num_scalar_prefetch=1,
        grid=(ns // nsb,),
        in_specs=in_specs,
        out_specs=tok,
        scratch_shapes=[pltpu.VMEM((nsb, H_ATT, HD_ATT, past), BF16),
                        pltpu.VMEM((nsb, H_ATT, HD_ATT, past), BF16)],
    )
    return pl.pallas_call(
        functools.partial(_fox_sample_kernel, n_pages=n_pages, lq=lq, nsb=nsb),
        out_shape=jax.ShapeDtypeStruct((ns, H_ATT, lq, HD_ATT), BF16),
        grid_spec=grid_spec,
        compiler_params=_cparams(1),
        name="fox_sample",
    )(page_table.reshape(-1), qh, kh, vh, lfn, tril, cache_lft, *pages)


def _ssd_sample_kernel(xbc_ref, z_ref, dtx_ref, cs_ref, st_ref, cw_ref, cb_ref, alx_ref, dsx_ref,
                       g_ref, y_ref, sto_ref, xp_ref, rows_ref, brow_ref, *, lq):
    nsb = xbc_ref.shape[0]
    assert nsb * (lq + 1) <= LANES
    kw = CONV_WIDTH - 1
    xp_ref[:, 0:kw, :] = cs_ref[...]
    xp_ref[:, kw:kw + lq, :] = xbc_ref[...]
    acc = cb_ref[...]
    for tap in range(CONV_WIDTH):
        acc = acc + xp_ref[:, tap:tap + lq, :] * cw_ref[tap:tap + 1, :]
    act = _silu(acc)
    xs = act[:, :, :D_SSM]
    bmat = act[:, :, D_SSM:D_SSM + N_BC_GROUPS * N_STATE]
    cmat = act[:, :, D_SSM + N_BC_GROUPS * N_STATE:]
    dt = dtx_ref[...]
    a = -jnp.exp(alx_ref[...]) * dt
    xdt = xs * dt
    acs = [a[:, 0:1, :]]
    for t in range(1, lq):
        acs.append(acs[-1] + a[:, t:t + 1, :])
    a_last = acs[lq - 1]

    ccat = jnp.concatenate([cmat[:, :, g * N_STATE:(g + 1) * N_STATE] for g in range(N_BC_GROUPS)],
                           axis=1).astype(BF16)
    state = st_ref[...]
    r = jnp.einsum("sgn,shn->sgh", ccat, state.astype(BF16), preferred_element_type=F32)
    group0 = lax.broadcasted_iota(jnp.int32, (1, 1, D_SSM), 2) < GROUP_W
    y_off = jnp.where(group0, r[:, 0:lq, :], r[:, lq:2 * lq, :])

    ys = []
    for t in range(lq):
        y_t = jnp.exp(acs[t]) * y_off[:, t:t + 1, :] + dsx_ref[...] * xs[:, t:t + 1, :]
        for s in range(t + 1):
            cb = jnp.concatenate(
                [jnp.broadcast_to(
                    jnp.sum(cmat[:, t:t + 1, g * N_STATE:(g + 1) * N_STATE]
                            * bmat[:, s:s + 1, g * N_STATE:(g + 1) * N_STATE], axis=-1, keepdims=True),
                    (nsb, 1, GROUP_W)) for g in range(N_BC_GROUPS)], axis=2)
            y_t = y_t + cb * jnp.exp(acs[t] - acs[s]) * xdt[:, s:s + 1, :]
        ys.append(y_t)
    y = jnp.concatenate(ys, axis=1)
    y_ref[...] = _ssd_gate_norm(y, z_ref[...], g_ref[...]).astype(y_ref.dtype)

    upd = jnp.concatenate([xdt[:, s:s + 1, :] * jnp.exp(a_last - acs[s]) for s in range(lq)], axis=1)
    e_last = jnp.exp(a_last)
    rows_ref[...] = jnp.zeros_like(rows_ref)
    brow_ref[...] = jnp.zeros_like(brow_ref)
    for i in range(nsb):
        rows_ref[i * lq:(i + 1) * lq, :] = upd[i]
        rows_ref[nsb * lq + i:nsb * lq + i + 1, :] = e_last[i]
        brow_ref[i * lq:(i + 1) * lq, :] = bmat[i]
    cols = rows_ref[...].T
    upd_cols = cols.astype(BF16)
    b_all = brow_ref[...]
    row_seq = lax.broadcasted_iota(jnp.int32, (LANES, N_BC_GROUPS * N_STATE), 0) // lq
    top = lax.broadcasted_iota(jnp.int32, (D_SSM, N_STATE), 0) < GROUP_W
    for i in range(nsb):
        b_rows = jnp.where(row_seq == i, b_all, 0.0).astype(BF16)
        m = _dot(upd_cols, b_rows)
        add = jnp.where(top, m[:, 0:N_STATE], m[:, N_STATE:2 * N_STATE])
        decay = jnp.broadcast_to(cols[:, nsb * lq + i:nsb * lq + i + 1], (D_SSM, N_STATE))
        sto_ref[i] = decay * state[i] + add


def _ssd_sample(xbc, z, dtx, conv_state, state, conv_w, conv_b, a_log_x, d_skip_x, norm_g, nsb):
    ns, lq, _ = xbc.shape
    seq = lambda r, w: pl.BlockSpec((nsb, r, w), lambda i: (i, 0, 0))
    const = lambda shape: pl.BlockSpec(shape, lambda i: (0,) * len(shape))
    return pl.pallas_call(
        functools.partial(_ssd_sample_kernel, lq=lq),
        out_shape=(jax.ShapeDtypeStruct((ns, lq, D_SSM), BF16),
                   jax.ShapeDtypeStruct((ns, D_SSM, N_STATE), F32)),
        grid=(ns // nsb,),
        in_specs=[seq(lq, CONV_CH), seq(lq, D_SSM), seq(lq, D_SSM), seq(CONV_WIDTH - 1, CONV_CH),
                  seq(D_SSM, N_STATE), const((CONV_WIDTH, CONV_CH)), const((1, CONV_CH)),
                  const((1, D_SSM)), const((1, D_SSM)), const((1, D_SSM))],
        out_specs=(seq(lq, D_SSM), seq(D_SSM, N_STATE)),
        scratch_shapes=[pltpu.VMEM((nsb, 8, CONV_CH), F32), pltpu.VMEM((LANES, D_SSM), F32),
                        pltpu.VMEM((LANES, N_BC_GROUPS * N_STATE), F32)],
        compiler_params=_cparams(1),
        name="ssd_sample",
    )(xbc, z, dtx, conv_state, state, conv_w, conv_b, a_log_x, d_skip_x, norm_g)


def _pick(pref, n):
    return pref if n % pref == 0 else n


def kernel(x_prompt, x_sample, mem_prompt, cache_k, cache_v, cache_logf, page_table, cache_mem_k, cache_mem_v, state_conv, state_ssm, norm_mix_g, w_in, b_forget, conv_w, conv_b, dt_bias, a_log, d_skip, ssm_norm_g, w_out, norm_cross_g, norm_mem_g, w_cq, w_ck, w_cv, w_co, norm_ffn_g, w_gate, w_up, w_down, final_norm_g):
    assert w_in.shape[0] == 1, "one layer"
    nb, l, _ = x_prompt.shape
    ns, lq, _ = x_sample.shape
    row = lambda v: v.reshape(1, -1).astype(F32)
    colv = lambda v: v.reshape(-1, 1).astype(F32)
    per_ch = lambda v: jnp.repeat(v.astype(F32), P_SSM).reshape(1, D_SSM)

    w_t = w_in[0].T
    cuts = [D_ATT, 2 * D_ATT, 3 * D_ATT, 3 * D_ATT + H_ATT, 3 * D_ATT + H_ATT + D_SSM,
            3 * D_ATT + H_ATT + D_SSM + CONV_CH]
    w_q, w_k, w_v, w_f, w_z, w_xbc, w_dt = jnp.split(w_t, cuts, axis=0)
    lane_pad = lambda a: jnp.pad(a, ((0, 0), (0, LANES - a.shape[1])))
    w_main_t = jnp.concatenate([w_q, w_k, w_z, w_xbc, jnp.repeat(w_dt, P_SSM, axis=0),
                                jnp.pad(w_f, ((0, LANES - H_ATT), (0, 0)))], axis=0).astype(BF16)
    w_small_t = jnp.concatenate([w_f, w_dt], axis=0).astype(BF16)
    in_params = (row(norm_mix_g[0]), w_main_t, w_v.astype(BF16), w_small_t, colv(b_forget[0]),
                 colv(dt_bias[0]), per_ch(dt_bias[0]), lane_pad(row(b_forget[0])))
    ssd_params = (conv_w[0], row(conv_b[0]), per_ch(a_log[0]))
    ssd_tail = (per_ch(d_skip[0]), row(ssm_norm_g[0]))
    w_out_b, w_cq_b, w_co_b = w_out[0].astype(BF16), w_cq[0].astype(BF16), w_co[0].astype(BF16)
    w_gate_b, w_up_b, w_down_b = w_gate[0].astype(BF16), w_up[0].astype(BF16), w_down[0].astype(BF16)
    g_cross, g_ffn, g_final = row(norm_cross_g[0]), row(norm_ffn_g[0]), row(final_norm_g)

    def tail(x, att, ssm, cross, bt):
        x1, qc = _mix_out(x, att, ssm, w_out_b, g_cross, w_cq_b, _pick(2 * bt, x.shape[0]))
        o = cross(qc)
        return _ffn(x1, o, w_co_b, g_ffn, w_gate_b, w_up_b, w_down_b, g_final, bt)

    bt = _pick(512, l)
    assert bt % SSD_CHUNK == 0
    pp = _in_projection(x_prompt, *in_params, bt, ssd=(*ssd_params, colv(a_log[0]), *ssd_tail))
    k32, v32t, logft, ssm, st_p = pp["k32"], pp["v32t"], pp["logft"], pp["yssm"], pp["state"]
    att = _fox_prompt(pp["qx"], pp["kx"], pp["vtb"], pp["c2t"], pp["kn"], _pick(1024, l), bt)
    mk32, mv32, mkb, mvb = _memory_kv(mem_prompt, row(norm_mem_g[0]), w_ck[0].astype(BF16),
                                      w_cv[0].astype(BF16))
    y_prompt = _prompt_tail(x_prompt, att, ssm, mkb, mvb, w_out_b, g_cross, w_cq_b, w_co_b, g_ffn,
                            w_gate_b, w_up_b, w_down_b, g_final, bt)

    ts = ns * lq
    bts = _pick(512, ts)
    ps = _in_projection(x_sample.reshape(1, ts, D_MODEL), *in_params, bts)
    qx_s, k32_s, vtb_s, v32t_s = ps["qx"], ps["k32"], ps["vtb"], ps["v32t"]
    z_s, xbc_s, dtx_s, logft_s = ps["z"], ps["xbc"], ps["dtx"], ps["logft"]
    seq3 = lambda a: a.reshape(ns, lq, a.shape[-1])
    lfn = logft_s.reshape(H_ATT, ns, lq).transpose(1, 0, 2)
    head_major = lambda a: a.reshape(ns, lq, H_ATT, HD_ATT).transpose(0, 2, 1, 3)
    from_t = lambda a: a.reshape(H_ATT, HD_ATT, ns, lq)
    qb_s = qx_s.reshape(ts, D_ATT // LANES, 2, LANES)[:, :, 0, :]
    att_s = _fox_sample(head_major(qb_s), head_major(k32_s.astype(BF16)),
                        from_t(vtb_s).transpose(2, 0, 3, 1), lfn,
                        cache_k[0].transpose(0, 2, 3, 1), cache_v[0].transpose(0, 2, 3, 1),
                        cache_logf[0].transpose(0, 2, 1), page_table, _pick(2, ns))
    att_s = att_s.transpose(0, 2, 1, 3)
    ssm_s, st_s = _ssd_sample(seq3(xbc_s), seq3(z_s), seq3(dtx_s), state_conv[0],
                              state_ssm[0].reshape(ns, D_SSM, N_STATE), *ssd_params, *ssd_tail,
                              _pick(8, ns))
    n_mem = mem_prompt.shape[1]
    stored = lambda a: a.reshape(ns, -1, H_X, 2, LANES).transpose(0, 1, 3, 2, 4).reshape(ns, -1, _X_SUB, LANES)
    cross_s = lambda qc: _cross_sample(
        stored(qc).reshape(ns, lq * _X_SUB, LANES), stored(cache_mem_k[0]), stored(cache_mem_v[0]), lq,
        _pick(4, ns)).reshape(ns, lq, 2, H_X, LANES).transpose(0, 1, 3, 2, 4).reshape(ts, D_MODEL)
    y_sample = tail(x_sample.reshape(ts, D_MODEL), att_s.reshape(ts, D_ATT), ssm_s.reshape(ts, D_SSM),
                    cross_s, bts).reshape(ns, lq, D_MODEL)

    heads = lambda a, n: a.reshape(1, n, -1, H_ATT, HD_ATT)
    kw = CONV_WIDTH - 1
    return (y_prompt, y_sample,
            heads(k32, nb), v32t.reshape(nb, H_ATT, HD_ATT, l).transpose(0, 3, 1, 2)[None],
            logft.transpose(0, 2, 1)[None],
            mk32.reshape(1, nb, n_mem, H_X, HD_X), mv32.reshape(1, nb, n_mem, H_X, HD_X),
            pp["tail"][:, 8 - kw:, :][None], st_p.reshape(1, nb, H_SSM, P_SSM, N_STATE),
            heads(k32_s, ns), from_t(v32t_s).transpose(2, 3, 0, 1)[None],
            logft_s.reshape(H_ATT, ns, lq).transpose(1, 2, 0)[None],
            seq3(xbc_s)[:, lq - kw:, :][None], st_s.reshape(1, ns, H_SSM, P_SSM, N_STATE))
```
